```python
import math
import jax
import jax.numpy as jnp
from jax import lax
import numpy as np

D_MODEL = 1024
BATCH = 4
SEQ = 4096
DEPTH = 2
DEC_BATCH = 128
DEC_SEQ = 8
PAST_LEN = 2048
PAGE_SIZE = 128

SSD_HEADS = 6
SSD_HEAD_DIM = 64
SSD_INNER = SSD_HEADS * SSD_HEAD_DIM
SSD_GROUPS = 2
SSD_STATE = 128
SSD_CONV = 4
SSD_CONV_DIM = SSD_INNER + 2 * SSD_GROUPS * SSD_STATE
SSD_CHUNK = 64
GLA_HEADS = 4
GLA_HEAD_K = 32
GLA_HEAD_V = 64
GLA_DK = GLA_HEADS * GLA_HEAD_K
GLA_DV = GLA_HEADS * GLA_HEAD_V
GLA_GATE_RANK = 16
GLA_TAU = 16.0
GLA_CHUNK = 16
ATT_HEADS = 6
ATT_HEAD_DIM = 64
ATT_DIM = ATT_HEADS * ATT_HEAD_DIM
DILATION_PATTERNS = ((128, 1), (512, 4), (2048, 16))
ATT_MAX_WINDOW = 2048
ATT_KEYS = DILATION_PATTERNS[0][0] // DILATION_PATTERNS[0][1] + 1
ATT_Q_BLOCK = 64
D_MIX = SSD_INNER + GLA_DV + ATT_DIM
IN_SPLITS = (SSD_INNER, SSD_CONV_DIM, SSD_HEADS,
             GLA_DK, GLA_DK, GLA_DV, GLA_DV, GLA_GATE_RANK,
             ATT_DIM, ATT_DIM, ATT_DIM)
D_IN_PROJ = sum(IN_SPLITS)
D_FF = ((8 * D_MODEL // 3 + 127) // 128) * 128
ADALN_MODS = 9
FFN_RES = 0.5
EPS = 1e-6

kernel_name = 'hybrid_ssd_gla_dilated_macaron_adaln_step'


def rmsnorm(x, w):
    xf = x.astype(jnp.float32)
    y = xf * lax.rsqrt(jnp.mean(xf * xf, axis=-1, keepdims=True) + EPS)
    return (y * w.astype(jnp.float32)).astype(x.dtype)


def modulate(h, shift, scale):
    return h * (1.0 + scale) + shift


def split_cols(h, sizes):
    offs = [int(o) for o in np.cumsum(sizes)[:-1]]
    return jnp.split(h, offs, axis=-1)


def swiglu(h, w_in, w_out):
    gate, up = jnp.split(h @ w_in, 2, axis=-1)
    return (jax.nn.silu(gate) * up) @ w_out


def causal_depthwise_conv(u, buf, w, b):
    full = jnp.concatenate([buf.astype(u.dtype), u], axis=1)
    out = lax.conv_general_dilated(full, w[:, None, :].astype(u.dtype), window_strides=(1,), padding='VALID',
                                   dimension_numbers=('NWC', 'WIO', 'NWC'), feature_group_count=u.shape[-1])
    return out + b, full[:, full.shape[1] - (SSD_CONV - 1):]


def ssd_chunked(x, dt, A, Bm, Cm, h0):
    f32 = jnp.float32
    Bsz, L, H, P = x.shape
    G, N = Bm.shape[2], Bm.shape[3]
    R = H // G
    Q = SSD_CHUNK if L % SSD_CHUNK == 0 else L
    nc = L // Q
    xc = x.astype(f32).reshape(Bsz, nc, Q, G, R, P)
    dtc = dt.reshape(Bsz, nc, Q, G, R)
    Bc = Bm.astype(f32).reshape(Bsz, nc, Q, G, N)
    Cc = Cm.astype(f32).reshape(Bsz, nc, Q, G, N)
    cum = jnp.cumsum(dtc * A.reshape(G, R), axis=2)
    cumT = jnp.moveaxis(cum, 2, -1)
    causal = jnp.tril(jnp.ones((Q, Q), bool))
    decay = jnp.exp(jnp.where(causal, cumT[..., :, None] - cumT[..., None, :], -jnp.inf))
    xdt = xc * dtc[..., None]
    cb = jnp.einsum('bcqgn,bcsgn->bcgqs', Cc, Bc)
    y_diag = jnp.einsum('bcgrqs,bcsgrp->bcqgrp', cb[:, :, :, None] * decay, xdt)
    to_end = jnp.exp(cum[:, :, -1:] - cum)
    s_local = jnp.einsum('bcsgn,bcsgrp->bcgrpn', Bc, xdt * to_end[..., None])
    chunk_decay = jnp.exp(cum[:, :, -1])

    def step(h, inp):
        dec, s = inp
        return dec[..., None, None] * h + s, h

    h_last, h_in = lax.scan(step, h0.astype(f32).reshape(Bsz, G, R, P, N),
                            (jnp.moveaxis(chunk_decay, 1, 0), jnp.moveaxis(s_local, 1, 0)))
    h_in = jnp.moveaxis(h_in, 0, 1)
    y_off = jnp.einsum('bcqgn,bcgrpn->bcqgrp', Cc, h_in) * jnp.exp(cum)[..., None]
    return (y_diag + y_off).reshape(Bsz, L, H, P), h_last.reshape(Bsz, H, P, N)


def gla_chunked(q, k, v, g, S0):
    f32 = jnp.float32
    Bsz, L, H, K = q.shape
    V = v.shape[-1]
    Q = GLA_CHUNK if L % GLA_CHUNK == 0 else L
    nc = L // Q
    qc = q.astype(f32).reshape(Bsz, nc, Q, H, K)
    kc = k.astype(f32).reshape(Bsz, nc, Q, H, K)
    vc = v.astype(f32).reshape(Bsz, nc, Q, H, V)
    Gc = jnp.cumsum(g.reshape(Bsz, nc, Q, H, K), axis=2)
    causal = jnp.tril(jnp.ones((Q, Q), bool))
    dec = jnp.exp(jnp.where(causal[:, :, None, None], Gc[:, :, :, None] - Gc[:, :, None], -jnp.inf))
    att = jnp.einsum('bctshk,bcshk->bchts', qc[:, :, :, None] * dec, kc)
    o_intra = jnp.einsum('bchts,bcshv->bcthv', att, vc)
    s_local = jnp.einsum('bcshk,bcshv->bchkv', kc * jnp.exp(Gc[:, :, -1:] - Gc), vc)
    chunk_decay = jnp.exp(Gc[:, :, -1])

    def step(S, inp):
        dec_c, s = inp
        return dec_c[..., None] * S + s, S

    S_last, S_in = lax.scan(step, S0.astype(f32), (jnp.moveaxis(chunk_decay, 1, 0), jnp.moveaxis(s_local, 1, 0)))
    S_in = jnp.moveaxis(S_in, 0, 1)
    o_inter = jnp.einsum('bcthk,bchkv->bcthv', qc * jnp.exp(Gc), S_in)
    return (o_intra + o_inter).reshape(Bsz, L, H, V), S_last


def dilated_attention(q, k_ctx, v_ctx, n_buf, n_abs):
    f32 = jnp.float32
    Bsz, L, H, Dh = q.shape
    pad = ((0, 0), (ATT_MAX_WINDOW, 0), (0, 0), (0, 0))
    kp = jnp.pad(k_ctx, pad)
    vp = jnp.pad(v_ctx, pad)
    qb = ATT_Q_BLOCK if L % ATT_Q_BLOCK == 0 else L
    nb = L // qb
    strides = jnp.array([d for _, d in DILATION_PATTERNS], jnp.int32)
    dist = strides[:, None] * jnp.arange(ATT_KEYS, dtype=jnp.int32)[None, :]
    scale = ATT_HEAD_DIM ** -0.5

    def block(bi):
        i = bi * qb + jnp.arange(qb, dtype=jnp.int32)
        idx = ATT_MAX_WINDOW + n_buf + i[None, :, None] - dist[:, None, :]
        valid = dist[:, None, :] <= n_abs + i[None, :, None]
        kg = kp[:, idx]
        vg = vp[:, idx]
        qblk = lax.dynamic_slice_in_dim(q, bi * qb, qb, axis=1)
        s = jnp.einsum('bqhd,bpqjhd->bpqhj', qblk, kg).astype(f32) * scale
        s = jnp.where(valid[:, :, None, :], s, -jnp.inf)
        lse = jax.nn.logsumexp(s, axis=-1)
        p = jnp.exp(s - lse[..., None])
        o = jnp.einsum('bpqhj,bpqjhd->bpqhd', p.astype(vg.dtype), vg).astype(f32)
        w = jax.nn.softmax(lse, axis=1)
        return jnp.einsum('bpqh,bpqhd->bqhd', w, o).astype(q.dtype)

    out = lax.map(block, jnp.arange(nb, dtype=jnp.int32))
    return jnp.moveaxis(out, 0, 1).reshape(Bsz, L, H, Dh)


def token_mixers(h, st_ssd, st_conv, st_gla, k_buf, v_buf, n_abs, keep, lp):
    f32 = jnp.float32
    Bsz, L, _ = h.shape
    proj = h @ lp['w_in']
    z, xbc, dt_raw, gq, gk, gv, gr, g_lr, aq, ak, av = split_cols(proj, IN_SPLITS)
    xbc_c, conv_new = causal_depthwise_conv(xbc, st_conv, lp['ssd_conv_w'], lp['ssd_conv_b'])
    xbc_c = jax.nn.silu(xbc_c)
    xs, bs, cs = split_cols(xbc_c, (SSD_INNER, SSD_GROUPS * SSD_STATE, SSD_GROUPS * SSD_STATE))
    xs = xs.reshape(Bsz, L, SSD_HEADS, SSD_HEAD_DIM)
    dt = jax.nn.softplus(dt_raw.astype(f32) + lp['ssd_dt_bias'].astype(f32))
    A = -jnp.exp(lp['ssd_a_log'].astype(f32))
    y, ssd_new = ssd_chunked(xs, dt, A, bs.reshape(Bsz, L, SSD_GROUPS, SSD_STATE),
                             cs.reshape(Bsz, L, SSD_GROUPS, SSD_STATE), st_ssd)
    y = y + lp['ssd_d'].astype(f32)[:, None] * xs.astype(f32)
    y = y.reshape(Bsz, L, SSD_INNER) * jax.nn.silu(z.astype(f32))
    y = rmsnorm(y.reshape(Bsz, L, SSD_GROUPS, SSD_INNER // SSD_GROUPS),
                lp['ssd_norm_w'].reshape(SSD_GROUPS, SSD_INNER // SSD_GROUPS)).reshape(Bsz, L, SSD_INNER).astype(h.dtype)
    q = gq.reshape(Bsz, L, GLA_HEADS, GLA_HEAD_K) * (GLA_HEAD_K ** -0.5)
    k = gk.reshape(Bsz, L, GLA_HEADS, GLA_HEAD_K)
    v = gv.reshape(Bsz, L, GLA_HEADS, GLA_HEAD_V)
    glog = jax.nn.log_sigmoid((g_lr @ lp['gla_w_gate'] + lp['gla_b_gate']).astype(f32)) / GLA_TAU
    o, gla_new = gla_chunked(q, k, v, glog.reshape(Bsz, L, GLA_HEADS, GLA_HEAD_K), st_gla)
    o = rmsnorm(o, lp['gla_norm_w']) * jax.nn.silu(gr.astype(f32)).reshape(Bsz, L, GLA_HEADS, GLA_HEAD_V)
    o = o.reshape(Bsz, L, GLA_DV).astype(h.dtype)
    aq = aq.reshape(Bsz, L, ATT_HEADS, ATT_HEAD_DIM)
    ak = ak.reshape(Bsz, L, ATT_HEADS, ATT_HEAD_DIM)
    av = av.reshape(Bsz, L, ATT_HEADS, ATT_HEAD_DIM)
    k_ctx = jnp.concatenate([k_buf.astype(ak.dtype), ak], axis=1)
    v_ctx = jnp.concatenate([v_buf.astype(av.dtype), av], axis=1)
    att = dilated_attention(aq, k_ctx, v_ctx, k_buf.shape[1], n_abs).reshape(Bsz, L, ATT_DIM).astype(h.dtype)
    mix = jnp.concatenate([y, o, att], axis=-1) @ lp['w_out']
    new_state = (ssd_new.astype(h.dtype), conv_new, gla_new.astype(h.dtype), ak[:, L - keep:], av[:, L - keep:])
    return mix, new_state


def run_trunk(x, c, st_ssd, st_conv, st_gla, k_buf, v_buf, n_abs, keep, params, norm_f):
    outs = ([], [], [], [], [])
    for l in range(DEPTH):
        lp = {name: arr[l] for name, arr in params.items()}
        mod = (jax.nn.silu(c) @ lp['w_mod'] + lp['b_mod']).reshape(c.shape[0], ADALN_MODS, 1, D_MODEL)
        h = modulate(rmsnorm(x, lp['norm_w'][0]), mod[:, 0], mod[:, 1])
        x = x + FFN_RES * mod[:, 2] * swiglu(h, lp['ffn1_w_in'], lp['ffn1_w_out'])
        h = modulate(rmsnorm(x, lp['norm_w'][1]), mod[:, 3], mod[:, 4])
        mix, new = token_mixers(h, st_ssd[l], st_conv[l], st_gla[l], k_buf[l], v_buf[l], n_abs, keep, lp)
        x = x + mod[:, 5] * mix
        h = modulate(rmsnorm(x, lp['norm_w'][2]), mod[:, 6], mod[:, 7])
        x = x + FFN_RES * mod[:, 8] * swiglu(h, lp['ffn2_w_in'], lp['ffn2_w_out'])
        for acc, s in zip(outs, new):
            acc.append(s)
    return rmsnorm(x, norm_f), [jnp.stack(acc) for acc in outs]


def setup_inputs(seed: int = 0) -> dict:
    key = jax.random.key(seed)
    ks = iter(jax.random.split(key, 40))
    f32 = jnp.float32

    def nrm(shape, scale):
        return jax.random.normal(next(ks), shape, f32) * scale

    win_buf = min(ATT_MAX_WINDOW, PAST_LEN)
    dt0 = jnp.exp(jax.random.uniform(next(ks), (DEPTH, SSD_HEADS), f32, math.log(1e-3), math.log(1e-1)))
    ssd_dt_bias = dt0 + jnp.log(-jnp.expm1(-dt0))
    ssd_a_log = jnp.log(jax.random.uniform(next(ks), (DEPTH, SSD_HEADS), f32, 1.0, 16.0))
    return {
        'x_prompt': nrm((BATCH, SEQ, D_MODEL), 1.0),
        'x_sample': nrm((DEC_BATCH, DEC_SEQ, D_MODEL), 1.0),
        'c_prompt': nrm((BATCH, D_MODEL), 1.0),
        'c_sample': nrm((DEC_BATCH, D_MODEL), 1.0),
        'state_ssd': nrm((DEPTH, DEC_BATCH, SSD_HEADS, SSD_HEAD_DIM, SSD_STATE), 0.1),
        'state_ssd_conv': nrm((DEPTH, DEC_BATCH, SSD_CONV - 1, SSD_CONV_DIM), 1.0),
        'state_gla': nrm((DEPTH, DEC_BATCH, GLA_HEADS, GLA_HEAD_K, GLA_HEAD_V), 1.0),
        'cache_attn_k': nrm((DEPTH, DEC_BATCH, win_buf, ATT_HEADS, ATT_HEAD_DIM), 1.0),
        'cache_attn_v': nrm((DEPTH, DEC_BATCH, win_buf, ATT_HEADS, ATT_HEAD_DIM), 1.0),
        'w_in': nrm((DEPTH, D_MODEL, D_IN_PROJ), D_MODEL ** -0.5),
        'w_out': nrm((DEPTH, D_MIX, D_MODEL), D_MIX ** -0.5),
        'ssd_conv_w': nrm((DEPTH, SSD_CONV, SSD_CONV_DIM), SSD_CONV ** -0.5),
        'ssd_conv_b': nrm((DEPTH, SSD_CONV_DIM), 0.02),
        'ssd_dt_bias': ssd_dt_bias,
        'ssd_a_log': ssd_a_log,
        'ssd_d': 1.0 + nrm((DEPTH, SSD_HEADS), 0.1),
        'ssd_norm_w': 1.0 + nrm((DEPTH, SSD_INNER), 0.1),
        'gla_w_gate': nrm((DEPTH, GLA_GATE_RANK, GLA_DK), GLA_GATE_RANK ** -0.5),
        'gla_b_gate': nrm((DEPTH, GLA_DK), 0.1),
        'gla_norm_w': 1.0 + nrm((DEPTH, GLA_HEAD_V), 0.1),
        'norm_w': 1.0 + nrm((DEPTH, 3, D_MODEL), 0.1),
        'w_mod': nrm((DEPTH, D_MODEL, ADALN_MODS * D_MODEL), 0.5 * D_MODEL ** -0.5),
        'b_mod': nrm((DEPTH, ADALN_MODS * D_MODEL), 0.02),
        'ffn1_w_in': nrm((DEPTH, D_MODEL, 2 * D_FF), D_MODEL ** -0.5),
        'ffn1_w_out': nrm((DEPTH, D_FF, D_MODEL), D_FF ** -0.5),
        'ffn2_w_in': nrm((DEPTH, D_MODEL, 2 * D_FF), D_MODEL ** -0.5),
        'ffn2_w_out': nrm((DEPTH, D_FF, D_MODEL), D_FF ** -0.5),
        'norm_f': 1.0 + nrm((D_MODEL,), 0.1),
    }


def reference(x_prompt, x_sample, c_prompt, c_sample, state_ssd, state_ssd_conv, state_gla, cache_attn_k, cache_attn_v,
              w_in, w_out, ssd_conv_w, ssd_conv_b, ssd_dt_bias, ssd_a_log, ssd_d, ssd_norm_w,
              gla_w_gate, gla_b_gate, gla_norm_w, norm_w, w_mod, b_mod,
              ffn1_w_in, ffn1_w_out, ffn2_w_in, ffn2_w_out, norm_f):
    params = dict(w_in=w_in, w_out=w_out, ssd_conv_w=ssd_conv_w, ssd_conv_b=ssd_conv_b, ssd_dt_bias=ssd_dt_bias,
                  ssd_a_log=ssd_a_log, ssd_d=ssd_d, ssd_norm_w=ssd_norm_w, gla_w_gate=gla_w_gate,
                  gla_b_gate=gla_b_gate, gla_norm_w=gla_norm_w, norm_w=norm_w, w_mod=w_mod, b_mod=b_mod,
                  ffn1_w_in=ffn1_w_in, ffn1_w_out=ffn1_w_out, ffn2_w_in=ffn2_w_in, ffn2_w_out=ffn2_w_out)
    dtp = x_prompt.dtype
    p_ssd0 = jnp.zeros((DEPTH, BATCH, SSD_HEADS, SSD_HEAD_DIM, SSD_STATE), dtp)
    p_conv0 = jnp.zeros((DEPTH, BATCH, SSD_CONV - 1, SSD_CONV_DIM), dtp)
    p_gla0 = jnp.zeros((DEPTH, BATCH, GLA_HEADS, GLA_HEAD_K, GLA_HEAD_V), dtp)
    p_kv0 = jnp.zeros((DEPTH, BATCH, 0, ATT_HEADS, ATT_HEAD_DIM), dtp)
    y_prompt, (ssd_p, conv_p, gla_p, k_p, v_p) = run_trunk(
        x_prompt, c_prompt, p_ssd0, p_conv0, p_gla0, p_kv0, p_kv0, 0, min(ATT_MAX_WINDOW, SEQ), params, norm_f)
    y_sample, (ssd_s, conv_s, gla_s, k_s, v_s) = run_trunk(
        x_sample, c_sample, state_ssd, state_ssd_conv, state_gla, cache_attn_k, cache_attn_v,
        PAST_LEN, DEC_SEQ, params, norm_f)
    return (y_prompt, y_sample, ssd_p, ssd_s, conv_p, conv_s, gla_p, gla_s, k_p, k_s, v_p, v_s)
```

```python
import functools
import math

import numpy as np
import jax
import jax.numpy as jnp
from jax import lax
from jax.experimental import pallas as pl
from jax.experimental.pallas import tpu as pltpu

F32 = jnp.float32
BF16 = jnp.bfloat16

D_MODEL = 1024
DEPTH = 2
SSD_HEADS = 6
SSD_HEAD_DIM = 64
SSD_INNER = SSD_HEADS * SSD_HEAD_DIM
SSD_GROUPS = 2
SSD_STATE = 128
SSD_CONV = 4
SSD_CONV_DIM = SSD_INNER + 2 * SSD_GROUPS * SSD_STATE
GLA_HEADS = 4
GLA_HEAD_K = 32
GLA_HEAD_V = 64
GLA_DK = GLA_HEADS * GLA_HEAD_K
GLA_DV = GLA_HEADS * GLA_HEAD_V
GLA_GATE_RANK = 16
GLA_TAU = 16.0
ATT_HEADS = 6
ATT_HEAD_DIM = 64
ATT_DIM = ATT_HEADS * ATT_HEAD_DIM
DILATION_PATTERNS = ((128, 1), (512, 4), (2048, 16))
ATT_MAX_WINDOW = 2048
ATT_KEYS = 129
D_MIX = SSD_INNER + GLA_DV + ATT_DIM
IN_SPLITS = (SSD_INNER, SSD_CONV_DIM, SSD_HEADS, GLA_DK, GLA_DK, GLA_DV, GLA_DV, GLA_GATE_RANK,
             ATT_DIM, ATT_DIM, ATT_DIM)
D_FF = 2816
ADALN_MODS = 9
FFN_RES = 0.5
EPS = 1e-6

LANES = 128
SUBLANES = 8
VMEM_LIMIT = 56 * 1024 * 1024

PROJ_GROUPS = (("z", SSD_INNER), ("xbc", SSD_CONV_DIM), ("gq", GLA_DK), ("gk", GLA_DK), ("gv", GLA_DV),
               ("gr", GLA_DV), ("aq", ATT_DIM), ("ak", ATT_DIM), ("av", ATT_DIM), ("small", LANES))
PROJ_WIDTH = sum(w for _, w in PROJ_GROUPS)
SMALL_DT_OFF = 0
SMALL_GLR_OFF = 8

ROW_TILE = 512
FF_CHUNK = 256
SSD_CHUNK = 128
GLA_CHUNK = 128
GLA_DIAG = 16
SAMPLE_SEQS = 16
ATT_BLOCK = 128
NEG = -1e30


def _dot(a, b):
    return jnp.dot(a, b, preferred_element_type=F32)


def _dot_nt(a, b):
    return lax.dot_general(a, b, (((1,), (1,)), ((), ())), preferred_element_type=F32)


def _sigmoid(x):
    return 1.0 / (1.0 + jnp.exp(-x))


def _silu(x):
    return x * _sigmoid(x)


def _softplus(x):
    return jnp.maximum(x, 0.0) + jnp.log1p(jnp.exp(-jnp.abs(x)))


def _split3_dot(m01, a):
    a1 = a.astype(BF16)
    r1 = a - a1.astype(F32)
    a2 = r1.astype(BF16)
    a3 = (r1 - a2.astype(F32)).astype(BF16)
    return _dot(m01, a1) + _dot(m01, a2) + _dot(m01, a3)


def _split2_dot(a, m01):
    a1 = a.astype(BF16)
    a2 = (a - a1.astype(F32)).astype(BF16)
    return _dot(a1, m01) + _dot(a2, m01)


def _rms_mod(x, nw, shift, scale):
    ms = jnp.mean(x * x, axis=-1, keepdims=True)
    y = x * lax.rsqrt(ms + EPS) * nw
    return y * (1.0 + scale) + shift


def _iota2(shape, axis):
    return lax.broadcasted_iota(jnp.int32, shape, axis)


def _params(sem):
    return pltpu.CompilerParams(dimension_semantics=sem, vmem_limit_bytes=VMEM_LIMIT)


def _const_spec(shape):
    nd = len(shape)
    return pl.BlockSpec(shape, lambda *_: (0,) * nd, pipeline_mode=pl.Buffered(1))


def _mod_kernel(c_ref, w_ref, b_ref, o_ref):
    c = c_ref[...]
    o_ref[...] = _dot(_silu(c).astype(BF16), w_ref[...]) + b_ref[...]


def _modulation(c_all, w_mod, b_mod):
    n, d = c_all.shape
    nout = w_mod.shape[1]
    tn = D_MODEL
    return pl.pallas_call(
        _mod_kernel,
        grid=(nout // tn,),
        in_specs=[pl.BlockSpec((n, d), lambda j: (0, 0)),
                  pl.BlockSpec((d, tn), lambda j: (0, j)),
                  pl.BlockSpec((1, tn), lambda j: (0, j))],
        out_specs=pl.BlockSpec((n, tn), lambda j: (0, j)),
        out_shape=jax.ShapeDtypeStruct((n, nout), F32),
        compiler_params=_params(("arbitrary",)),
        name="adaln_mod",
    )(c_all, w_mod, b_mod.reshape(1, nout))


def _row_tiling(b, l):
    if l >= ROW_TILE:
        assert l % ROW_TILE == 0
        return 1, ROW_TILE
    assert ROW_TILE % l == 0 and b % (ROW_TILE // l) == 0
    return ROW_TILE // l, l


def _x_spec(bb, ll, nlb):
    return pl.BlockSpec((bb, ll, D_MODEL), lambda i: (i // nlb, i % nlb, 0))


def _mod_spec(k, bb, nlb):
    return pl.BlockSpec((1, bb, 1, D_MODEL), lambda i: (k, i // nlb, 0, 0))


def _rows_spec(r, width):
    return pl.BlockSpec((r, width), lambda i: (i, 0))


def _ffn_kernel(*refs, premix, final):
    refs = list(refs)
    x_ref = refs.pop(0)
    if premix:
        y_ref, o_ref, a_ref, g2_ref, wo_ref = refs[:5]
        refs = refs[5:]
    sh_ref, sc_ref, gt_ref, nw_ref, win_ref, wout_ref = refs[:6]
    refs = refs[6:]
    if final:
        nf_ref = refs.pop(0)
    out_ref, act_ref = refs
    bb, ll, d = x_ref.shape
    r = bb * ll
    x = x_ref[...]
    if premix:
        mix = (_dot(y_ref[...], wo_ref[0:SSD_INNER, :])
               + _dot(o_ref[...], wo_ref[SSD_INNER:SSD_INNER + GLA_DV, :])
               + _dot(a_ref[...], wo_ref[SSD_INNER + GLA_DV:D_MIX, :]))
        x = x + g2_ref[0] * mix.reshape(bb, ll, d)
    h = _rms_mod(x, nw_ref[...], sh_ref[0], sc_ref[0]).reshape(r, d).astype(BF16)
    for c in range(D_FF // FF_CHUNK):
        g = _dot(h, win_ref[:, c * FF_CHUNK:(c + 1) * FF_CHUNK])
        u = _dot(h, win_ref[:, D_FF + c * FF_CHUNK:D_FF + (c + 1) * FF_CHUNK])
        act_ref[:, c * FF_CHUNK:(c + 1) * FF_CHUNK] = (_silu(g) * u).astype(BF16)
    y = _dot(act_ref[...], wout_ref[...])
    x = x + FFN_RES * gt_ref[0] * y.reshape(bb, ll, d)
    if final:
        ms = jnp.mean(x * x, axis=-1, keepdims=True)
        x = x * lax.rsqrt(ms + EPS) * nf_ref[...]
    out_ref[...] = x


def _ffn(x, mod, mod_base, norm_w, w_in, w_out, premix=None, final_norm=None):
    b, l, d = x.shape
    bb, ll = _row_tiling(b, l)
    nlb = l // ll
    r = bb * ll
    nsteps = (b // bb) * nlb
    args, specs = [x], [_x_spec(bb, ll, nlb)]
    if premix is not None:
        y, o, a, gate_row, wo = premix
        args += [y, o, a, mod, wo]
        specs += [_rows_spec(r, SSD_INNER), _rows_spec(r, GLA_DV), _rows_spec(r, ATT_DIM),
                  _mod_spec(gate_row, bb, nlb), _const_spec((D_MIX, d))]
    args += [mod, mod, mod, norm_w.reshape(1, d), w_in, w_out]
    specs += [_mod_spec(mod_base, bb, nlb), _mod_spec(mod_base + 1, bb, nlb), _mod_spec(mod_base + 2, bb, nlb),
              _const_spec((1, d)), _const_spec((d, 2 * D_FF)), _const_spec((D_FF, d))]
    if final_norm is not None:
        args.append(final_norm.reshape(1, d))
        specs.append(_const_spec((1, d)))
    return pl.pallas_call(
        functools.partial(_ffn_kernel, premix=premix is not None, final=final_norm is not None),
        grid=(nsteps,),
        in_specs=specs,
        out_specs=_x_spec(bb, ll, nlb),
        out_shape=jax.ShapeDtypeStruct((b, l, d), F32),
        scratch_shapes=[pltpu.VMEM((r, D_FF), BF16)],
        compiler_params=_params(("arbitrary",)),
        name="ffn",
    )(*args)


def _inproj_kernel(x_ref, sh_ref, sc_ref, nw_ref, w_ref, *out_refs):
    bb, ll, d = x_ref.shape
    h = _rms_mod(x_ref[...], nw_ref[...], sh_ref[0], sc_ref[0]).reshape(bb * ll, d).astype(BF16)
    off = 0
    for ref, (_, width) in zip(out_refs, PROJ_GROUPS):
        ref[...] = _dot(h, w_ref[:, off:off + width])
        off += width


def _inproj(x, mod, norm_w, w_in_p):
    b, l, d = x.shape
    bb, ll = _row_tiling(b, l)
    nlb = l // ll
    r = bb * ll
    return pl.pallas_call(
        _inproj_kernel,
        grid=((b // bb) * nlb,),
        in_specs=[_x_spec(bb, ll, nlb), _mod_spec(3, bb, nlb), _mod_spec(4, bb, nlb),
                  _const_spec((1, d)), _const_spec((d, PROJ_WIDTH))],
        out_specs=[_rows_spec(r, w) for _, w in PROJ_GROUPS],
        out_shape=[jax.ShapeDtypeStruct((b * l, w), F32) for _, w in PROJ_GROUPS],
        compiler_params=_params(("arbitrary",)),
        name="inproj",
    )(x, mod, mod, norm_w.reshape(1, d), w_in_p)


def _ssd_prepare(conv, small, dtb, alog, seg01, seglast01):
    xc = _silu(conv)
    xs = xc[:, 0:SSD_INNER]
    bm = xc[:, SSD_INNER:SSD_INNER + SSD_GROUPS * SSD_STATE]
    cm = xc[:, SSD_INNER + SSD_GROUPS * SSD_STATE:SSD_CONV_DIM]
    dt = _softplus(small + dtb)
    a = dt * (-jnp.exp(alog))
    cum = _split3_dot(seg01, a)
    if seglast01 is None:
        cum_last = jnp.broadcast_to(cum[cum.shape[0] - 1:, :], cum.shape)
    else:
        cum_last = _split3_dot(seglast01, cum)
    return xs, bm, cm, dt, cum, cum_last


def _ssd_diag(xs, bm, cm, dt, cum, cum_t, mask):
    ydiag, xdt = [], []
    cb = [_dot_nt(cm[:, g * SSD_STATE:(g + 1) * SSD_STATE].astype(BF16),
                  bm[:, g * SSD_STATE:(g + 1) * SSD_STATE].astype(BF16)) for g in range(SSD_GROUPS)]
    for h in range(SSD_HEADS):
        g = h // (SSD_HEADS // SSD_GROUPS)
        diff = cum[:, h:h + 1] - cum_t[h:h + 1, :]
        decay = jnp.exp(jnp.where(mask, diff, NEG))
        xh = xs[:, h * SSD_HEAD_DIM:(h + 1) * SSD_HEAD_DIM] * dt[:, h:h + 1]
        xdt.append(xh)
        ydiag.append(_dot((cb[g] * decay).astype(BF16), xh.astype(BF16)))
    return ydiag, xdt


def _ssd_finish(y, xs, z, dvec, normw):
    y = (y + dvec * xs) * _silu(z)
    sq = y * y
    half = SSD_INNER // SSD_GROUPS
    lane = _iota2(y.shape, 1)
    s0 = jnp.sum(jnp.where(lane < half, sq, 0.0), axis=-1, keepdims=True)
    s1 = jnp.sum(jnp.where(lane >= half, sq, 0.0), axis=-1, keepdims=True)
    ms = jnp.where(lane < half, s0, s1) * (1.0 / half)
    return y * lax.rsqrt(ms + EPS) * normw


def _ssd_prompt_kernel(z_ref, xbc_ref, small_ref, cw_ref, cb_ref, dtb_ref, alog_ref, dvec_ref, nw_ref,
                       y_ref, st_ref, h_scr, tail_scr):
    c = pl.program_id(1)
    rows = xbc_ref.shape[0]

    @pl.when(c == 0)
    def _():
        h_scr[...] = jnp.zeros_like(h_scr)
        tail_scr[...] = jnp.zeros_like(tail_scr)

    u = xbc_ref[...]
    tail = tail_scr[...]
    row8 = _iota2((SUBLANES, SSD_CONV_DIM), 0)
    conv = u * cw_ref[SSD_CONV - 1:SSD_CONV, :] + cb_ref[...]
    for j in range(1, SSD_CONV):
        uj = pltpu.roll(u, j, axis=0)
        back = (SUBLANES - (SSD_CONV - 1) + j) % SUBLANES
        tj = pltpu.roll(tail, back, axis=0) if back else tail
        top = jnp.where(row8 < j, tj, uj[0:SUBLANES, :])
        uj = jnp.concatenate([top, uj[SUBLANES:, :]], axis=0)
        conv = conv + uj * cw_ref[SSD_CONV - 1 - j:SSD_CONV - j, :]
    tail_scr[...] = pltpu.roll(u[rows - SUBLANES:, :], SSD_CONV - 1, axis=0)

    qi = _iota2((rows, rows), 0)
    si = _iota2((rows, rows), 1)
    mask = si <= qi
    seg01 = jnp.where(mask, 1.0, 0.0).astype(BF16)
    xs, bm, cm, dt, cum, cum_last = _ssd_prepare(conv, small_ref[...], dtb_ref[...], alog_ref[...], seg01, None)
    cum_t = cum.T
    ydiag, xdt = _ssd_diag(xs, bm, cm, dt, cum, cum_t, mask)
    to_end_t = jnp.exp(cum_last - cum).T
    xdt_t = jnp.concatenate(xdt, axis=1).T
    ys = []
    for h in range(SSD_HEADS):
        g = h // (SSD_HEADS // SSD_GROUPS)
        hprev = h_scr[h]
        cg = cm[:, g * SSD_STATE:(g + 1) * SSD_STATE].astype(BF16)
        yoff = _dot_nt(cg, hprev.astype(BF16)) * jnp.exp(cum[:, h:h + 1])
        ys.append(ydiag[h] + yoff)
        xw_t = xdt_t[h * SSD_HEAD_DIM:(h + 1) * SSD_HEAD_DIM, :] * to_end_t[h:h + 1, :]
        s_local = _dot(xw_t.astype(BF16), bm[:, g * SSD_STATE:(g + 1) * SSD_STATE].astype(BF16))
        h_scr[h] = jnp.exp(cum_last[0:1, h:h + 1]) * hprev + s_local
    y = _ssd_finish(jnp.concatenate(ys, axis=1), xs, z_ref[...], dvec_ref[...], nw_ref[...])
    y_ref[...] = y.astype(y_ref.dtype)

    @pl.when(c == pl.num_programs(1) - 1)
    def _():
        st_ref[0] = h_scr[...]


def _ssd_vec_specs():
    return [_const_spec((SSD_CONV, SSD_CONV_DIM)), _const_spec((1, SSD_CONV_DIM)), _const_spec((1, LANES)),
            _const_spec((1, LANES)), _const_spec((1, SSD_INNER)), _const_spec((1, SSD_INNER))]


def _ssd_prompt(z, xbc, small, lp, b, l):
    nc = l // SSD_CHUNK
    rs = lambda w: pl.BlockSpec((SSD_CHUNK, w), lambda bi, ci: (bi * nc + ci, 0))
    return pl.pallas_call(
        _ssd_prompt_kernel,
        grid=(b, nc),
        in_specs=[rs(SSD_INNER), rs(SSD_CONV_DIM), rs(LANES)] + _ssd_vec_specs(),
        out_specs=[rs(SSD_INNER),
                   pl.BlockSpec((1, SSD_HEADS, SSD_HEAD_DIM, SSD_STATE), lambda bi, ci: (bi, 0, 0, 0))],
        out_shape=[jax.ShapeDtypeStruct((b * l, SSD_INNER), BF16),
                   jax.ShapeDtypeStruct((b, SSD_HEADS, SSD_HEAD_DIM, SSD_STATE), F32)],
        scratch_shapes=[pltpu.VMEM((SSD_HEADS, SSD_HEAD_DIM, SSD_STATE), F32),
                        pltpu.VMEM((SUBLANES, SSD_CONV_DIM), F32)],
        compiler_params=_params(("arbitrary", "arbitrary")),
        name="ssd_prompt",
    )(z, xbc, small, lp["conv_w"], lp["conv_b"], lp["dtb"], lp["alog"], lp["dvec"], lp["ssd_nw"])


def _ssd_sample_kernel(z_ref, xbc_ref, small_ref, buf_ref, h0_ref, cw_ref, cb_ref, dtb_ref, alog_ref, dvec_ref,
                       nw_ref, y_ref, st_ref, *, seq):
    rows = xbc_ref.shape[0]
    nseq = rows // seq
    u = xbc_ref[...]
    bufp = buf_ref[...]
    tpos = _iota2((rows, SSD_CONV_DIM), 0) % seq
    conv = u * cw_ref[SSD_CONV - 1:SSD_CONV, :] + cb_ref[...]
    for j in range(1, SSD_CONV):
        uj = pltpu.roll(u, j, axis=0)
        back = (rows - (SSD_CONV - 1 - j)) % rows
        bj = pltpu.roll(bufp, back, axis=0) if back else bufp
        conv = conv + jnp.where(tpos < j, bj, uj) * cw_ref[SSD_CONV - 1 - j:SSD_CONV - j, :]

    qi = _iota2((rows, rows), 0)
    si = _iota2((rows, rows), 1)
    same = (qi // seq) == (si // seq)
    mask = same & (si <= qi)
    seg01 = jnp.where(mask, 1.0, 0.0).astype(BF16)
    last01 = jnp.where(si == (qi // seq) * seq + (seq - 1), 1.0, 0.0).astype(BF16)
    xs, bm, cm, dt, cum, cum_last = _ssd_prepare(conv, small_ref[...], dtb_ref[...], alog_ref[...], seg01, last01)
    cum_t = cum.T
    cum_last_t = cum_last.T
    ydiag, xdt = _ssd_diag(xs, bm, cm, dt, cum, cum_t, mask)
    to_end_t = jnp.exp(cum_last_t - cum_t)
    xdt_t = jnp.concatenate(xdt, axis=1).T
    hpg = SSD_HEADS // SSD_GROUPS
    grows = hpg * SSD_HEAD_DIM
    colseq = _iota2((grows, rows), 1) // seq
    yoff_t = []
    for g in range(SSD_GROUPS):
        cg = cm[:, g * SSD_STATE:(g + 1) * SSD_STATE].astype(BF16)
        bg = bm[:, g * SSD_STATE:(g + 1) * SSD_STATE].astype(BF16)
        xw_t = jnp.concatenate(
            [xdt_t[h * SSD_HEAD_DIM:(h + 1) * SSD_HEAD_DIM, :] * to_end_t[h:h + 1, :]
             for h in range(g * hpg, (g + 1) * hpg)], axis=0)
        acc = jnp.zeros((grows, rows), F32)
        for b in range(nseq):
            h0 = h0_ref[b, g * hpg:(g + 1) * hpg].reshape(grows, SSD_STATE)
            acc = jnp.where(colseq == b, _dot_nt(h0.astype(BF16), cg), acc)
            s_local = _dot(jnp.where(colseq == b, xw_t, 0.0).astype(BF16), bg)
            for hh in range(hpg):
                h = g * hpg + hh
                dec = jnp.exp(cum_last_t[h:h + 1, b * seq:b * seq + 1])
                st_ref[b, h] = (dec * h0[hh * SSD_HEAD_DIM:(hh + 1) * SSD_HEAD_DIM, :]
                                + s_local[hh * SSD_HEAD_DIM:(hh + 1) * SSD_HEAD_DIM, :])
        for hh in range(hpg):
            h = g * hpg + hh
            yoff_t.append(acc[hh * SSD_HEAD_DIM:(hh + 1) * SSD_HEAD_DIM, :] * jnp.exp(cum_t[h:h + 1, :]))
    yoff = jnp.concatenate(yoff_t, axis=0).T
    y = _ssd_finish(jnp.concatenate(ydiag, axis=1) + yoff, xs, z_ref[...], dvec_ref[...], nw_ref[...])
    y_ref[...] = y.astype(y_ref.dtype)


def _ssd_sample(z, xbc, small, bufp, h0, lp, b, l):
    rows = SAMPLE_SEQS * l
    rs = lambda w: pl.BlockSpec((rows, w), lambda i: (i, 0))
    st = pl.BlockSpec((SAMPLE_SEQS, SSD_HEADS, SSD_HEAD_DIM, SSD_STATE), lambda i: (i, 0, 0, 0))
    return pl.pallas_call(
        functools.partial(_ssd_sample_kernel, seq=l),
        grid=(b // SAMPLE_SEQS,),
        in_specs=[rs(SSD_INNER), rs(SSD_CONV_DIM), rs(LANES), rs(SSD_CONV_DIM), st] + _ssd_vec_specs(),
        out_specs=[rs(SSD_INNER), st],
        out_shape=[jax.ShapeDtypeStruct((b * l, SSD_INNER), BF16),
                   jax.ShapeDtypeStruct((b, SSD_HEADS, SSD_HEAD_DIM, SSD_STATE), F32)],
        compiler_params=_params(("arbitrary",)),
        name="ssd_sample",
    )(z, xbc, small, bufp, h0, lp["conv_w"], lp["conv_b"], lp["dtb"], lp["alog"], lp["dvec"], lp["ssd_nw"])


def _gla_consts():
    rk = np.arange(GLA_DK)[:, None] // GLA_HEAD_K
    cv = np.arange(GLA_DV)[None, :] // GLA_HEAD_V
    expand = (rk == cv).astype(np.float32)
    rv = np.arange(GLA_DV)[:, None] // GLA_HEAD_V
    seg = (rv == cv).astype(np.float32) / GLA_HEAD_V
    return jnp.asarray(expand, BF16), jnp.asarray(seg, BF16)


def _gla_prepare(gq, gk, small, wg, bg, seg01):
    glin = _dot(small.astype(BF16), wg) + bg
    g = -_softplus(-glin) * (1.0 / GLA_TAU)
    gc = _split3_dot(seg01, g)
    q = gq * (GLA_HEAD_K ** -0.5)
    return q, gk, gc


def _gla_pairwise(q, k, v, gc, expand, diag):
    rows = q.shape[0]
    nb = rows // diag
    q4 = q.reshape(nb, diag, 1, GLA_DK)
    g4 = gc.reshape(nb, diag, 1, GLA_DK)
    k4 = k.reshape(nb, 1, diag, GLA_DK)
    gs4 = gc.reshape(nb, 1, diag, GLA_DK)
    shape = (nb, diag, diag, GLA_DK)
    ti = lax.broadcasted_iota(jnp.int32, shape, 1)
    si = lax.broadcasted_iota(jnp.int32, shape, 2)
    w = jnp.exp(jnp.where(si <= ti, g4 - gs4, NEG))
    m = (q4 * k4 * w).reshape(nb * diag * diag, GLA_DK)
    p = _dot(m.astype(BF16), expand).reshape(nb, diag, diag, GLA_DV)
    o = jnp.sum(p * v.reshape(nb, 1, diag, GLA_DV), axis=2)
    return o.reshape(rows, GLA_DV)


def _gla_finish(o, gr, seg, nw):
    ms = _split2_dot(o * o, seg)
    return o * lax.rsqrt(ms + EPS) * nw * _silu(gr)


def _head_stack(x, head_dim, heads):
    lane = _iota2(x.shape, 1) // head_dim
    return jnp.concatenate([jnp.where(lane == h, x, 0.0) for h in range(heads)], axis=0)


def _gla_prompt_kernel(gq_ref, gk_ref, gv_ref, gr_ref, small_ref, wg_ref, bg_ref, nw_ref, ex_ref, seg_ref,
                       o_ref, st_ref, s_scr, o_scr):
    c = pl.program_id(1)
    rows = gq_ref.shape[0]

    @pl.when(c == 0)
    def _():
        s_scr[...] = jnp.zeros_like(s_scr)

    qi = _iota2((rows, rows), 0)
    si = _iota2((rows, rows), 1)
    seg01 = jnp.where(si <= qi, 1.0, 0.0).astype(BF16)
    q, k, gc = _gla_prepare(gq_ref[...], gk_ref[...], small_ref[...], wg_ref[...], bg_ref[...], seg01)
    v = gv_ref[...]
    vb = v.astype(BF16)
    s_all = s_scr[...]
    o_scr[...] = (_dot((q * jnp.exp(gc)).astype(BF16), s_all.astype(BF16))
                  + _gla_pairwise(q, k, v, gc, ex_ref[...], GLA_DIAG))
    half = rows // 2
    while half >= GLA_DIAG:
        vlane = _iota2((half, GLA_DV), 1) // GLA_HEAD_V
        for blk in range(rows // (2 * half)):
            s0 = blk * 2 * half
            t0 = s0 + half
            ref = gc[t0 - 1:t0, :]
            qs = q[t0:t0 + half, :] * jnp.exp(gc[t0:t0 + half, :] - ref)
            ks = k[s0:t0, :] * jnp.exp(ref - gc[s0:t0, :])
            att = _dot_nt(_head_stack(qs, GLA_HEAD_K, GLA_HEADS).astype(BF16), ks.astype(BF16))
            pv = _dot(att.astype(BF16), vb[s0:t0, :])
            ot = jnp.zeros((half, GLA_DV), F32)
            for h in range(GLA_HEADS):
                ot = jnp.where(vlane == h, pv[h * half:(h + 1) * half, :], ot)
            o_scr[t0:t0 + half, :] += ot
        half //= 2
    o_ref[...] = _gla_finish(o_scr[...], gr_ref[...], seg_ref[...], nw_ref[...]).astype(o_ref.dtype)
    gc_t = gc.T
    dcol = gc_t[:, rows - 1:rows]
    kd_t = k.T * jnp.exp(dcol - gc_t)
    upd = _dot(kd_t.astype(BF16), vb)
    bd = (_iota2((GLA_DK, GLA_DV), 0) // GLA_HEAD_K) == (_iota2((GLA_DK, GLA_DV), 1) // GLA_HEAD_V)
    s_new = jnp.exp(dcol) * s_all + jnp.where(bd, upd, 0.0)
    s_scr[...] = s_new

    @pl.when(c == pl.num_programs(1) - 1)
    def _():
        st_ref[0] = s_new


def _gla_vec_specs():
    return [_const_spec((LANES, GLA_DK)), _const_spec((1, GLA_DK)), _const_spec((1, GLA_DV)),
            _const_spec((GLA_DK, GLA_DV)), _const_spec((GLA_DV, GLA_DV))]


def _gla_prompt(gq, gk, gv, gr, small, lp, b, l):
    nc = l // GLA_CHUNK
    rs = lambda w: pl.BlockSpec((GLA_CHUNK, w), lambda bi, ci: (bi * nc + ci, 0))
    expand, seg = _gla_consts()
    return pl.pallas_call(
        _gla_prompt_kernel,
        grid=(b, nc),
        in_specs=[rs(GLA_DK), rs(GLA_DK), rs(GLA_DV), rs(GLA_DV), rs(LANES)] + _gla_vec_specs(),
        out_specs=[rs(GLA_DV), pl.BlockSpec((1, GLA_DK, GLA_DV), lambda bi, ci: (bi, 0, 0))],
        out_shape=[jax.ShapeDtypeStruct((b * l, GLA_DV), BF16),
                   jax.ShapeDtypeStruct((b, GLA_DK, GLA_DV), F32)],
        scratch_shapes=[pltpu.VMEM((GLA_DK, GLA_DV), F32), pltpu.VMEM((GLA_CHUNK, GLA_DV), F32)],
        compiler_params=_params(("arbitrary", "arbitrary")),
        name="gla_prompt",
    )(gq, gk, gv, gr, small, lp["gla_wg"], lp["gla_bg"], lp["gla_nw"], expand, seg)


def _gla_sample_kernel(gq_ref, gk_ref, gv_ref, gr_ref, small_ref, s0_ref, wg_ref, bg_ref, nw_ref, ex_ref, seg_ref,
                       o_ref, st_ref, *, seq):
    rows = gq_ref.shape[0]
    nseq = rows // seq
    qi = _iota2((rows, rows), 0)
    si = _iota2((rows, rows), 1)
    seg01 = jnp.where(((qi // seq) == (si // seq)) & (si <= qi), 1.0, 0.0).astype(BF16)
    last01 = jnp.where(si == (qi // seq) * seq + (seq - 1), 1.0, 0.0).astype(BF16)
    q, k, gc = _gla_prepare(gq_ref[...], gk_ref[...], small_ref[...], wg_ref[...], bg_ref[...], seg01)
    v = gv_ref[...]
    vb = v.astype(BF16)
    gc_last = _split3_dot(last01, gc)
    qg = (q * jnp.exp(gc)).astype(BF16)
    kd_t = (k * jnp.exp(gc_last - gc)).T
    dec_t = jnp.exp(gc_last).T
    colseq = _iota2((GLA_DK, rows), 1) // seq
    rowseq = _iota2((rows, GLA_DV), 0) // seq
    bd = (_iota2((GLA_DK, GLA_DV), 0) // GLA_HEAD_K) == (_iota2((GLA_DK, GLA_DV), 1) // GLA_HEAD_V)
    o = _gla_pairwise(q, k, v, gc, ex_ref[...], seq)
    for b in range(nseq):
        s0 = s0_ref[b]
        o = o + jnp.where(rowseq == b, _dot(qg, s0.astype(BF16)), 0.0)
        upd = _dot(jnp.where(colseq == b, kd_t, 0.0).astype(BF16), vb)
        st_ref[b] = dec_t[:, b * seq:b * seq + 1] * s0 + jnp.where(bd, upd, 0.0)
    o_ref[...] = _gla_finish(o, gr_ref[...], seg_ref[...], nw_ref[...]).astype(o_ref.dtype)


def _gla_sample(gq, gk, gv, gr, small, s0, lp, b, l):
    rows = SAMPLE_SEQS * l
    rs = lambda w: pl.BlockSpec((rows, w), lambda i: (i, 0))
    st = pl.BlockSpec((SAMPLE_SEQS, GLA_DK, GLA_DV), lambda i: (i, 0, 0))
    expand, seg = _gla_consts()
    return pl.pallas_call(
        functools.partial(_gla_sample_kernel, seq=l),
        grid=(b // SAMPLE_SEQS,),
        in_specs=[rs(GLA_DK), rs(GLA_DK), rs(GLA_DV), rs(GLA_DV), rs(LANES), st] + _gla_vec_specs(),
        out_specs=[rs(GLA_DV), st],
        out_shape=[jax.ShapeDtypeStruct((b * l, GLA_DV), BF16),
                   jax.ShapeDtypeStruct((b, GLA_DK, GLA_DV), F32)],
        compiler_params=_params(("arbitrary",)),
        name="gla_sample",
    )(gq, gk, gv, gr, small, s0, lp["gla_wg"], lp["gla_bg"], lp["gla_nw"], expand, seg)


def _gla_state_expand(s):
    b = s.shape[0]
    eye = jnp.eye(GLA_HEADS, dtype=s.dtype)
    return (s[:, :, :, None, :] * eye[None, :, None, :, None]).reshape(b, GLA_DK, GLA_DV)


def _gla_state_extract(s):
    b = s.shape[0]
    s5 = s.reshape(b, GLA_HEADS, GLA_HEAD_K, GLA_HEADS, GLA_HEAD_V)
    return jnp.stack([s5[:, h, :, h, :] for h in range(GLA_HEADS)], axis=1)


def _att_prompt_kernel(*refs, first, last):
    refs = list(refs)
    q_ref, kp_ref, kc_ref, vp_ref, vc_ref = refs[:5]
    refs = refs[5:]
    if not first:
        acc_in, m_in, l_in = refs[:3]
        refs = refs[3:]
    if last:
        (out_ref,) = refs
    else:
        acc_out, m_out, l_out = refs
    mblk = pl.program_id(2)
    tq = ATT_BLOCK
    q = (q_ref[0] * (ATT_HEAD_DIM ** -0.5)).astype(BF16)
    kcat = jnp.concatenate([kp_ref[0], kc_ref[0]], axis=0).astype(BF16)
    vcat = jnp.concatenate([vp_ref[0], vc_ref[0]], axis=0).astype(BF16)
    rel = tq + _iota2((tq, 2 * tq), 0) - _iota2((tq, 2 * tq), 1)
    valid = (rel >= 0) & (rel <= ATT_KEYS - 1) & ((mblk > 0) | (_iota2((tq, 2 * tq), 1) >= tq))
    lane_half = _iota2((tq, LANES), 1) // ATT_HEAD_DIM
    for p in range(ATT_HEADS // 2):
        sl = slice(p * LANES, (p + 1) * LANES)
        qp, kp, vp = q[:, sl], kcat[:, sl], vcat[:, sl]
        if first:
            m_prev = jnp.full((tq, LANES), NEG, F32)
            l_prev = jnp.zeros((tq, LANES), F32)
            a_prev = jnp.zeros((tq, LANES), F32)
        else:
            m_prev, l_prev, a_prev = m_in[0, :, sl], l_in[0, :, sl], acc_in[0, :, sl]
        m_new, l_new, a_new = m_prev, l_prev, a_prev
        for half in range(2):
            s = _dot_nt(jnp.where(lane_half == half, qp, 0.0).astype(BF16), kp)
            s = jnp.where(valid, s, NEG)
            mp = m_prev[:, half * ATT_HEAD_DIM:half * ATT_HEAD_DIM + 1]
            lp_ = l_prev[:, half * ATT_HEAD_DIM:half * ATT_HEAD_DIM + 1]
            mn = jnp.maximum(mp, jnp.max(s, axis=-1, keepdims=True))
            pr = jnp.exp(s - mn)
            alpha = jnp.exp(mp - mn)
            ln = lp_ * alpha + jnp.sum(pr, axis=-1, keepdims=True)
            pv = _dot(pr.astype(BF16), vp)
            sel = lane_half == half
            m_new = jnp.where(sel, mn, m_new)
            l_new = jnp.where(sel, ln, l_new)
            a_new = jnp.where(sel, a_prev * alpha + pv, a_new)
        if last:
            out_ref[0, :, sl] = (a_new / l_new).astype(out_ref.dtype)
        else:
            acc_out[0, :, sl] = a_new
            m_out[0, :, sl] = m_new
            l_out[0, :, sl] = l_new


def _att_prompt_stage(q, k, v, stats, stride, b, l, first, last):
    lm = l // stride
    nm = lm // ATT_BLOCK
    view = lambda a: a.reshape(b, lm, stride * ATT_DIM)
    cur = pl.BlockSpec((1, ATT_BLOCK, ATT_DIM), lambda bi, ri, mi: (bi, mi, ri))
    prev = pl.BlockSpec((1, ATT_BLOCK, ATT_DIM), lambda bi, ri, mi: (bi, jnp.maximum(mi - 1, 0), ri))
    args = [view(q), view(k), view(k), view(v), view(v)]
    specs = [cur, prev, cur, prev, cur]
    if not first:
        args += [view(s) for s in stats]
        specs += [cur, cur, cur]
    if last:
        out_shape = jax.ShapeDtypeStruct((b, lm, stride * ATT_DIM), BF16)
        out_specs = cur
    else:
        out_shape = [jax.ShapeDtypeStruct((b, lm, stride * ATT_DIM), F32)] * 3
        out_specs = [cur, cur, cur]
    out = pl.pallas_call(
        functools.partial(_att_prompt_kernel, first=first, last=last),
        grid=(b, stride, nm),
        in_specs=specs,
        out_specs=out_specs,
        out_shape=out_shape,
        compiler_params=_params(("arbitrary", "arbitrary", "arbitrary")),
        name="att_prompt",
    )(*args)
    if last:
        return out.reshape(b * l, ATT_DIM)
    return [o.reshape(b * l, ATT_DIM) for o in out]


def _att_prompt(aq, ak, av, b, l):
    stats = None
    n = len(DILATION_PATTERNS)
    for i, (_, stride) in enumerate(DILATION_PATTERNS):
        stats = _att_prompt_stage(aq, ak, av, stats, stride, b, l, i == 0, i == n - 1)
    return stats


def _att_counts(seq, nbuf):
    qpos = nbuf + np.arange(seq)[:, None]
    kpos = np.arange(nbuf + seq)[None, :]
    delta = qpos - kpos
    cnt = np.zeros(delta.shape, np.float32)
    for window, stride in DILATION_PATTERNS:
        cnt += ((delta >= 0) & (delta % stride == 0) & (delta <= window)).astype(np.float32)
    cnt = np.tile(cnt, (ATT_HEADS, 1))
    new = np.zeros((ATT_HEADS * seq, LANES), np.float32)
    new[:, :seq] = cnt[:, nbuf:]
    return jnp.asarray(cnt[:, :nbuf]), jnp.asarray(new)


def _att_sample_kernel(q_ref, kn_ref, vn_ref, kc_ref, vc_ref, cc_ref, cn_ref, out_ref):
    seq = q_ref.shape[1]
    q = q_ref[0] * (ATT_HEAD_DIM ** -0.5)
    qrows = _head_stack(q, ATT_HEAD_DIM, ATT_HEADS).astype(BF16)
    pad = jnp.zeros((LANES - seq, ATT_DIM), F32)
    kn = jnp.concatenate([kn_ref[0], pad], axis=0).astype(BF16)
    vn = jnp.concatenate([vn_ref[0], pad], axis=0).astype(BF16)
    cc = cc_ref[...]
    cn = cn_ref[...]
    sc = jnp.where(cc > 0, _dot_nt(qrows, kc_ref[0].astype(BF16)), NEG)
    sn = jnp.where(cn > 0, _dot_nt(qrows, kn), NEG)
    m = jnp.maximum(jnp.max(sc, axis=-1, keepdims=True), jnp.max(sn, axis=-1, keepdims=True))
    pc = cc * jnp.exp(sc - m)
    pn = cn * jnp.exp(sn - m)
    den = jnp.sum(pc, axis=-1, keepdims=True) + jnp.sum(pn, axis=-1, keepdims=True)
    o = (_dot(pc.astype(BF16), vc_ref[0].astype(BF16)) + _dot(pn.astype(BF16), vn)) / den
    lane = _iota2((seq, ATT_DIM), 1) // ATT_HEAD_DIM
    att = jnp.zeros((seq, ATT_DIM), F32)
    for h in range(ATT_HEADS):
        att = jnp.where(lane == h, o[h * seq:(h + 1) * seq, :], att)
    out_ref[0] = att.astype(out_ref.dtype)


def _att_sample(aq, ak, av, kcache, vcache, layer, b, l):
    nbuf = kcache.shape[2]
    cc, cn = _att_counts(l, nbuf)
    new = pl.BlockSpec((1, l, ATT_DIM), lambda i: (i, 0, 0))
    cache = pl.BlockSpec((None, 1, nbuf, ATT_DIM), lambda i: (layer, i, 0, 0))
    out = pl.pallas_call(
        _att_sample_kernel,
        grid=(b,),
        in_specs=[new, new, new, cache, cache,
                  _const_spec((ATT_HEADS * l, nbuf)), _const_spec((ATT_HEADS * l, LANES))],
        out_specs=new,
        out_shape=jax.ShapeDtypeStruct((b, l, ATT_DIM), BF16),
        compiler_params=_params(("arbitrary",)),
        name="att_sample",
    )(aq.reshape(b, l, ATT_DIM), ak.reshape(b, l, ATT_DIM), av.reshape(b, l, ATT_DIM), kcache, vcache, cc, cn)
    return out.reshape(b * l, ATT_DIM)


def _pad_lanes(v, width, offset=0):
    out = jnp.zeros((1, width), F32)
    return out.at[0, offset:offset + v.shape[0]].set(v.astype(F32))


def _layer_params(l, w_in, w_out, ssd_conv_w, ssd_conv_b, ssd_dt_bias, ssd_a_log, ssd_d, ssd_norm_w,
                  gla_w_gate, gla_b_gate, gla_norm_w, norm_w, ffn1_w_in, ffn1_w_out, ffn2_w_in, ffn2_w_out):
    offs = np.concatenate([[0], np.cumsum(IN_SPLITS)])
    cols = {n: w_in[l][:, offs[i]:offs[i + 1]] for i, n in enumerate(
        ("z", "xbc", "dt", "gq", "gk", "gv", "gr", "glr", "aq", "ak", "av"))}
    small = jnp.zeros((D_MODEL, LANES), F32)
    small = small.at[:, SMALL_DT_OFF:SMALL_DT_OFF + SSD_HEADS].set(cols["dt"])
    small = small.at[:, SMALL_GLR_OFF:SMALL_GLR_OFF + GLA_GATE_RANK].set(cols["glr"])
    cols["small"] = small
    w_in_p = jnp.concatenate([cols[n] for n, _ in PROJ_GROUPS], axis=1).astype(BF16)
    wg = jnp.zeros((LANES, GLA_DK), F32).at[SMALL_GLR_OFF:SMALL_GLR_OFF + GLA_GATE_RANK, :].set(gla_w_gate[l])
    return dict(
        w_in_p=w_in_p, w_out=w_out[l].astype(BF16),
        conv_w=ssd_conv_w[l], conv_b=ssd_conv_b[l].reshape(1, SSD_CONV_DIM),
        dtb=_pad_lanes(ssd_dt_bias[l], LANES, SMALL_DT_OFF), alog=_pad_lanes(ssd_a_log[l], LANES, SMALL_DT_OFF),
        dvec=jnp.repeat(ssd_d[l].astype(F32), SSD_HEAD_DIM).reshape(1, SSD_INNER),
        ssd_nw=ssd_norm_w[l].reshape(1, SSD_INNER),
        gla_wg=wg.astype(BF16), gla_bg=gla_b_gate[l].reshape(1, GLA_DK),
        gla_nw=jnp.tile(gla_norm_w[l], GLA_HEADS).reshape(1, GLA_DV),
        norm_w=norm_w[l],
        ffn1_in=ffn1_w_in[l].astype(BF16), ffn1_out=ffn1_w_out[l].astype(BF16),
        ffn2_in=ffn2_w_in[l].astype(BF16), ffn2_out=ffn2_w_out[l].astype(BF16),
    )


def _trunk(x, mods, layers, norm_f, states, sample):
    b, l, _ = x.shape
    keep = min(ATT_MAX_WINDOW, l)
    outs = ([], [], [], [], [])
    for li, lp in enumerate(layers):
        mod = mods[li]
        x = _ffn(x, mod, 0, lp["norm_w"][0], lp["ffn1_in"], lp["ffn1_out"])
        proj = dict(zip([n for n, _ in PROJ_GROUPS], _inproj(x, mod, lp["norm_w"][1], lp["w_in_p"])))
        if sample:
            st_ssd, st_conv, st_gla, kcache, vcache = states
            bufp = jnp.pad(st_conv[li], ((0, 0), (0, l - (SSD_CONV - 1)), (0, 0))).reshape(b * l, SSD_CONV_DIM)
            y, ssd_new = _ssd_sample(proj["z"], proj["xbc"], proj["small"], bufp, st_ssd[li], lp, b, l)
            o, gla_new = _gla_sample(proj["gq"], proj["gk"], proj["gv"], proj["gr"], proj["small"],
                                     _gla_state_expand(st_gla[li]), lp, b, l)
            att = _att_sample(proj["aq"], proj["ak"], proj["av"], kcache, vcache, li, b, l)
        else:
            y, ssd_new = _ssd_prompt(proj["z"], proj["xbc"], proj["small"], lp, b, l)
            o, gla_new = _gla_prompt(proj["gq"], proj["gk"], proj["gv"], proj["gr"], proj["small"], lp, b, l)
            att = _att_prompt(proj["aq"], proj["ak"], proj["av"], b, l)
        x = _ffn(x, mod, 6, lp["norm_w"][2], lp["ffn2_in"], lp["ffn2_out"],
                 premix=(y, o, att, 5, lp["w_out"]),
                 final_norm=norm_f if li == len(layers) - 1 else None)
        outs[0].append(ssd_new)
        outs[1].append(proj["xbc"].reshape(b, l, SSD_CONV_DIM)[:, l - (SSD_CONV - 1):])
        outs[2].append(_gla_state_extract(gla_new))
        outs[3].append(proj["ak"].reshape(b, l, ATT_HEADS, ATT_HEAD_DIM)[:, l - keep:])
        outs[4].append(proj["av"].reshape(b, l, ATT_HEADS, ATT_HEAD_DIM)[:, l - keep:])
    return x, [jnp.stack(a) for a in outs]


def kernel(x_prompt, x_sample, c_prompt, c_sample, state_ssd, state_ssd_conv, state_gla, cache_attn_k, cache_attn_v,
           w_in, w_out, ssd_conv_w, ssd_conv_b, ssd_dt_bias, ssd_a_log, ssd_d, ssd_norm_w,
           gla_w_gate, gla_b_gate, gla_norm_w, norm_w, w_mod, b_mod,
           ffn1_w_in, ffn1_w_out, ffn2_w_in, ffn2_w_out, norm_f):
    bp, bs = x_prompt.shape[0], x_sample.shape[0]
    depth = w_in.shape[0]
    layers = [_layer_params(l, w_in, w_out, ssd_conv_w, ssd_conv_b, ssd_dt_bias, ssd_a_log, ssd_d, ssd_norm_w,
                            gla_w_gate, gla_b_gate, gla_norm_w, norm_w, ffn1_w_in, ffn1_w_out, ffn2_w_in,
                            ffn2_w_out) for l in range(depth)]
    npad = -(bp + bs) % SUBLANES
    c_all = jnp.concatenate([c_prompt, c_sample, jnp.zeros((npad, D_MODEL), F32)], axis=0)
    mods_p, mods_s = [], []
    for l in range(depth):
        m = _modulation(c_all, w_mod[l].astype(BF16), b_mod[l])
        m = m.reshape(-1, ADALN_MODS, 1, D_MODEL).transpose(1, 0, 2, 3)
        mods_p.append(m[:, :bp])
        mods_s.append(m[:, bp:bp + bs])
    nbuf = cache_attn_k.shape[2]
    kcache = cache_attn_k.reshape(depth, bs, nbuf, ATT_DIM)
    vcache = cache_attn_v.reshape(depth, bs, nbuf, ATT_DIM)
    y_p, (ssd_p, conv_p, gla_p, k_p, v_p) = _trunk(x_prompt, mods_p, layers, norm_f, None, sample=False)
    y_s, (ssd_s, conv_s, gla_s, k_s, v_s) = _trunk(
        x_sample, mods_s, layers, norm_f, (state_ssd, state_ssd_conv, state_gla, kcache, vcache), sample=True)
    return (y_p, y_s, ssd_p, ssd_s, conv_p, conv_s, gla_p, gla_s, k_p, k_s, v_p, v_s)
```

```python
import functools
import math

import numpy as np
import jax
import jax.numpy as jnp
from jax import lax
from jax.experimental import pallas as pl
from jax.experimental.pallas import tpu as pltpu

F32 = jnp.float32
BF16 = jnp.bfloat16

D_MODEL = 1024
DEPTH = 2
SSD_HEADS = 6
SSD_HEAD_DIM = 64
SSD_INNER = SSD_HEADS * SSD_HEAD_DIM
SSD_GROUPS = 2
SSD_STATE = 128
SSD_CONV = 4
SSD_CONV_DIM = SSD_INNER + 2 * SSD_GROUPS * SSD_STATE
GLA_HEADS = 4
GLA_HEAD_K = 32
GLA_HEAD_V = 64
GLA_DK = GLA_HEADS * GLA_HEAD_K
GLA_DV = GLA_HEADS * GLA_HEAD_V
GLA_GATE_RANK = 16
GLA_TAU = 16.0
ATT_HEADS = 6
ATT_HEAD_DIM = 64
ATT_DIM = ATT_HEADS * ATT_HEAD_DIM
DILATION_PATTERNS = ((128, 1), (512, 4), (2048, 16))
ATT_MAX_WINDOW = 2048
ATT_KEYS = 129
D_MIX = SSD_INNER + GLA_DV + ATT_DIM
IN_SPLITS = (SSD_INNER, SSD_CONV_DIM, SSD_HEADS, GLA_DK, GLA_DK, GLA_DV, GLA_DV, GLA_GATE_RANK,
             ATT_DIM, ATT_DIM, ATT_DIM)
D_FF = 2816
ADALN_MODS = 9
FFN_RES = 0.5
EPS = 1e-6

LANES = 128
SUBLANES = 8
VMEM_LIMIT = 56 * 1024 * 1024

PROJ_GROUPS = (("z", SSD_INNER), ("xbc", SSD_CONV_DIM), ("gq", GLA_DK), ("gk", GLA_DK), ("gv", GLA_DV),
               ("gr", GLA_DV), ("aq", ATT_DIM), ("ak", ATT_DIM), ("av", ATT_DIM), ("small", LANES))
PROJ_WIDTH = sum(w for _, w in PROJ_GROUPS)
PAIR_MAJOR = ("aq", "ak", "av")
SMALL_DT_OFF = 0
SMALL_GLR_OFF = 8

ROW_TILE = 512
FF_CHUNK = 256
SSD_CHUNK = 128
GLA_CHUNK = 128
GLA_STEP = 256
GLA_DIAG = 16
SAMPLE_SEQS = 16
ATT_BLOCK = 128
NEG = -1e30


def _dot(a, b):
    return jnp.dot(a, b, preferred_element_type=F32)


def _dot_nt(a, b):
    return lax.dot_general(a, b, (((1,), (1,)), ((), ())), preferred_element_type=F32)


def _sigmoid(x):
    return 1.0 / (1.0 + jnp.exp(-x))


def _silu(x):
    return x * _sigmoid(x)


def _softplus(x):
    return jnp.maximum(x, 0.0) + jnp.log1p(jnp.exp(-jnp.abs(x)))


def _split3_dot(m01, a):
    a1 = a.astype(BF16)
    r1 = a - a1.astype(F32)
    a2 = r1.astype(BF16)
    a3 = (r1 - a2.astype(F32)).astype(BF16)
    return _dot(m01, a1) + _dot(m01, a2) + _dot(m01, a3)


def _split2_dot(a, m01):
    a1 = a.astype(BF16)
    a2 = (a - a1.astype(F32)).astype(BF16)
    return _dot(a1, m01) + _dot(a2, m01)


def _rms_mod(x, nw, shift, scale):
    ms = jnp.mean(x * x, axis=-1, keepdims=True)
    y = x * lax.rsqrt(ms + EPS) * nw
    return y * (1.0 + scale) + shift


def _iota2(shape, axis):
    return lax.broadcasted_iota(jnp.int32, shape, axis)


def _params(sem):
    return pltpu.CompilerParams(dimension_semantics=sem, vmem_limit_bytes=VMEM_LIMIT)


def _const_spec(shape):
    nd = len(shape)
    return pl.BlockSpec(shape, lambda *_: (0,) * nd, pipeline_mode=pl.Buffered(1))


def _mod_kernel(c_ref, w_ref, b_ref, o_ref):
    c = c_ref[...]
    o_ref[...] = _dot(_silu(c).astype(BF16), w_ref[...]) + b_ref[...]


def _modulation(c_all, w_mod, b_mod):
    n, d = c_all.shape
    nout = w_mod.shape[1]
    tn = D_MODEL
    return pl.pallas_call(
        _mod_kernel,
        grid=(nout // tn,),
        in_specs=[pl.BlockSpec((n, d), lambda j: (0, 0)),
                  pl.BlockSpec((d, tn), lambda j: (0, j)),
                  pl.BlockSpec((1, tn), lambda j: (0, j))],
        out_specs=pl.BlockSpec((n, tn), lambda j: (0, j)),
        out_shape=jax.ShapeDtypeStruct((n, nout), F32),
        compiler_params=_params(("arbitrary",)),
        name="adaln_mod",
    )(c_all, w_mod, b_mod.reshape(1, nout))


def _row_tiling(b, l):
    if l >= ROW_TILE:
        assert l % ROW_TILE == 0
        return 1, ROW_TILE
    assert ROW_TILE % l == 0 and b % (ROW_TILE // l) == 0
    return ROW_TILE // l, l


def _x_spec(bb, ll, nlb):
    return pl.BlockSpec((bb, ll, D_MODEL), lambda i: (i // nlb, i % nlb, 0))


def _mod_spec(k, bb, nlb):
    return pl.BlockSpec((1, bb, 1, D_MODEL), lambda i: (k, i // nlb, 0, 0))


def _rows_spec(r, width):
    return pl.BlockSpec((r, width), lambda i: (i, 0))


def _pair_spec(r, width):
    return pl.BlockSpec((width // LANES, r, LANES), lambda i: (0, i, 0))


def _ffn_kernel(*refs, premix, final):
    refs = list(refs)
    x_ref = refs.pop(0)
    if premix:
        y_ref, o_ref, a_ref, g2_ref, wo_ref = refs[:5]
        refs = refs[5:]
    sh_ref, sc_ref, gt_ref, nw_ref, win_ref, wout_ref = refs[:6]
    refs = refs[6:]
    if final:
        nf_ref = refs.pop(0)
    out_ref, act_ref = refs
    bb, ll, d = x_ref.shape
    r = bb * ll
    x = x_ref[...]
    if premix:
        mix = (_dot(y_ref[...], wo_ref[0:SSD_INNER, :])
               + _dot(o_ref[...], wo_ref[SSD_INNER:SSD_INNER + GLA_DV, :])
               + sum(_dot(a_ref[p], wo_ref[SSD_INNER + GLA_DV + p * LANES:SSD_INNER + GLA_DV + (p + 1) * LANES, :])
                     for p in range(ATT_DIM // LANES)))
        x = x + g2_ref[0] * mix.reshape(bb, ll, d)
    h = _rms_mod(x, nw_ref[...], sh_ref[0], sc_ref[0]).reshape(r, d).astype(BF16)
    for c in range(D_FF // FF_CHUNK):
        g = _dot(h, win_ref[:, c * FF_CHUNK:(c + 1) * FF_CHUNK])
        u = _dot(h, win_ref[:, D_FF + c * FF_CHUNK:D_FF + (c + 1) * FF_CHUNK])
        act_ref[:, c * FF_CHUNK:(c + 1) * FF_CHUNK] = (_silu(g) * u).astype(BF16)
    y = _dot(act_ref[...], wout_ref[...])
    x = x + FFN_RES * gt_ref[0] * y.reshape(bb, ll, d)
    if final:
        ms = jnp.mean(x * x, axis=-1, keepdims=True)
        x = x * lax.rsqrt(ms + EPS) * nf_ref[...]
    out_ref[...] = x


def _ffn(x, mod, mod_base, norm_w, w_in, w_out, premix=None, final_norm=None):
    b, l, d = x.shape
    bb, ll = _row_tiling(b, l)
    nlb = l // ll
    r = bb * ll
    nsteps = (b // bb) * nlb
    args, specs = [x], [_x_spec(bb, ll, nlb)]
    if premix is not None:
        y, o, a, gate_row, wo = premix
        args += [y, o, a, mod, wo]
        specs += [_rows_spec(r, SSD_INNER), _rows_spec(r, GLA_DV), _pair_spec(r, ATT_DIM),
                  _mod_spec(gate_row, bb, nlb), _const_spec((D_MIX, d))]
    args += [mod, mod, mod, norm_w.reshape(1, d), w_in, w_out]
    specs += [_mod_spec(mod_base, bb, nlb), _mod_spec(mod_base + 1, bb, nlb), _mod_spec(mod_base + 2, bb, nlb),
              _const_spec((1, d)), _const_spec((d, 2 * D_FF)), _const_spec((D_FF, d))]
    if final_norm is not None:
        args.append(final_norm.reshape(1, d))
        specs.append(_const_spec((1, d)))
    return pl.pallas_call(
        functools.partial(_ffn_kernel, premix=premix is not None, final=final_norm is not None),
        grid=(nsteps,),
        in_specs=specs,
        out_specs=_x_spec(bb, ll, nlb),
        out_shape=jax.ShapeDtypeStruct((b, l, d), F32),
        scratch_shapes=[pltpu.VMEM((r, D_FF), BF16)],
        compiler_params=_params(("arbitrary",)),
        name="ffn",
    )(*args)


def _inproj_kernel(x_ref, sh_ref, sc_ref, nw_ref, w_ref, *out_refs):
    bb, ll, d = x_ref.shape
    h = _rms_mod(x_ref[...], nw_ref[...], sh_ref[0], sc_ref[0]).reshape(bb * ll, d).astype(BF16)
    off = 0
    for ref, (name, width) in zip(out_refs, PROJ_GROUPS):
        if name in PAIR_MAJOR:
            for p in range(width // LANES):
                ref[p] = _dot(h, w_ref[:, off + p * LANES:off + (p + 1) * LANES])
        else:
            ref[...] = _dot(h, w_ref[:, off:off + width])
        off += width


def _inproj(x, mod, norm_w, w_in_p):
    b, l, d = x.shape
    bb, ll = _row_tiling(b, l)
    nlb = l // ll
    r = bb * ll
    return pl.pallas_call(
        _inproj_kernel,
        grid=((b // bb) * nlb,),
        in_specs=[_x_spec(bb, ll, nlb), _mod_spec(3, bb, nlb), _mod_spec(4, bb, nlb),
                  _const_spec((1, d)), _const_spec((d, PROJ_WIDTH))],
        out_specs=[_pair_spec(r, w) if n in PAIR_MAJOR else _rows_spec(r, w) for n, w in PROJ_GROUPS],
        out_shape=[jax.ShapeDtypeStruct((w // LANES, b * l, LANES) if n in PAIR_MAJOR else (b * l, w), F32)
                   for n, w in PROJ_GROUPS],
        compiler_params=_params(("arbitrary",)),
        name="inproj",
    )(x, mod, mod, norm_w.reshape(1, d), w_in_p)


def _ssd_prepare(conv, small, dtb, alog, seg01, seglast01):
    xc = _silu(conv)
    xs = xc[:, 0:SSD_INNER]
    bm = xc[:, SSD_INNER:SSD_INNER + SSD_GROUPS * SSD_STATE]
    cm = xc[:, SSD_INNER + SSD_GROUPS * SSD_STATE:SSD_CONV_DIM]
    dt = _softplus(small + dtb)
    a = dt * (-jnp.exp(alog))
    cum = _split3_dot(seg01, a)
    if seglast01 is None:
        cum_last = jnp.broadcast_to(cum[cum.shape[0] - 1:, :], cum.shape)
    else:
        cum_last = _split3_dot(seglast01, cum)
    return xs, bm, cm, dt, cum, cum_last


def _ssd_diag(xs, bm, cm, dt, cum, cum_t, mask):
    ydiag, xdt = [], []
    cb = [_dot_nt(cm[:, g * SSD_STATE:(g + 1) * SSD_STATE].astype(BF16),
                  bm[:, g * SSD_STATE:(g + 1) * SSD_STATE].astype(BF16)) for g in range(SSD_GROUPS)]
    for h in range(SSD_HEADS):
        g = h // (SSD_HEADS // SSD_GROUPS)
        diff = cum[:, h:h + 1] - cum_t[h:h + 1, :]
        decay = jnp.exp(jnp.where(mask, diff, NEG))
        xh = xs[:, h * SSD_HEAD_DIM:(h + 1) * SSD_HEAD_DIM] * dt[:, h:h + 1]
        xdt.append(xh)
        ydiag.append(_dot((cb[g] * decay).astype(BF16), xh.astype(BF16)))
    return ydiag, xdt


def _ssd_finish(y, xs, z, dvec, normw):
    y = (y + dvec * xs) * _silu(z)
    sq = y * y
    half = SSD_INNER // SSD_GROUPS
    lane = _iota2(y.shape, 1)
    s0 = jnp.sum(jnp.where(lane < half, sq, 0.0), axis=-1, keepdims=True)
    s1 = jnp.sum(jnp.where(lane >= half, sq, 0.0), axis=-1, keepdims=True)
    ms = jnp.where(lane < half, s0, s1) * (1.0 / half)
    return y * lax.rsqrt(ms + EPS) * normw


def _ssd_prompt_kernel(z_ref, xbc_ref, small_ref, cw_ref, cb_ref, dtb_ref, alog_ref, dvec_ref, nw_ref,
                       y_ref, st_ref, h_scr, tail_scr):
    c = pl.program_id(1)
    rows = xbc_ref.shape[0]

    @pl.when(c == 0)
    def _():
        h_scr[...] = jnp.zeros_like(h_scr)
        tail_scr[...] = jnp.zeros_like(tail_scr)

    u = xbc_ref[...]
    tail = tail_scr[...]
    row8 = _iota2((SUBLANES, SSD_CONV_DIM), 0)
    conv = u * cw_ref[SSD_CONV - 1:SSD_CONV, :] + cb_ref[...]
    for j in range(1, SSD_CONV):
        uj = pltpu.roll(u, j, axis=0)
        back = (SUBLANES - (SSD_CONV - 1) + j) % SUBLANES
        tj = pltpu.roll(tail, back, axis=0) if back else tail
        top = jnp.where(row8 < j, tj, uj[0:SUBLANES, :])
        uj = jnp.concatenate([top, uj[SUBLANES:, :]], axis=0)
        conv = conv + uj * cw_ref[SSD_CONV - 1 - j:SSD_CONV - j, :]
    tail_scr[...] = pltpu.roll(u[rows - SUBLANES:, :], SSD_CONV - 1, axis=0)

    qi = _iota2((rows, rows), 0)
    si = _iota2((rows, rows), 1)
    mask = si <= qi
    seg01 = jnp.where(mask, 1.0, 0.0).astype(BF16)
    xs, bm, cm, dt, cum, cum_last = _ssd_prepare(conv, small_ref[...], dtb_ref[...], alog_ref[...], seg01, None)
    cum_t = cum.T
    ydiag, xdt = _ssd_diag(xs, bm, cm, dt, cum, cum_t, mask)
    to_end_t = jnp.exp(cum_last - cum).T
    xdt_t = jnp.concatenate(xdt, axis=1).T
    ys = []
    for h in range(SSD_HEADS):
        g = h // (SSD_HEADS // SSD_GROUPS)
        hprev = h_scr[h]
        cg = cm[:, g * SSD_STATE:(g + 1) * SSD_STATE].astype(BF16)
        yoff = _dot_nt(cg, hprev.astype(BF16)) * jnp.exp(cum[:, h:h + 1])
        ys.append(ydiag[h] + yoff)
        xw_t = xdt_t[h * SSD_HEAD_DIM:(h + 1) * SSD_HEAD_DIM, :] * to_end_t[h:h + 1, :]
        s_local = _dot(xw_t.astype(BF16), bm[:, g * SSD_STATE:(g + 1) * SSD_STATE].astype(BF16))
        h_scr[h] = jnp.exp(cum_last[0:1, h:h + 1]) * hprev + s_local
    y = _ssd_finish(jnp.concatenate(ys, axis=1), xs, z_ref[...], dvec_ref[...], nw_ref[...])
    y_ref[...] = y.astype(y_ref.dtype)

    @pl.when(c == pl.num_programs(1) - 1)
    def _():
        st_ref[0] = h_scr[...]


def _ssd_vec_specs():
    return [_const_spec((SSD_CONV, SSD_CONV_DIM)), _const_spec((1, SSD_CONV_DIM)), _const_spec((1, LANES)),
            _const_spec((1, LANES)), _const_spec((1, SSD_INNER)), _const_spec((1, SSD_INNER))]


def _ssd_prompt(z, xbc, small, lp, b, l):
    nc = l // SSD_CHUNK
    rs = lambda w: pl.BlockSpec((SSD_CHUNK, w), lambda bi, ci: (bi * nc + ci, 0))
    return pl.pallas_call(
        _ssd_prompt_kernel,
        grid=(b, nc),
        in_specs=[rs(SSD_INNER), rs(SSD_CONV_DIM), rs(LANES)] + _ssd_vec_specs(),
        out_specs=[rs(SSD_INNER),
                   pl.BlockSpec((1, SSD_HEADS, SSD_HEAD_DIM, SSD_STATE), lambda bi, ci: (bi, 0, 0, 0))],
        out_shape=[jax.ShapeDtypeStruct((b * l, SSD_INNER), BF16),
                   jax.ShapeDtypeStruct((b, SSD_HEADS, SSD_HEAD_DIM, SSD_STATE), F32)],
        scratch_shapes=[pltpu.VMEM((SSD_HEADS, SSD_HEAD_DIM, SSD_STATE), F32),
                        pltpu.VMEM((SUBLANES, SSD_CONV_DIM), F32)],
        compiler_params=_params(("arbitrary", "arbitrary")),
        name="ssd_prompt",
    )(z, xbc, small, lp["conv_w"], lp["conv_b"], lp["dtb"], lp["alog"], lp["dvec"], lp["ssd_nw"])


def _ssd_sample_kernel(z_ref, xbc_ref, small_ref, buf_ref, h0_ref, cw_ref, cb_ref, dtb_ref, alog_ref, dvec_ref,
                       nw_ref, y_ref, st_ref, *, seq):
    rows = xbc_ref.shape[0]
    nseq = rows // seq
    u = xbc_ref[...]
    bufp = buf_ref[...]
    tpos = _iota2((rows, SSD_CONV_DIM), 0) % seq
    conv = u * cw_ref[SSD_CONV - 1:SSD_CONV, :] + cb_ref[...]
    for j in range(1, SSD_CONV):
        uj = pltpu.roll(u, j, axis=0)
        back = (rows - (SSD_CONV - 1 - j)) % rows
        bj = pltpu.roll(bufp, back, axis=0) if back else bufp
        conv = conv + jnp.where(tpos < j, bj, uj) * cw_ref[SSD_CONV - 1 - j:SSD_CONV - j, :]

    qi = _iota2((rows, rows), 0)
    si = _iota2((rows, rows), 1)
    same = (qi // seq) == (si // seq)
    mask = same & (si <= qi)
    seg01 = jnp.where(mask, 1.0, 0.0).astype(BF16)
    last01 = jnp.where(si == (qi // seq) * seq + (seq - 1), 1.0, 0.0).astype(BF16)
    xs, bm, cm, dt, cum, cum_last = _ssd_prepare(conv, small_ref[...], dtb_ref[...], alog_ref[...], seg01, last01)
    cum_t = cum.T
    cum_last_t = cum_last.T
    ydiag, xdt = _ssd_diag(xs, bm, cm, dt, cum, cum_t, mask)
    to_end_t = jnp.exp(cum_last_t - cum_t)
    xdt_t = jnp.concatenate(xdt, axis=1).T
    hpg = SSD_HEADS // SSD_GROUPS
    grows = hpg * SSD_HEAD_DIM
    colseq = _iota2((grows, rows), 1) // seq
    yoff_t = []
    for g in range(SSD_GROUPS):
        cg = cm[:, g * SSD_STATE:(g + 1) * SSD_STATE].astype(BF16)
        bg = bm[:, g * SSD_STATE:(g + 1) * SSD_STATE].astype(BF16)
        xw_t = jnp.concatenate(
            [xdt_t[h * SSD_HEAD_DIM:(h + 1) * SSD_HEAD_DIM, :] * to_end_t[h:h + 1, :]
             for h in range(g * hpg, (g + 1) * hpg)], axis=0)
        acc = jnp.zeros((grows, rows), F32)
        for b in range(nseq):
            h0 = h0_ref[b, g * hpg:(g + 1) * hpg].reshape(grows, SSD_STATE)
            acc = jnp.where(colseq == b, _dot_nt(h0.astype(BF16), cg), acc)
            s_local = _dot(jnp.where(colseq == b, xw_t, 0.0).astype(BF16), bg)
            for hh in range(hpg):
                h = g * hpg + hh
                dec = jnp.exp(cum_last_t[h:h + 1, b * seq:b * seq + 1])
                st_ref[b, h] = (dec * h0[hh * SSD_HEAD_DIM:(hh + 1) * SSD_HEAD_DIM, :]
                                + s_local[hh * SSD_HEAD_DIM:(hh + 1) * SSD_HEAD_DIM, :])
        for hh in range(hpg):
            h = g * hpg + hh
            yoff_t.append(acc[hh * SSD_HEAD_DIM:(hh + 1) * SSD_HEAD_DIM, :] * jnp.exp(cum_t[h:h + 1, :]))
    yoff = jnp.concatenate(yoff_t, axis=0).T
    y = _ssd_finish(jnp.concatenate(ydiag, axis=1) + yoff, xs, z_ref[...], dvec_ref[...], nw_ref[...])
    y_ref[...] = y.astype(y_ref.dtype)


def _ssd_sample(z, xbc, small, bufp, h0, lp, b, l):
    rows = SAMPLE_SEQS * l
    rs = lambda w: pl.BlockSpec((rows, w), lambda i: (i, 0))
    st = pl.BlockSpec((SAMPLE_SEQS, SSD_HEADS, SSD_HEAD_DIM, SSD_STATE), lambda i: (i, 0, 0, 0))
    return pl.pallas_call(
        functools.partial(_ssd_sample_kernel, seq=l),
        grid=(b // SAMPLE_SEQS,),
        in_specs=[rs(SSD_INNER), rs(SSD_CONV_DIM), rs(LANES), rs(SSD_CONV_DIM), st] + _ssd_vec_specs(),
        out_specs=[rs(SSD_INNER), st],
        out_shape=[jax.ShapeDtypeStruct((b * l, SSD_INNER), BF16),
                   jax.ShapeDtypeStruct((b, SSD_HEADS, SSD_HEAD_DIM, SSD_STATE), F32)],
        compiler_params=_params(("arbitrary",)),
        name="ssd_sample",
    )(z, xbc, small, bufp, h0, lp["conv_w"], lp["conv_b"], lp["dtb"], lp["alog"], lp["dvec"], lp["ssd_nw"])


def _gla_consts():
    rk = np.arange(GLA_DK)[:, None] // GLA_HEAD_K
    cv = np.arange(GLA_DV)[None, :] // GLA_HEAD_V
    expand = (rk == cv).astype(np.float32)
    rv = np.arange(GLA_DV)[:, None] // GLA_HEAD_V
    seg = (rv == cv).astype(np.float32) / GLA_HEAD_V
    return jnp.asarray(expand, BF16), jnp.asarray(seg, BF16)


def _gla_prepare(gq, gk, small, wg, bg, seg01):
    glin = _dot(small.astype(BF16), wg) + bg
    g = -_softplus(-glin) * (1.0 / GLA_TAU)
    gc = _split3_dot(seg01, g)
    q = gq * (GLA_HEAD_K ** -0.5)
    return q, gk, gc


def _gla_pairwise(q, k, v, gc, expand, diag):
    rows = q.shape[0]
    nb = rows // diag
    q4 = q.reshape(nb, diag, 1, GLA_DK)
    g4 = gc.reshape(nb, diag, 1, GLA_DK)
    k4 = k.reshape(nb, 1, diag, GLA_DK)
    gs4 = gc.reshape(nb, 1, diag, GLA_DK)
    shape = (nb, diag, diag, GLA_DK)
    ti = lax.broadcasted_iota(jnp.int32, shape, 1)
    si = lax.broadcasted_iota(jnp.int32, shape, 2)
    w = jnp.exp(jnp.where(si <= ti, g4 - gs4, NEG))
    m = (q4 * k4 * w).reshape(nb * diag * diag, GLA_DK)
    p = _dot(m.astype(BF16), expand).reshape(nb, diag, diag, GLA_DV)
    o = jnp.sum(p * v.reshape(nb, 1, diag, GLA_DV), axis=2)
    return o.reshape(rows, GLA_DV)


def _gla_finish(o, gr, seg, nw):
    ms = _split2_dot(o * o, seg)
    return o * lax.rsqrt(ms + EPS) * nw * _silu(gr)


def _head_stack(x, head_dim, heads):
    lane = _iota2(x.shape, 1) // head_dim
    return jnp.concatenate([jnp.where(lane == h, x, 0.0) for h in range(heads)], axis=0)


def _gla_prompt_kernel(gq_ref, gk_ref, gv_ref, gr_ref, small_ref, wg_ref, bg_ref, nw_ref, ex_ref, seg_ref,
                       o_ref, st_ref, s_scr, o_scr):
    c = pl.program_id(1)
    rows = GLA_CHUNK

    @pl.when(c == 0)
    def _():
        s_scr[...] = jnp.zeros_like(s_scr)

    qi = _iota2((rows, rows), 0)
    si = _iota2((rows, rows), 1)
    seg01 = jnp.where(si <= qi, 1.0, 0.0).astype(BF16)
    bd = (_iota2((GLA_DK, GLA_DV), 0) // GLA_HEAD_K) == (_iota2((GLA_DK, GLA_DV), 1) // GLA_HEAD_V)
    s_all = s_scr[...]
    for ci in range(gq_ref.shape[0] // rows):
        r0 = ci * rows
        q, k, gc = _gla_prepare(gq_ref[r0:r0 + rows, :], gk_ref[r0:r0 + rows, :], small_ref[r0:r0 + rows, :],
                                wg_ref[...], bg_ref[...], seg01)
        v = gv_ref[r0:r0 + rows, :]
        vb = v.astype(BF16)
        o_scr[r0:r0 + rows, :] = (_dot((q * jnp.exp(gc)).astype(BF16), s_all.astype(BF16))
                                  + _gla_pairwise(q, k, v, gc, ex_ref[...], GLA_DIAG))
        half = rows // 2
        while half >= GLA_DIAG:
            vlane = _iota2((half, GLA_DV), 1) // GLA_HEAD_V
            for blk in range(rows // (2 * half)):
                s0 = blk * 2 * half
                t0 = s0 + half
                ref = gc[t0 - 1:t0, :]
                qs = q[t0:t0 + half, :] * jnp.exp(gc[t0:t0 + half, :] - ref)
                ks = k[s0:t0, :] * jnp.exp(ref - gc[s0:t0, :])
                att = _dot_nt(_head_stack(qs, GLA_HEAD_K, GLA_HEADS).astype(BF16), ks.astype(BF16))
                pv = _dot(att.astype(BF16), vb[s0:t0, :])
                ot = jnp.zeros((half, GLA_DV), F32)
                for h in range(GLA_HEADS):
                    ot = jnp.where(vlane == h, pv[h * half:(h + 1) * half, :], ot)
                o_scr[r0 + t0:r0 + t0 + half, :] += ot
            half //= 2
        gc_t = gc.T
        dcol = gc_t[:, rows - 1:rows]
        kd_t = k.T * jnp.exp(dcol - gc_t)
        upd = _dot(kd_t.astype(BF16), vb)
        s_all = jnp.exp(dcol) * s_all + jnp.where(bd, upd, 0.0)
    o_ref[...] = _gla_finish(o_scr[...], gr_ref[...], seg_ref[...], nw_ref[...]).astype(o_ref.dtype)
    s_scr[...] = s_all

    @pl.when(c == pl.num_programs(1) - 1)
    def _():
        st_ref[0] = s_all


def _gla_vec_specs():
    return [_const_spec((LANES, GLA_DK)), _const_spec((1, GLA_DK)), _const_spec((1, GLA_DV)),
            _const_spec((GLA_DK, GLA_DV)), _const_spec((GLA_DV, GLA_DV))]


def _gla_prompt(gq, gk, gv, gr, small, lp, b, l):
    nc = l // GLA_STEP
    rs = lambda w: pl.BlockSpec((GLA_STEP, w), lambda bi, ci: (bi * nc + ci, 0))
    expand, seg = _gla_consts()
    return pl.pallas_call(
        _gla_prompt_kernel,
        grid=(b, nc),
        in_specs=[rs(GLA_DK), rs(GLA_DK), rs(GLA_DV), rs(GLA_DV), rs(LANES)] + _gla_vec_specs(),
        out_specs=[rs(GLA_DV), pl.BlockSpec((1, GLA_DK, GLA_DV), lambda bi, ci: (bi, 0, 0))],
        out_shape=[jax.ShapeDtypeStruct((b * l, GLA_DV), BF16),
                   jax.ShapeDtypeStruct((b, GLA_DK, GLA_DV), F32)],
        scratch_shapes=[pltpu.VMEM((GLA_DK, GLA_DV), F32), pltpu.VMEM((GLA_STEP, GLA_DV), F32)],
        compiler_params=_params(("arbitrary", "arbitrary")),
        name="gla_prompt",
    )(gq, gk, gv, gr, small, lp["gla_wg"], lp["gla_bg"], lp["gla_nw"], expand, seg)


def _gla_sample_kernel(gq_ref, gk_ref, gv_ref, gr_ref, small_ref, s0_ref, wg_ref, bg_ref, nw_ref, ex_ref, seg_ref,
                       o_ref, st_ref, *, seq):
    rows = gq_ref.shape[0]
    nseq = rows // seq
    qi = _iota2((rows, rows), 0)
    si = _iota2((rows, rows), 1)
    seg01 = jnp.where(((qi // seq) == (si // seq)) & (si <= qi), 1.0, 0.0).astype(BF16)
    last01 = jnp.where(si == (qi // seq) * seq + (seq - 1), 1.0, 0.0).astype(BF16)
    q, k, gc = _gla_prepare(gq_ref[...], gk_ref[...], small_ref[...], wg_ref[...], bg_ref[...], seg01)
    v = gv_ref[...]
    vb = v.astype(BF16)
    gc_last = _split3_dot(last01, gc)
    qg = (q * jnp.exp(gc)).astype(BF16)
    kd_t = (k * jnp.exp(gc_last - gc)).T
    dec_t = jnp.exp(gc_last).T
    colseq = _iota2((GLA_DK, rows), 1) // seq
    rowseq = _iota2((rows, GLA_DV), 0) // seq
    bd = (_iota2((GLA_DK, GLA_DV), 0) // GLA_HEAD_K) == (_iota2((GLA_DK, GLA_DV), 1) // GLA_HEAD_V)
    o = _gla_pairwise(q, k, v, gc, ex_ref[...], seq)
    for b in range(nseq):
        s0 = s0_ref[b]
        o = o + jnp.where(rowseq == b, _dot(qg, s0.astype(BF16)), 0.0)
        upd = _dot(jnp.where(colseq == b, kd_t, 0.0).astype(BF16), vb)
        st_ref[b] = dec_t[:, b * seq:b * seq + 1] * s0 + jnp.where(bd, upd, 0.0)
    o_ref[...] = _gla_finish(o, gr_ref[...], seg_ref[...], nw_ref[...]).astype(o_ref.dtype)


def _gla_sample(gq, gk, gv, gr, small, s0, lp, b, l):
    rows = SAMPLE_SEQS * l
    rs = lambda w: pl.BlockSpec((rows, w), lambda i: (i, 0))
    st = pl.BlockSpec((SAMPLE_SEQS, GLA_DK, GLA_DV), lambda i: (i, 0, 0))
    expand, seg = _gla_consts()
    return pl.pallas_call(
        functools.partial(_gla_sample_kernel, seq=l),
        grid=(b // SAMPLE_SEQS,),
        in_specs=[rs(GLA_DK), rs(GLA_DK), rs(GLA_DV), rs(GLA_DV), rs(LANES), st] + _gla_vec_specs(),
        out_specs=[rs(GLA_DV), st],
        out_shape=[jax.ShapeDtypeStruct((b * l, GLA_DV), BF16),
                   jax.ShapeDtypeStruct((b, GLA_DK, GLA_DV), F32)],
        compiler_params=_params(("arbitrary",)),
        name="gla_sample",
    )(gq, gk, gv, gr, small, s0, lp["gla_wg"], lp["gla_bg"], lp["gla_nw"], expand, seg)


def _gla_state_expand(s):
    b = s.shape[0]
    eye = jnp.eye(GLA_HEADS, dtype=s.dtype)
    return (s[:, :, :, None, :] * eye[None, :, None, :, None]).reshape(b, GLA_DK, GLA_DV)


def _gla_state_extract(s):
    b = s.shape[0]
    s5 = s.reshape(b, GLA_HEADS, GLA_HEAD_K, GLA_HEADS, GLA_HEAD_V)
    return jnp.stack([s5[:, h, :, h, :] for h in range(GLA_HEADS)], axis=1)


def _att_head_pair(qp, kp, vp, valid, m_prev, l_prev, a_prev):
    lane_half = _iota2(qp.shape, 1) // ATT_HEAD_DIM
    m_new, l_new, a_new = [], [], a_prev
    for half in range(2):
        s = _dot_nt(jnp.where(lane_half == half, qp, 0.0).astype(BF16), kp)
        s = jnp.where(valid, s, NEG)
        smax = jnp.max(s, axis=-1, keepdims=True)
        if m_prev is None:
            mn = jnp.broadcast_to(smax, qp.shape)
        else:
            mn = jnp.maximum(m_prev[half], smax)
        pr = jnp.exp(s - jnp.concatenate([mn] * (s.shape[1] // LANES), axis=1))
        psum = jnp.sum(pr, axis=-1, keepdims=True)
        pv = _dot(pr.astype(BF16), vp)
        if m_prev is None:
            ln = jnp.broadcast_to(psum, qp.shape)
            an = pv
        else:
            alpha = jnp.exp(m_prev[half] - mn)
            ln = l_prev[half] * alpha + psum
            an = a_prev * alpha + pv
        a_new = an if half == 0 else jnp.where(lane_half == half, an, a_new)
        m_new.append(mn)
        l_new.append(ln)
    return m_new, l_new, a_new


def _att_prompt_kernel(q_ref, k_ref, v_ref, out_ref, acc_scr, m_scr, l_scr):
    tq = ATT_BLOCK
    sup = out_ref.shape[1]
    base = pl.program_id(1) * sup
    rel = tq + _iota2((tq, 2 * tq), 0) - _iota2((tq, 2 * tq), 1)
    band = (rel >= 0) & (rel <= ATT_KEYS - 1)
    in_cur = _iota2((tq, 2 * tq), 1) >= tq
    strides = sorted((d for _, d in DILATION_PATTERNS), reverse=True)
    for idx, d in enumerate(strides):
        first, last = idx == 0, idx == len(strides) - 1

        def rows(start, d=d):
            return pl.ds(start, tq, stride=d) if d > 1 else pl.ds(start, tq)

        def body(sb, carry, d=d, first=first, last=last, rows=rows):
            if d > 1:
                r = sb % d
                mi = sb // d
                loc = r + d * tq * mi
                start_q = base + loc
            else:
                mi = sb
                loc = pl.multiple_of(sb * tq, tq)
                start_q = pl.multiple_of(base + loc, tq)
            mglob = base // (d * tq) + mi
            start_p = jnp.where(mglob == 0, start_q, start_q - d * tq)
            valid = band & ((mglob > 0) | in_cur)
            npair = ATT_HEADS // 2
            prev = []
            for p in range(npair):
                if first:
                    prev.append((None, None, None))
                else:
                    prev.append(([m_scr[2 * p + hf, rows(loc), :] for hf in range(2)],
                                 [l_scr[2 * p + hf, rows(loc), :] for hf in range(2)],
                                 acc_scr[p, rows(loc), :]))
            new = []
            for p in range(npair):
                qp = (q_ref[p, rows(start_q), :] * (ATT_HEAD_DIM ** -0.5)).astype(BF16)
                kp = jnp.concatenate([k_ref[p, rows(start_p), :], k_ref[p, rows(start_q), :]], axis=0).astype(BF16)
                vp = jnp.concatenate([v_ref[p, rows(start_p), :], v_ref[p, rows(start_q), :]], axis=0).astype(BF16)
                new.append(_att_head_pair(qp, kp, vp, valid, *prev[p]))
            lane_half = _iota2((tq, LANES), 1) // ATT_HEAD_DIM
            for p in range(npair):
                m_new, l_new, a_new = new[p]
                if last:
                    den = jnp.where(lane_half == 0, l_new[0], l_new[1])
                    out_ref[p, rows(loc), :] = (a_new / den).astype(out_ref.dtype)
                else:
                    for hf in range(2):
                        m_scr[2 * p + hf, rows(loc), :] = m_new[hf]
                        l_scr[2 * p + hf, rows(loc), :] = l_new[hf]
                    acc_scr[p, rows(loc), :] = a_new
            return carry

        lax.fori_loop(0, sup // tq, body, 0)


def _att_prompt(aq, ak, av, b, l):
    npair = ATT_HEADS // 2
    sup = ATT_BLOCK * max(d for _, d in DILATION_PATTERNS)
    assert l % sup == 0
    seq = pl.BlockSpec((npair, l, LANES), lambda bi, j: (0, bi, 0), pipeline_mode=pl.Buffered(1))
    return pl.pallas_call(
        _att_prompt_kernel,
        grid=(b, l // sup),
        in_specs=[seq, seq, seq],
        out_specs=pl.BlockSpec((npair, sup, LANES), lambda bi, j: (0, bi * (l // sup) + j, 0)),
        out_shape=jax.ShapeDtypeStruct((npair, b * l, LANES), BF16),
        scratch_shapes=[pltpu.VMEM((npair, sup, LANES), F32), pltpu.VMEM((ATT_HEADS, sup, LANES), F32),
                        pltpu.VMEM((ATT_HEADS, sup, LANES), F32)],
        compiler_params=_params(("arbitrary", "arbitrary")),
        name="att_prompt",
    )(aq, ak, av)


def _att_counts(seq, nbuf):
    qpos = nbuf + np.arange(seq)[:, None]
    kpos = np.arange(nbuf + seq)[None, :]
    delta = qpos - kpos
    cnt = np.zeros(delta.shape, np.float32)
    for window, stride in DILATION_PATTERNS:
        cnt += ((delta >= 0) & (delta % stride == 0) & (delta <= window)).astype(np.float32)
    cnt = np.tile(cnt, (2, 1))
    new = np.zeros((2 * seq, LANES), np.float32)
    new[:, :seq] = cnt[:, nbuf:]
    return jnp.asarray(cnt[:, :nbuf]), jnp.asarray(new)


def _att_sample_kernel(q_ref, kn_ref, vn_ref, kc_ref, vc_ref, cc_ref, cn_ref, out_ref):
    seq = q_ref.shape[1]
    cc = cc_ref[...]
    cn = cn_ref[...]
    pad = jnp.zeros((LANES - seq, LANES), F32)
    lane_half = _iota2((seq, LANES), 1) // ATT_HEAD_DIM
    for p in range(ATT_HEADS // 2):
        q = q_ref[p] * (ATT_HEAD_DIM ** -0.5)
        q2 = jnp.concatenate([jnp.where(lane_half == 0, q, 0.0), jnp.where(lane_half == 1, q, 0.0)],
                             axis=0).astype(BF16)
        kt = kc_ref[0, 2 * p:2 * p + 2].reshape(LANES, -1).astype(BF16)
        vt = vc_ref[0, 2 * p:2 * p + 2].reshape(LANES, -1).astype(BF16)
        kn = jnp.concatenate([kn_ref[p], pad], axis=0).astype(BF16)
        vn = jnp.concatenate([vn_ref[p], pad], axis=0).astype(BF16)
        sc = jnp.where(cc > 0, _dot(q2, kt), NEG)
        sn = jnp.where(cn > 0, _dot_nt(q2, kn), NEG)
        m = jnp.maximum(jnp.max(sc, axis=-1, keepdims=True), jnp.max(sn, axis=-1, keepdims=True))
        pc = cc * jnp.exp(sc - m)
        pn = cn * jnp.exp(sn - m)
        den = jnp.sum(pc, axis=-1, keepdims=True) + jnp.sum(pn, axis=-1, keepdims=True)
        o = (_dot_nt(pc.astype(BF16), vt) + _dot(pn.astype(BF16), vn)) / den
        out_ref[p] = jnp.where(lane_half == 0, o[0:seq, :], o[seq:2 * seq, :]).astype(out_ref.dtype)


def _att_sample(aq, ak, av, kcache_t, vcache_t, layer, b, l):
    npair = ATT_HEADS // 2
    nbuf = kcache_t.shape[-1]
    cc, cn = _att_counts(l, nbuf)
    new = pl.BlockSpec((npair, l, LANES), lambda i: (0, i, 0))
    cache = pl.BlockSpec((None, 1, ATT_HEADS, ATT_HEAD_DIM, nbuf), lambda i: (layer, i, 0, 0, 0))
    return pl.pallas_call(
        _att_sample_kernel,
        grid=(b,),
        in_specs=[new, new, new, cache, cache, _const_spec((2 * l, nbuf)), _const_spec((2 * l, LANES))],
        out_specs=new,
        out_shape=jax.ShapeDtypeStruct((npair, b * l, LANES), BF16),
        compiler_params=_params(("arbitrary",)),
        name="att_sample",
    )(aq, ak, av, kcache_t, vcache_t, cc, cn)


def _pad_lanes(v, width, offset=0):
    out = jnp.zeros((1, width), F32)
    return out.at[0, offset:offset + v.shape[0]].set(v.astype(F32))


def _layer_params(l, w_in, w_out, ssd_conv_w, ssd_conv_b, ssd_dt_bias, ssd_a_log, ssd_d, ssd_norm_w,
                  gla_w_gate, gla_b_gate, gla_norm_w, norm_w, ffn1_w_in, ffn1_w_out, ffn2_w_in, ffn2_w_out):
    offs = np.concatenate([[0], np.cumsum(IN_SPLITS)])
    cols = {n: w_in[l][:, offs[i]:offs[i + 1]] for i, n in enumerate(
        ("z", "xbc", "dt", "gq", "gk", "gv", "gr", "glr", "aq", "ak", "av"))}
    small = jnp.zeros((D_MODEL, LANES), F32)
    small = small.at[:, SMALL_DT_OFF:SMALL_DT_OFF + SSD_HEADS].set(cols["dt"])
    small = small.at[:, SMALL_GLR_OFF:SMALL_GLR_OFF + GLA_GATE_RANK].set(cols["glr"])
    cols["small"] = small
    w_in_p = jnp.concatenate([cols[n] for n, _ in PROJ_GROUPS], axis=1).astype(BF16)
    wg = jnp.zeros((LANES, GLA_DK), F32).at[SMALL_GLR_OFF:SMALL_GLR_OFF + GLA_GATE_RANK, :].set(gla_w_gate[l])
    return dict(
        w_in_p=w_in_p, w_out=w_out[l].astype(BF16),
        conv_w=ssd_conv_w[l], conv_b=ssd_conv_b[l].reshape(1, SSD_CONV_DIM),
        dtb=_pad_lanes(ssd_dt_bias[l], LANES, SMALL_DT_OFF), alog=_pad_lanes(ssd_a_log[l], LANES, SMALL_DT_OFF),
        dvec=jnp.repeat(ssd_d[l].astype(F32), SSD_HEAD_DIM).reshape(1, SSD_INNER),
        ssd_nw=ssd_norm_w[l].reshape(1, SSD_INNER),
        gla_wg=wg.astype(BF16), gla_bg=gla_b_gate[l].reshape(1, GLA_DK),
        gla_nw=jnp.tile(gla_norm_w[l], GLA_HEADS).reshape(1, GLA_DV),
        norm_w=norm_w[l],
        ffn1_in=ffn1_w_in[l].astype(BF16), ffn1_out=ffn1_w_out[l].astype(BF16),
        ffn2_in=ffn2_w_in[l].astype(BF16), ffn2_out=ffn2_w_out[l].astype(BF16),
    )


def _trunk(x, mods, layers, norm_f, states, sample):
    b, l, _ = x.shape
    keep = min(ATT_MAX_WINDOW, l)
    outs = ([], [], [], [], [])
    for li, lp in enumerate(layers):
        mod = mods[li]
        x = _ffn(x, mod, 0, lp["norm_w"][0], lp["ffn1_in"], lp["ffn1_out"])
        proj = dict(zip([n for n, _ in PROJ_GROUPS], _inproj(x, mod, lp["norm_w"][1], lp["w_in_p"])))
        if sample:
            st_ssd, st_conv, st_gla, kcache, vcache = states
            bufp = jnp.pad(st_conv[li], ((0, 0), (0, l - (SSD_CONV - 1)), (0, 0))).reshape(b * l, SSD_CONV_DIM)
            y, ssd_new = _ssd_sample(proj["z"], proj["xbc"], proj["small"], bufp, st_ssd[li], lp, b, l)
            o, gla_new = _gla_sample(proj["gq"], proj["gk"], proj["gv"], proj["gr"], proj["small"],
                                     _gla_state_expand(st_gla[li]), lp, b, l)
            att = _att_sample(proj["aq"], proj["ak"], proj["av"], kcache, vcache, li, b, l)
        else:
            y, ssd_new = _ssd_prompt(proj["z"], proj["xbc"], proj["small"], lp, b, l)
            o, gla_new = _gla_prompt(proj["gq"], proj["gk"], proj["gv"], proj["gr"], proj["small"], lp, b, l)
            att = _att_prompt(proj["aq"], proj["ak"], proj["av"], b, l)
        x = _ffn(x, mod, 6, lp["norm_w"][2], lp["ffn2_in"], lp["ffn2_out"],
                 premix=(y, o, att, 5, lp["w_out"]),
                 final_norm=norm_f if li == len(layers) - 1 else None)
        outs[0].append(ssd_new)
        outs[1].append(proj["xbc"].reshape(b, l, SSD_CONV_DIM)[:, l - (SSD_CONV - 1):])
        outs[2].append(_gla_state_extract(gla_new))
        for acc, name in ((outs[3], "ak"), (outs[4], "av")):
            kv = proj[name].reshape(ATT_HEADS // 2, b, l, 2, ATT_HEAD_DIM)[:, :, l - keep:]
            acc.append(kv.transpose(1, 2, 0, 3, 4).reshape(b, keep, ATT_HEADS, ATT_HEAD_DIM))
    return x, [jnp.stack(a) for a in outs]


def kernel(x_prompt, x_sample, c_prompt, c_sample, state_ssd, state_ssd_conv, state_gla, cache_attn_k, cache_attn_v,
           w_in, w_out, ssd_conv_w, ssd_conv_b, ssd_dt_bias, ssd_a_log, ssd_d, ssd_norm_w,
           gla_w_gate, gla_b_gate, gla_norm_w, norm_w, w_mod, b_mod,
           ffn1_w_in, ffn1_w_out, ffn2_w_in, ffn2_w_out, norm_f):
    bp, bs = x_prompt.shape[0], x_sample.shape[0]
    depth = w_in.shape[0]
    layers = [_layer_params(l, w_in, w_out, ssd_conv_w, ssd_conv_b, ssd_dt_bias, ssd_a_log, ssd_d, ssd_norm_w,
                            gla_w_gate, gla_b_gate, gla_norm_w, norm_w, ffn1_w_in, ffn1_w_out, ffn2_w_in,
                            ffn2_w_out) for l in range(depth)]
    npad = -(bp + bs) % SUBLANES
    c_all = jnp.concatenate([c_prompt, c_sample, jnp.zeros((npad, D_MODEL), F32)], axis=0)
    mods_p, mods_s = [], []
    for l in range(depth):
        m = _modulation(c_all, w_mod[l].astype(BF16), b_mod[l])
        m = m.reshape(-1, ADALN_MODS, 1, D_MODEL).transpose(1, 0, 2, 3)
        mods_p.append(m[:, :bp])
        mods_s.append(m[:, bp:bp + bs])
    kcache = cache_attn_k.transpose(0, 1, 3, 4, 2)
    vcache = cache_attn_v.transpose(0, 1, 3, 4, 2)
    y_p, (ssd_p, conv_p, gla_p, k_p, v_p) = _trunk(x_prompt, mods_p, layers, norm_f, None, sample=False)
    y_s, (ssd_s, conv_s, gla_s, k_s, v_s) = _trunk(
        x_sample, mods_s, layers, norm_f, (state_ssd, state_ssd_conv, state_gla, kcache, vcache), sample=True)
    return (y_p, y_s, ssd_p, ssd_s, conv_p, conv_s, gla_p, gla_s, k_p, k_s, v_p, v_s)
```

```python
import functools
import math

import numpy as np
import jax
import jax.numpy as jnp
from jax import lax
from jax.experimental import pallas as pl
from jax.experimental.pallas import tpu as pltpu

F32 = jnp.float32
BF16 = jnp.bfloat16

D_MODEL = 1024
DEPTH = 2
SSD_HEADS = 6
SSD_HEAD_DIM = 64
SSD_INNER = SSD_HEADS * SSD_HEAD_DIM
SSD_GROUPS = 2
SSD_STATE = 128
SSD_CONV = 4
SSD_CONV_DIM = SSD_INNER + 2 * SSD_GROUPS * SSD_STATE
GLA_HEADS = 4
GLA_HEAD_K = 32
GLA_HEAD_V = 64
GLA_DK = GLA_HEADS * GLA_HEAD_K
GLA_DV = GLA_HEADS * GLA_HEAD_V
GLA_GATE_RANK = 16
GLA_TAU = 16.0
ATT_HEADS = 6
ATT_HEAD_DIM = 64
ATT_DIM = ATT_HEADS * ATT_HEAD_DIM
DILATION_PATTERNS = ((128, 1), (512, 4), (2048, 16))
ATT_MAX_WINDOW = 2048
ATT_KEYS = 129
D_MIX = SSD_INNER + GLA_DV + ATT_DIM
IN_SPLITS = (SSD_INNER, SSD_CONV_DIM, SSD_HEADS, GLA_DK, GLA_DK, GLA_DV, GLA_DV, GLA_GATE_RANK,
             ATT_DIM, ATT_DIM, ATT_DIM)
D_FF = 2816
ADALN_MODS = 9
FFN_RES = 0.5
EPS = 1e-6

LANES = 128
SUBLANES = 8
VMEM_LIMIT = 56 * 1024 * 1024

PROJ_GROUPS = (("z", SSD_INNER), ("xbc", SSD_CONV_DIM), ("gq", GLA_DK), ("gk", GLA_DK), ("gv", GLA_DV),
               ("gr", GLA_DV), ("aq", ATT_DIM), ("ak", ATT_DIM), ("av", ATT_DIM), ("small", LANES))
PROJ_WIDTH = sum(w for _, w in PROJ_GROUPS)
PAIR_MAJOR = ("aq", "ak", "av")
KV_T = ("ak", "av")
SMALL_DT_OFF = 0
SMALL_GLR_OFF = 8

ROW_TILE = 512
FF_CHUNK = 256
SSD_CHUNK = 128
GLA_CHUNK = 128
GLA_STEP = 256
GLA_DIAG = 16
SAMPLE_SEQS = 16
ATT_BLOCK = 128
NEG = -1e30


def _dot(a, b):
    return jnp.dot(a, b, preferred_element_type=F32)


def _dot_nt(a, b):
    return lax.dot_general(a, b, (((1,), (1,)), ((), ())), preferred_element_type=F32)


def _sigmoid(x):
    return 1.0 / (1.0 + jnp.exp(-x))


def _silu(x):
    return x * _sigmoid(x)


def _softplus(x):
    return jnp.maximum(x, 0.0) + jnp.log1p(jnp.exp(-jnp.abs(x)))


def _split3_dot(m01, a):
    a1 = a.astype(BF16)
    r1 = a - a1.astype(F32)
    a2 = r1.astype(BF16)
    a3 = (r1 - a2.astype(F32)).astype(BF16)
    return _dot(m01, a1) + _dot(m01, a2) + _dot(m01, a3)


def _split2_dot(a, m01):
    a1 = a.astype(BF16)
    a2 = (a - a1.astype(F32)).astype(BF16)
    return _dot(a1, m01) + _dot(a2, m01)


def _rms_mod(x, nw, shift, scale):
    ms = jnp.mean(x * x, axis=-1, keepdims=True)
    y = x * lax.rsqrt(ms + EPS) * nw
    return y * (1.0 + scale) + shift


def _iota2(shape, axis):
    return lax.broadcasted_iota(jnp.int32, shape, axis)


def _params(sem):
    return pltpu.CompilerParams(dimension_semantics=sem, vmem_limit_bytes=VMEM_LIMIT)


def _const_spec(shape, layer=None):
    nd = len(shape)
    if layer is None:
        return pl.BlockSpec(shape, lambda *_: (0,) * nd, pipeline_mode=pl.Buffered(1))
    return pl.BlockSpec((None,) + tuple(shape), lambda *_: (layer,) + (0,) * nd, pipeline_mode=pl.Buffered(1))


def _mod_kernel(c_ref, w_ref, b_ref, o_ref):
    c = c_ref[...]
    o_ref[...] = _dot(_silu(c).astype(BF16), w_ref[...]) + b_ref[...]


def _modulation(c_all, w_mod, b_mod):
    n, d = c_all.shape
    depth, _, nout = w_mod.shape
    tn = D_MODEL
    return pl.pallas_call(
        _mod_kernel,
        grid=(depth, nout // tn),
        in_specs=[pl.BlockSpec((n, d), lambda l, j: (0, 0)),
                  pl.BlockSpec((None, d, tn), lambda l, j: (l, 0, j)),
                  pl.BlockSpec((None, 1, tn), lambda l, j: (l, 0, j))],
        out_specs=pl.BlockSpec((None, n, tn), lambda l, j: (l, 0, j)),
        out_shape=jax.ShapeDtypeStruct((depth, n, nout), F32),
        compiler_params=_params(("arbitrary", "arbitrary")),
        name="adaln_mod",
    )(c_all, w_mod, b_mod.reshape(depth, 1, nout))


def _row_tiling(b, l):
    if l >= ROW_TILE:
        assert l % ROW_TILE == 0
        return 1, ROW_TILE
    assert ROW_TILE % l == 0 and b % (ROW_TILE // l) == 0
    return ROW_TILE // l, l


def _x_spec(bb, ll, nlb):
    return pl.BlockSpec((bb, ll, D_MODEL), lambda i: (i // nlb, i % nlb, 0))


def _mod_spec(k, bb, nlb):
    return pl.BlockSpec((1, bb, 1, D_MODEL), lambda i: (k, i // nlb, 0, 0))


def _rows_spec(r, width):
    return pl.BlockSpec((r, width), lambda i: (i, 0))


def _pair_spec(r, width):
    return pl.BlockSpec((width // LANES, r, LANES), lambda i: (0, i, 0))


def _ffn_kernel(*refs, premix, final):
    refs = list(refs)
    x_ref = refs.pop(0)
    if premix:
        y_ref, o_ref, a_ref, g2_ref, wo_ref = refs[:5]
        refs = refs[5:]
    sh_ref, sc_ref, gt_ref, nw_ref, win_ref, wout_ref = refs[:6]
    refs = refs[6:]
    if final:
        nf_ref = refs.pop(0)
    out_ref, act_ref = refs
    bb, ll, d = x_ref.shape
    r = bb * ll
    x = x_ref[...]
    if premix:
        mix = (_dot(y_ref[...], wo_ref[0:SSD_INNER, :])
               + _dot(o_ref[...], wo_ref[SSD_INNER:SSD_INNER + GLA_DV, :])
               + sum(_dot(a_ref[p], wo_ref[SSD_INNER + GLA_DV + p * LANES:SSD_INNER + GLA_DV + (p + 1) * LANES, :])
                     for p in range(ATT_DIM // LANES)))
        x = x + g2_ref[0] * mix.reshape(bb, ll, d)
    h = _rms_mod(x, nw_ref[...], sh_ref[0], sc_ref[0]).reshape(r, d).astype(BF16)
    for c in range(D_FF // FF_CHUNK):
        g = _dot(h, win_ref[:, c * FF_CHUNK:(c + 1) * FF_CHUNK])
        u = _dot(h, win_ref[:, D_FF + c * FF_CHUNK:D_FF + (c + 1) * FF_CHUNK])
        act_ref[:, c * FF_CHUNK:(c + 1) * FF_CHUNK] = (_silu(g) * u).astype(BF16)
    y = _dot(act_ref[...], wout_ref[...])
    x = x + FFN_RES * gt_ref[0] * y.reshape(bb, ll, d)
    if final:
        ms = jnp.mean(x * x, axis=-1, keepdims=True)
        x = x * lax.rsqrt(ms + EPS) * nf_ref[...]
    out_ref[...] = x


def _ffn(x, mod, mod_base, norm_w, w_in, w_out, layer, premix=None, final_norm=None):
    b, l, d = x.shape
    bb, ll = _row_tiling(b, l)
    nlb = l // ll
    r = bb * ll
    nsteps = (b // bb) * nlb
    args, specs = [x], [_x_spec(bb, ll, nlb)]
    if premix is not None:
        y, o, a, gate_row, wo = premix
        args += [y, o, a, mod, wo]
        specs += [_rows_spec(r, SSD_INNER), _rows_spec(r, GLA_DV), _pair_spec(r, ATT_DIM),
                  _mod_spec(gate_row, bb, nlb), _const_spec((D_MIX, d), layer)]
    args += [mod, mod, mod, norm_w.reshape(1, d), w_in, w_out]
    specs += [_mod_spec(mod_base, bb, nlb), _mod_spec(mod_base + 1, bb, nlb), _mod_spec(mod_base + 2, bb, nlb),
              _const_spec((1, d)), _const_spec((d, 2 * D_FF), layer), _const_spec((D_FF, d), layer)]
    if final_norm is not None:
        args.append(final_norm.reshape(1, d))
        specs.append(_const_spec((1, d)))
    return pl.pallas_call(
        functools.partial(_ffn_kernel, premix=premix is not None, final=final_norm is not None),
        grid=(nsteps,),
        in_specs=specs,
        out_specs=_x_spec(bb, ll, nlb),
        out_shape=jax.ShapeDtypeStruct((b, l, d), F32),
        scratch_shapes=[pltpu.VMEM((r, D_FF), BF16)],
        compiler_params=_params(("arbitrary",)),
        name="ffn",
    )(*args)


def _inproj_kernel(x_ref, sh_ref, sc_ref, nw_ref, w_ref, *out_refs, first_kept):
    bb, ll, d = x_ref.shape
    h = _rms_mod(x_ref[...], nw_ref[...], sh_ref[0], sc_ref[0]).reshape(bb * ll, d).astype(BF16)
    off = 0
    for ref, (name, width) in zip(out_refs, PROJ_GROUPS):
        res = _dot(h, w_ref[:, off:off + width])
        if name in PAIR_MAJOR:
            for p in range(width // LANES):
                ref[p] = res[:, p * LANES:(p + 1) * LANES]
        else:
            ref[...] = res
        if first_kept is not None and name in KV_T:
            t_ref = out_refs[len(PROJ_GROUPS) + KV_T.index(name)]

            @pl.when(pl.program_id(0) % first_kept[1] >= first_kept[0])
            def _(t_ref=t_ref, res=res):
                t_ref[0] = res.T
        off += width


def _inproj(x, mod, norm_w, w_in_p, keep_t=None):
    b, l, d = x.shape
    bb, ll = _row_tiling(b, l)
    nlb = l // ll
    r = bb * ll
    out_specs = [_pair_spec(r, w) if n in PAIR_MAJOR else _rows_spec(r, w) for n, w in PROJ_GROUPS]
    out_shape = [jax.ShapeDtypeStruct((w // LANES, b * l, LANES) if n in PAIR_MAJOR else (b * l, w), F32)
                 for n, w in PROJ_GROUPS]
    first_kept = None
    if keep_t is not None:
        assert bb == 1 and keep_t % ll == 0 and keep_t <= l
        skip = nlb - keep_t // ll
        first_kept = (skip, nlb)
        t_spec = pl.BlockSpec((1, ATT_DIM, ll), lambda i: (i // nlb, 0, jnp.maximum(i % nlb - skip, 0)))
        out_specs += [t_spec] * len(KV_T)
        out_shape += [jax.ShapeDtypeStruct((b, ATT_DIM, keep_t), F32)] * len(KV_T)
    return pl.pallas_call(
        functools.partial(_inproj_kernel, first_kept=first_kept),
        grid=((b // bb) * nlb,),
        in_specs=[_x_spec(bb, ll, nlb), _mod_spec(3, bb, nlb), _mod_spec(4, bb, nlb),
                  _const_spec((1, d)), _const_spec((d, PROJ_WIDTH))],
        out_specs=out_specs,
        out_shape=out_shape,
        compiler_params=_params(("arbitrary",)),
        name="inproj",
    )(x, mod, mod, norm_w.reshape(1, d), w_in_p)


def _ssd_prepare(conv, small, dtb, alog, seg01, seglast01):
    xc = _silu(conv)
    xs = xc[:, 0:SSD_INNER]
    bm = xc[:, SSD_INNER:SSD_INNER + SSD_GROUPS * SSD_STATE]
    cm = xc[:, SSD_INNER + SSD_GROUPS * SSD_STATE:SSD_CONV_DIM]
    dt = _softplus(small + dtb)
    a = dt * (-jnp.exp(alog))
    cum = _split3_dot(seg01, a)
    if seglast01 is None:
        cum_last = jnp.broadcast_to(cum[cum.shape[0] - 1:, :], cum.shape)
    else:
        cum_last = _split3_dot(seglast01, cum)
    return xs, bm, cm, dt, cum, cum_last


def _ssd_diag(xs, bm, cm, dt, cum, cum_t, mask):
    ydiag, xdt = [], []
    cb = [_dot_nt(cm[:, g * SSD_STATE:(g + 1) * SSD_STATE].astype(BF16),
                  bm[:, g * SSD_STATE:(g + 1) * SSD_STATE].astype(BF16)) for g in range(SSD_GROUPS)]
    for h in range(SSD_HEADS):
        g = h // (SSD_HEADS // SSD_GROUPS)
        diff = cum[:, h:h + 1] - cum_t[h:h + 1, :]
        decay = jnp.exp(jnp.where(mask, diff, NEG))
        xh = xs[:, h * SSD_HEAD_DIM:(h + 1) * SSD_HEAD_DIM] * dt[:, h:h + 1]
        xdt.append(xh)
        ydiag.append(_dot((cb[g] * decay).astype(BF16), xh.astype(BF16)))
    return ydiag, xdt


def _ssd_finish(y, xs, z, dvec, normw):
    y = (y + dvec * xs) * _silu(z)
    sq = y * y
    half = SSD_INNER // SSD_GROUPS
    lane = _iota2(y.shape, 1)
    s0 = jnp.sum(jnp.where(lane < half, sq, 0.0), axis=-1, keepdims=True)
    s1 = jnp.sum(jnp.where(lane >= half, sq, 0.0), axis=-1, keepdims=True)
    ms = jnp.where(lane < half, s0, s1) * (1.0 / half)
    return y * lax.rsqrt(ms + EPS) * normw


def _ssd_prompt_kernel(z_ref, xbc_ref, small_ref, cw_ref, cb_ref, dtb_ref, alog_ref, dvec_ref, nw_ref,
                       y_ref, st_ref, h_scr, tail_scr):
    c = pl.program_id(1)
    rows = xbc_ref.shape[0]

    @pl.when(c == 0)
    def _():
        h_scr[...] = jnp.zeros_like(h_scr)
        tail_scr[0:SUBLANES, :] = jnp.zeros((SUBLANES, SSD_CONV_DIM), F32)

    u = xbc_ref[...]
    tail_scr[SUBLANES:SUBLANES + rows, :] = u
    conv = u * cw_ref[SSD_CONV - 1:SSD_CONV, :] + cb_ref[...]
    for j in range(1, SSD_CONV):
        conv = conv + tail_scr[SUBLANES - j:SUBLANES - j + rows, :] * cw_ref[SSD_CONV - 1 - j:SSD_CONV - j, :]
    tail_scr[0:SUBLANES, :] = u[rows - SUBLANES:, :]

    qi = _iota2((rows, rows), 0)
    si = _iota2((rows, rows), 1)
    mask = si <= qi
    seg01 = jnp.where(mask, 1.0, 0.0).astype(BF16)
    xs, bm, cm, dt, cum, cum_last = _ssd_prepare(conv, small_ref[...], dtb_ref[...], alog_ref[...], seg01, None)
    cum_t = cum.T
    ydiag, xdt = _ssd_diag(xs, bm, cm, dt, cum, cum_t, mask)
    to_end_t = jnp.exp(cum_last - cum).T
    xdt_t = jnp.concatenate(xdt, axis=1).T
    ys = []
    for h in range(SSD_HEADS):
        g = h // (SSD_HEADS // SSD_GROUPS)
        hprev = h_scr[h]
        cg = cm[:, g * SSD_STATE:(g + 1) * SSD_STATE].astype(BF16)
        yoff = _dot_nt(cg, hprev.astype(BF16)) * jnp.exp(cum[:, h:h + 1])
        ys.append(ydiag[h] + yoff)
        xw_t = xdt_t[h * SSD_HEAD_DIM:(h + 1) * SSD_HEAD_DIM, :] * to_end_t[h:h + 1, :]
        s_local = _dot(xw_t.astype(BF16), bm[:, g * SSD_STATE:(g + 1) * SSD_STATE].astype(BF16))
        h_scr[h] = jnp.exp(cum_last[0:1, h:h + 1]) * hprev + s_local
    y = _ssd_finish(jnp.concatenate(ys, axis=1), xs, z_ref[...], dvec_ref[...], nw_ref[...])
    y_ref[...] = y.astype(y_ref.dtype)

    @pl.when(c == pl.num_programs(1) - 1)
    def _():
        st_ref[0] = h_scr[...]


def _ssd_vec_specs():
    return [_const_spec((SSD_CONV, SSD_CONV_DIM)), _const_spec((1, SSD_CONV_DIM)), _const_spec((1, LANES)),
            _const_spec((1, LANES)), _const_spec((1, SSD_INNER)), _const_spec((1, SSD_INNER))]


def _ssd_prompt(z, xbc, small, lp, b, l):
    nc = l // SSD_CHUNK
    rs = lambda w: pl.BlockSpec((SSD_CHUNK, w), lambda bi, ci: (bi * nc + ci, 0))
    return pl.pallas_call(
        _ssd_prompt_kernel,
        grid=(b, nc),
        in_specs=[rs(SSD_INNER), rs(SSD_CONV_DIM), rs(LANES)] + _ssd_vec_specs(),
        out_specs=[rs(SSD_INNER),
                   pl.BlockSpec((1, SSD_HEADS, SSD_HEAD_DIM, SSD_STATE), lambda bi, ci: (bi, 0, 0, 0))],
        out_shape=[jax.ShapeDtypeStruct((b * l, SSD_INNER), BF16),
                   jax.ShapeDtypeStruct((b, SSD_HEADS, SSD_HEAD_DIM, SSD_STATE), F32)],
        scratch_shapes=[pltpu.VMEM((SSD_HEADS, SSD_HEAD_DIM, SSD_STATE), F32),
                        pltpu.VMEM((SUBLANES + SSD_CHUNK, SSD_CONV_DIM), F32)],
        compiler_params=_params(("arbitrary", "arbitrary")),
        name="ssd_prompt",
    )(z, xbc, small, lp["conv_w"], lp["conv_b"], lp["dtb"], lp["alog"], lp["dvec"], lp["ssd_nw"])


def _ssd_sample_kernel(z_ref, xbc_ref, small_ref, buf_ref, h0_ref, cw_ref, cb_ref, dtb_ref, alog_ref, dvec_ref,
                       nw_ref, y_ref, st_ref, *, seq):
    rows = xbc_ref.shape[0]
    nseq = rows // seq
    u = xbc_ref[...]
    bufp = buf_ref[...]
    tpos = _iota2((rows, SSD_CONV_DIM), 0) % seq
    conv = u * cw_ref[SSD_CONV - 1:SSD_CONV, :] + cb_ref[...]
    for j in range(1, SSD_CONV):
        uj = pltpu.roll(u, j, axis=0)
        back = (rows - (SSD_CONV - 1 - j)) % rows
        bj = pltpu.roll(bufp, back, axis=0) if back else bufp
        conv = conv + jnp.where(tpos < j, bj, uj) * cw_ref[SSD_CONV - 1 - j:SSD_CONV - j, :]

    qi = _iota2((rows, rows), 0)
    si = _iota2((rows, rows), 1)
    same = (qi // seq) == (si // seq)
    mask = same & (si <= qi)
    seg01 = jnp.where(mask, 1.0, 0.0).astype(BF16)
    last01 = jnp.where(si == (qi // seq) * seq + (seq - 1), 1.0, 0.0).astype(BF16)
    xs, bm, cm, dt, cum, cum_last = _ssd_prepare(conv, small_ref[...], dtb_ref[...], alog_ref[...], seg01, last01)
    cum_t = cum.T
    cum_last_t = cum_last.T
    ydiag, xdt = _ssd_diag(xs, bm, cm, dt, cum, cum_t, mask)
    to_end_t = jnp.exp(cum_last_t - cum_t)
    xdt_t = jnp.concatenate(xdt, axis=1).T
    hpg = SSD_HEADS // SSD_GROUPS
    grows = hpg * SSD_HEAD_DIM
    colseq = _iota2((grows, rows), 1) // seq
    yoff_t = []
    for g in range(SSD_GROUPS):
        cg = cm[:, g * SSD_STATE:(g + 1) * SSD_STATE].astype(BF16)
        bg = bm[:, g * SSD_STATE:(g + 1) * SSD_STATE].astype(BF16)
        xw_t = jnp.concatenate(
            [xdt_t[h * SSD_HEAD_DIM:(h + 1) * SSD_HEAD_DIM, :] * to_end_t[h:h + 1, :]
             for h in range(g * hpg, (g + 1) * hpg)], axis=0)
        acc = jnp.zeros((grows, rows), F32)
        for b in range(nseq):
            h0 = h0_ref[b, g * hpg:(g + 1) * hpg].reshape(grows, SSD_STATE)
            acc = jnp.where(colseq == b, _dot_nt(h0.astype(BF16), cg), acc)
            s_local = _dot(jnp.where(colseq == b, xw_t, 0.0).astype(BF16), bg)
            for hh in range(hpg):
                h = g * hpg + hh
                dec = jnp.exp(cum_last_t[h:h + 1, b * seq:b * seq + 1])
                st_ref[b, h] = (dec * h0[hh * SSD_HEAD_DIM:(hh + 1) * SSD_HEAD_DIM, :]
                                + s_local[hh * SSD_HEAD_DIM:(hh + 1) * SSD_HEAD_DIM, :])
        for hh in range(hpg):
            h = g * hpg + hh
            yoff_t.append(acc[hh * SSD_HEAD_DIM:(hh + 1) * SSD_HEAD_DIM, :] * jnp.exp(cum_t[h:h + 1, :]))
    yoff = jnp.concatenate(yoff_t, axis=0).T
    y = _ssd_finish(jnp.concatenate(ydiag, axis=1) + yoff, xs, z_ref[...], dvec_ref[...], nw_ref[...])
    y_ref[...] = y.astype(y_ref.dtype)


def _ssd_sample(z, xbc, small, bufp, h0_all, layer, lp, b, l):
    rows = SAMPLE_SEQS * l
    rs = lambda w: pl.BlockSpec((rows, w), lambda i: (i, 0))
    st = pl.BlockSpec((SAMPLE_SEQS, SSD_HEADS, SSD_HEAD_DIM, SSD_STATE), lambda i: (i, 0, 0, 0))
    st_in = pl.BlockSpec((None, SAMPLE_SEQS, SSD_HEADS, SSD_HEAD_DIM, SSD_STATE), lambda i: (layer, i, 0, 0, 0))
    return pl.pallas_call(
        functools.partial(_ssd_sample_kernel, seq=l),
        grid=(b // SAMPLE_SEQS,),
        in_specs=[rs(SSD_INNER), rs(SSD_CONV_DIM), rs(LANES), rs(SSD_CONV_DIM), st_in] + _ssd_vec_specs(),
        out_specs=[rs(SSD_INNER), st],
        out_shape=[jax.ShapeDtypeStruct((b * l, SSD_INNER), BF16),
                   jax.ShapeDtypeStruct((b, SSD_HEADS, SSD_HEAD_DIM, SSD_STATE), F32)],
        compiler_params=_params(("arbitrary",)),
        name="ssd_sample",
    )(z, xbc, small, bufp, h0_all, lp["conv_w"], lp["conv_b"], lp["dtb"], lp["alog"], lp["dvec"], lp["ssd_nw"])


def _gla_consts():
    rk = np.arange(GLA_DK)[:, None] // GLA_HEAD_K
    cv = np.arange(GLA_DV)[None, :] // GLA_HEAD_V
    expand = (rk == cv).astype(np.float32)
    rv = np.arange(GLA_DV)[:, None] // GLA_HEAD_V
    seg = (rv == cv).astype(np.float32) / GLA_HEAD_V
    return jnp.asarray(expand, BF16), jnp.asarray(seg, BF16)


def _gla_prepare(gq, gk, small, wg, bg, seg01):
    glin = _dot(small.astype(BF16), wg) + bg
    g = -_softplus(-glin) * (1.0 / GLA_TAU)
    gc = _split3_dot(seg01, g)
    q = gq * (GLA_HEAD_K ** -0.5)
    return q, gk, gc


def _gla_pairwise(q, k, v, gc, expand, diag):
    rows = q.shape[0]
    nb = rows // diag
    q4 = q.reshape(nb, 1, diag, GLA_DK)
    g4 = gc.reshape(nb, 1, diag, GLA_DK)
    k4 = k.reshape(nb, diag, 1, GLA_DK)
    gs4 = gc.reshape(nb, diag, 1, GLA_DK)
    shape = (nb, diag, diag, GLA_DK)
    si = lax.broadcasted_iota(jnp.int32, shape, 1)
    ti = lax.broadcasted_iota(jnp.int32, shape, 2)
    w = jnp.exp(jnp.where(si <= ti, g4 - gs4, NEG))
    m = (q4 * k4 * w).reshape(nb * diag * diag, GLA_DK)
    p = _dot(m.astype(BF16), expand).reshape(nb, diag, diag, GLA_DV)
    o = jnp.sum(p * v.reshape(nb, diag, 1, GLA_DV), axis=1)
    return o.reshape(rows, GLA_DV)


def _gla_finish(o, gr, seg, nw):
    ms = _split2_dot(o * o, seg)
    return o * lax.rsqrt(ms + EPS) * nw * _silu(gr)


def _head_stack(x, head_dim, heads):
    lane = _iota2(x.shape, 1) // head_dim
    return jnp.concatenate([jnp.where(lane == h, x, 0.0) for h in range(heads)], axis=0)


def _gla_prompt_kernel(gq_ref, gk_ref, gv_ref, gr_ref, small_ref, wg_ref, bg_ref, nw_ref, ex_ref, seg_ref,
                       o_ref, st_ref, s_scr, o_scr):
    c = pl.program_id(1)
    rows = GLA_CHUNK

    @pl.when(c == 0)
    def _():
        s_scr[...] = jnp.zeros_like(s_scr)

    qi = _iota2((rows, rows), 0)
    si = _iota2((rows, rows), 1)
    seg01 = jnp.where(si <= qi, 1.0, 0.0).astype(BF16)
    bd = (_iota2((GLA_DK, GLA_DV), 0) // GLA_HEAD_K) == (_iota2((GLA_DK, GLA_DV), 1) // GLA_HEAD_V)
    s_all = s_scr[...]
    for ci in range(gq_ref.shape[0] // rows):
        r0 = ci * rows
        q, k, gc = _gla_prepare(gq_ref[r0:r0 + rows, :], gk_ref[r0:r0 + rows, :], small_ref[r0:r0 + rows, :],
                                wg_ref[...], bg_ref[...], seg01)
        v = gv_ref[r0:r0 + rows, :]
        vb = v.astype(BF16)
        o_scr[r0:r0 + rows, :] = (_dot((q * jnp.exp(gc)).astype(BF16), s_all.astype(BF16))
                                  + _gla_pairwise(q, k, v, gc, ex_ref[...], GLA_DIAG))
        half = rows // 2
        while half >= GLA_DIAG:
            vlane = _iota2((half, GLA_DV), 1) // GLA_HEAD_V
            for blk in range(rows // (2 * half)):
                s0 = blk * 2 * half
                t0 = s0 + half
                ref = gc[t0 - 1:t0, :]
                qs = q[t0:t0 + half, :] * jnp.exp(gc[t0:t0 + half, :] - ref)
                ks = k[s0:t0, :] * jnp.exp(ref - gc[s0:t0, :])
                att = _dot_nt(_head_stack(qs, GLA_HEAD_K, GLA_HEADS).astype(BF16), ks.astype(BF16))
                pv = _dot(att.astype(BF16), vb[s0:t0, :])
                ot = jnp.zeros((half, GLA_DV), F32)
                for h in range(GLA_HEADS):
                    ot = jnp.where(vlane == h, pv[h * half:(h + 1) * half, :], ot)
                o_scr[r0 + t0:r0 + t0 + half, :] += ot
            half //= 2
        gc_t = gc.T
        dcol = gc_t[:, rows - 1:rows]
        kd_t = k.T * jnp.exp(dcol - gc_t)
        upd = _dot(kd_t.astype(BF16), vb)
        s_all = jnp.exp(dcol) * s_all + jnp.where(bd, upd, 0.0)
    o_ref[...] = _gla_finish(o_scr[...], gr_ref[...], seg_ref[...], nw_ref[...]).astype(o_ref.dtype)
    s_scr[...] = s_all

    @pl.when(c == pl.num_programs(1) - 1)
    def _():
        st_ref[0] = s_all


def _gla_vec_specs():
    return [_const_spec((LANES, GLA_DK)), _const_spec((1, GLA_DK)), _const_spec((1, GLA_DV)),
            _const_spec((GLA_DK, GLA_DV)), _const_spec((GLA_DV, GLA_DV))]


def _gla_prompt(gq, gk, gv, gr, small, lp, b, l):
    nc = l // GLA_STEP
    rs = lambda w: pl.BlockSpec((GLA_STEP, w), lambda bi, ci: (bi * nc + ci, 0))
    expand, seg = _gla_consts()
    return pl.pallas_call(
        _gla_prompt_kernel,
        grid=(b, nc),
        in_specs=[rs(GLA_DK), rs(GLA_DK), rs(GLA_DV), rs(GLA_DV), rs(LANES)] + _gla_vec_specs(),
        out_specs=[rs(GLA_DV), pl.BlockSpec((1, GLA_DK, GLA_DV), lambda bi, ci: (bi, 0, 0))],
        out_shape=[jax.ShapeDtypeStruct((b * l, GLA_DV), BF16),
                   jax.ShapeDtypeStruct((b, GLA_DK, GLA_DV), F32)],
        scratch_shapes=[pltpu.VMEM((GLA_DK, GLA_DV), F32), pltpu.VMEM((GLA_STEP, GLA_DV), F32)],
        compiler_params=_params(("arbitrary", "arbitrary")),
        name="gla_prompt",
    )(gq, gk, gv, gr, small, lp["gla_wg"], lp["gla_bg"], lp["gla_nw"], expand, seg)


def _gla_sample_kernel(gq_ref, gk_ref, gv_ref, gr_ref, small_ref, s0_ref, wg_ref, bg_ref, nw_ref, ex_ref, seg_ref,
                       o_ref, st_ref, *, seq):
    rows = gq_ref.shape[0]
    nseq = rows // seq
    qi = _iota2((rows, rows), 0)
    si = _iota2((rows, rows), 1)
    seg01 = jnp.where(((qi // seq) == (si // seq)) & (si <= qi), 1.0, 0.0).astype(BF16)
    last01 = jnp.where(si == (qi // seq) * seq + (seq - 1), 1.0, 0.0).astype(BF16)
    q, k, gc = _gla_prepare(gq_ref[...], gk_ref[...], small_ref[...], wg_ref[...], bg_ref[...], seg01)
    v = gv_ref[...]
    vb = v.astype(BF16)
    gc_last = _split3_dot(last01, gc)
    qg = (q * jnp.exp(gc)).astype(BF16)
    kd_t = (k * jnp.exp(gc_last - gc)).T
    dec_t = jnp.exp(gc_last).T
    colseq = _iota2((GLA_DK, rows), 1) // seq
    rowseq = _iota2((rows, GLA_DV), 0) // seq
    bd = (_iota2((GLA_DK, GLA_DV), 0) // GLA_HEAD_K) == (_iota2((GLA_DK, GLA_DV), 1) // GLA_HEAD_V)
    o = _gla_pairwise(q, k, v, gc, ex_ref[...], seq)
    for b in range(nseq):
        s0 = s0_ref[b]
        o = o + jnp.where(rowseq == b, _dot(qg, s0.astype(BF16)), 0.0)
        upd = _dot(jnp.where(colseq == b, kd_t, 0.0).astype(BF16), vb)
        st_ref[b] = dec_t[:, b * seq:b * seq + 1] * s0 + jnp.where(bd, upd, 0.0)
    o_ref[...] = _gla_finish(o, gr_ref[...], seg_ref[...], nw_ref[...]).astype(o_ref.dtype)


def _gla_sample(gq, gk, gv, gr, small, s0, lp, b, l):
    rows = SAMPLE_SEQS * l
    rs = lambda w: pl.BlockSpec((rows, w), lambda i: (i, 0))
    st = pl.BlockSpec((SAMPLE_SEQS, GLA_DK, GLA_DV), lambda i: (i, 0, 0))
    expand, seg = _gla_consts()
    return pl.pallas_call(
        functools.partial(_gla_sample_kernel, seq=l),
        grid=(b // SAMPLE_SEQS,),
        in_specs=[rs(GLA_DK), rs(GLA_DK), rs(GLA_DV), rs(GLA_DV), rs(LANES), st] + _gla_vec_specs(),
        out_specs=[rs(GLA_DV), st],
        out_shape=[jax.ShapeDtypeStruct((b * l, GLA_DV), BF16),
                   jax.ShapeDtypeStruct((b, GLA_DK, GLA_DV), F32)],
        compiler_params=_params(("arbitrary",)),
        name="gla_sample",
    )(gq, gk, gv, gr, small, s0, lp["gla_wg"], lp["gla_bg"], lp["gla_nw"], expand, seg)


def _gla_state_expand(s):
    b = s.shape[0]
    eye = jnp.eye(GLA_HEADS, dtype=s.dtype)
    return (s[:, :, :, None, :] * eye[None, :, None, :, None]).reshape(b, GLA_DK, GLA_DV)


def _gla_state_extract(s):
    b = s.shape[0]
    s5 = s.reshape(b, GLA_HEADS, GLA_HEAD_K, GLA_HEADS, GLA_HEAD_V)
    return jnp.stack([s5[:, h, :, h, :] for h in range(GLA_HEADS)], axis=1)


def _att_head_pair(qp, kp, vp, valid, m_prev, l_prev, a_prev):
    lane_half = _iota2(qp.shape, 1) // ATT_HEAD_DIM
    m_new, l_new, a_new = [], [], a_prev
    for half in range(2):
        s = _dot_nt(jnp.where(lane_half == half, qp, 0.0).astype(BF16), kp)
        s = jnp.where(valid, s, NEG)
        smax = jnp.max(s, axis=-1, keepdims=True)
        if m_prev is None:
            mn = jnp.broadcast_to(smax, qp.shape)
        else:
            mn = jnp.maximum(m_prev[half], smax)
        pr = jnp.exp2(s - jnp.concatenate([mn] * (s.shape[1] // LANES), axis=1))
        psum = jnp.sum(pr, axis=-1, keepdims=True)
        pv = _dot(pr.astype(BF16), vp)
        if m_prev is None:
            ln = jnp.broadcast_to(psum, qp.shape)
            an = pv
        else:
            alpha = jnp.exp2(m_prev[half] - mn)
            ln = l_prev[half] * alpha + psum
            an = a_prev * alpha + pv
        a_new = an if half == 0 else jnp.where(lane_half == half, an, a_new)
        m_new.append(mn)
        l_new.append(ln)
    return m_new, l_new, a_new


def _att_prompt_kernel(q_ref, k_ref, v_ref, out_ref, acc_scr, m_scr, l_scr):
    tq = ATT_BLOCK
    sup = out_ref.shape[1]
    base = pl.program_id(1) * sup
    rel = tq + _iota2((tq, 2 * tq), 0) - _iota2((tq, 2 * tq), 1)
    band = (rel >= 0) & (rel <= ATT_KEYS - 1)
    in_cur = _iota2((tq, 2 * tq), 1) >= tq
    strides = sorted((d for _, d in DILATION_PATTERNS), reverse=True)
    for idx, d in enumerate(strides):
        first, last = idx == 0, idx == len(strides) - 1

        def rows(start, d=d):
            return pl.ds(start, tq, stride=d) if d > 1 else pl.ds(start, tq)

        def body(sb, carry, d=d, first=first, last=last, rows=rows):
            if d > 1:
                r = sb % d
                mi = sb // d
                loc = r + d * tq * mi
                start_q = base + loc
            else:
                mi = sb
                loc = pl.multiple_of(sb * tq, tq)
                start_q = pl.multiple_of(base + loc, tq)
            mglob = base // (d * tq) + mi
            start_p = jnp.where(mglob == 0, start_q, start_q - d * tq)
            valid = band & ((mglob > 0) | in_cur)
            npair = ATT_HEADS // 2
            prev = []
            for p in range(npair):
                if first:
                    prev.append((None, None, None))
                else:
                    prev.append(([m_scr[2 * p + hf, rows(loc), :] for hf in range(2)],
                                 [l_scr[2 * p + hf, rows(loc), :] for hf in range(2)],
                                 acc_scr[p, rows(loc), :]))
            new = []
            for p in range(npair):
                qp = (q_ref[p, rows(start_q), :] * (ATT_HEAD_DIM ** -0.5 * math.log2(math.e))).astype(BF16)
                kp = jnp.concatenate([k_ref[p, rows(start_p), :], k_ref[p, rows(start_q), :]], axis=0).astype(BF16)
                vp = jnp.concatenate([v_ref[p, rows(start_p), :], v_ref[p, rows(start_q), :]], axis=0).astype(BF16)
                new.append(_att_head_pair(qp, kp, vp, valid, *prev[p]))
            lane_half = _iota2((tq, LANES), 1) // ATT_HEAD_DIM
            for p in range(npair):
                m_new, l_new, a_new = new[p]
                if last:
                    den = jnp.where(lane_half == 0, l_new[0], l_new[1])
                    out_ref[p, rows(loc), :] = (a_new / den).astype(out_ref.dtype)
                else:
                    for hf in range(2):
                        m_scr[2 * p + hf, rows(loc), :] = m_new[hf]
                        l_scr[2 * p + hf, rows(loc), :] = l_new[hf]
                    acc_scr[p, rows(loc), :] = a_new
            return carry

        lax.fori_loop(0, sup // tq, body, 0)


def _att_prompt(aq, ak, av, b, l):
    npair = ATT_HEADS // 2
    sup = ATT_BLOCK * max(d for _, d in DILATION_PATTERNS)
    assert l % sup == 0
    seq = pl.BlockSpec((npair, l, LANES), lambda bi, j: (0, bi, 0), pipeline_mode=pl.Buffered(1))
    return pl.pallas_call(
        _att_prompt_kernel,
        grid=(b, l // sup),
        in_specs=[seq, seq, seq],
        out_specs=pl.BlockSpec((npair, sup, LANES), lambda bi, j: (0, bi * (l // sup) + j, 0)),
        out_shape=jax.ShapeDtypeStruct((npair, b * l, LANES), BF16),
        scratch_shapes=[pltpu.VMEM((npair, sup, LANES), F32), pltpu.VMEM((ATT_HEADS, sup, LANES), F32),
                        pltpu.VMEM((ATT_HEADS, sup, LANES), F32)],
        compiler_params=_params(("arbitrary", "arbitrary")),
        name="att_prompt",
    )(aq, ak, av)


def _att_counts(seq, nbuf):
    qpos = nbuf + np.arange(seq)[:, None]
    kpos = np.arange(nbuf + seq)[None, :]
    delta = qpos - kpos
    cnt = np.zeros(delta.shape, np.float32)
    for window, stride in DILATION_PATTERNS:
        cnt += ((delta >= 0) & (delta % stride == 0) & (delta <= window)).astype(np.float32)
    cnt = np.tile(cnt, (2, 1))
    new = np.zeros((2 * seq, LANES), np.float32)
    new[:, :seq] = cnt[:, nbuf:]
    return jnp.asarray(cnt[:, :nbuf]), jnp.asarray(new)


def _att_sample_kernel(q_ref, kn_ref, vn_ref, kc_ref, vc_ref, cc_ref, cn_ref, out_ref):
    seq = q_ref.shape[1]
    cc = cc_ref[...]
    cn = cn_ref[...]
    pad = jnp.zeros((LANES - seq, LANES), F32)
    lane_half = _iota2((seq, LANES), 1) // ATT_HEAD_DIM
    for p in range(ATT_HEADS // 2):
        q = q_ref[p] * (ATT_HEAD_DIM ** -0.5)
        q2 = jnp.concatenate([jnp.where(lane_half == 0, q, 0.0), jnp.where(lane_half == 1, q, 0.0)],
                             axis=0).astype(BF16)
        kt = kc_ref[0, 2 * p:2 * p + 2].reshape(LANES, -1).astype(BF16)
        vt = vc_ref[0, 2 * p:2 * p + 2].reshape(LANES, -1).astype(BF16)
        kn = jnp.concatenate([kn_ref[p], pad], axis=0).astype(BF16)
        vn = jnp.concatenate([vn_ref[p], pad], axis=0).astype(BF16)
        sc = jnp.where(cc > 0, _dot(q2, kt), NEG)
        sn = jnp.where(cn > 0, _dot_nt(q2, kn), NEG)
        m = jnp.maximum(jnp.max(sc, axis=-1, keepdims=True), jnp.max(sn, axis=-1, keepdims=True))
        pc = cc * jnp.exp(sc - m)
        pn = cn * jnp.exp(sn - m)
        den = jnp.sum(pc, axis=-1, keepdims=True) + jnp.sum(pn, axis=-1, keepdims=True)
        o = (_dot_nt(pc.astype(BF16), vt) + _dot(pn.astype(BF16), vn)) / den
        out_ref[p] = jnp.where(lane_half == 0, o[0:seq, :], o[seq:2 * seq, :]).astype(out_ref.dtype)


def _att_sample(aq, ak, av, kcache_t, vcache_t, layer, b, l):
    npair = ATT_HEADS // 2
    nbuf = kcache_t.shape[-1]
    cc, cn = _att_counts(l, nbuf)
    new = pl.BlockSpec((npair, l, LANES), lambda i: (0, i, 0))
    cache = pl.BlockSpec((None, 1, ATT_HEADS, ATT_HEAD_DIM, nbuf), lambda i: (layer, i, 0, 0, 0))
    return pl.pallas_call(
        _att_sample_kernel,
        grid=(b,),
        in_specs=[new, new, new, cache, cache, _const_spec((2 * l, nbuf)), _const_spec((2 * l, LANES))],
        out_specs=new,
        out_shape=jax.ShapeDtypeStruct((npair, b * l, LANES), BF16),
        compiler_params=_params(("arbitrary",)),
        name="att_sample",
    )(aq, ak, av, kcache_t, vcache_t, cc, cn)


def _pad_lanes(v, width, offset=0):
    out = jnp.zeros((1, width), F32)
    return out.at[0, offset:offset + v.shape[0]].set(v.astype(F32))


def _layer_params(l, w_in, ssd_conv_w, ssd_conv_b, ssd_dt_bias, ssd_a_log, ssd_d, ssd_norm_w,
                  gla_w_gate, gla_b_gate, gla_norm_w, norm_w):
    offs = np.concatenate([[0], np.cumsum(IN_SPLITS)])
    cols = {n: w_in[l][:, offs[i]:offs[i + 1]] for i, n in enumerate(
        ("z", "xbc", "dt", "gq", "gk", "gv", "gr", "glr", "aq", "ak", "av"))}
    small = jnp.zeros((D_MODEL, LANES), F32)
    small = small.at[:, SMALL_DT_OFF:SMALL_DT_OFF + SSD_HEADS].set(cols["dt"])
    small = small.at[:, SMALL_GLR_OFF:SMALL_GLR_OFF + GLA_GATE_RANK].set(cols["glr"])
    cols["small"] = small
    w_in_p = jnp.concatenate([cols[n] for n, _ in PROJ_GROUPS], axis=1).astype(BF16)
    wg = jnp.zeros((LANES, GLA_DK), F32).at[SMALL_GLR_OFF:SMALL_GLR_OFF + GLA_GATE_RANK, :].set(gla_w_gate[l])
    return dict(
        w_in_p=w_in_p,
        conv_w=ssd_conv_w[l], conv_b=ssd_conv_b[l].reshape(1, SSD_CONV_DIM),
        dtb=_pad_lanes(ssd_dt_bias[l], LANES, SMALL_DT_OFF), alog=_pad_lanes(ssd_a_log[l], LANES, SMALL_DT_OFF),
        dvec=jnp.repeat(ssd_d[l].astype(F32), SSD_HEAD_DIM).reshape(1, SSD_INNER),
        ssd_nw=ssd_norm_w[l].reshape(1, SSD_INNER),
        gla_wg=wg.astype(BF16), gla_bg=gla_b_gate[l].reshape(1, GLA_DK),
        gla_nw=jnp.tile(gla_norm_w[l], GLA_HEADS).reshape(1, GLA_DV),
        norm_w=norm_w[l],
    )


def _trunk(x, mods, layers, stacked, norm_f, states, sample):
    b, l, _ = x.shape
    keep = min(ATT_MAX_WINDOW, l)
    names = [n for n, _ in PROJ_GROUPS]
    outs = ([], [], [], [], [])
    for li, lp in enumerate(layers):
        mod = mods[li]
        x = _ffn(x, mod, 0, lp["norm_w"][0], stacked["ffn1_in"], stacked["ffn1_out"], li)
        if sample:
            proj = dict(zip(names, _inproj(x, mod, lp["norm_w"][1], lp["w_in_p"])))
            st_ssd, st_conv, st_gla, kcache, vcache = states
            bufp = jnp.pad(st_conv[li], ((0, 0), (0, l - (SSD_CONV - 1)), (0, 0))).reshape(b * l, SSD_CONV_DIM)
            y, ssd_new = _ssd_sample(proj["z"], proj["xbc"], proj["small"], bufp, st_ssd, li, lp, b, l)
            o, gla_new = _gla_sample(proj["gq"], proj["gk"], proj["gv"], proj["gr"], proj["small"],
                                     _gla_state_expand(st_gla[li]), lp, b, l)
            att = _att_sample(proj["aq"], proj["ak"], proj["av"], kcache, vcache, li, b, l)
            for acc, name in ((outs[3], "ak"), (outs[4], "av")):
                kv = proj[name].reshape(ATT_HEADS // 2, b, l, 2, ATT_HEAD_DIM)[:, :, l - keep:]
                acc.append(kv.transpose(1, 2, 0, 3, 4).reshape(b, keep, ATT_HEADS, ATT_HEAD_DIM))
        else:
            res = _inproj(x, mod, lp["norm_w"][1], lp["w_in_p"], keep_t=keep)
            proj = dict(zip(names, res))
            y, ssd_new = _ssd_prompt(proj["z"], proj["xbc"], proj["small"], lp, b, l)
            o, gla_new = _gla_prompt(proj["gq"], proj["gk"], proj["gv"], proj["gr"], proj["small"], lp, b, l)
            att = _att_prompt(proj["aq"], proj["ak"], proj["av"], b, l)
            for acc, kv_t in zip((outs[3], outs[4]), res[len(names):]):
                acc.append(kv_t.reshape(b, ATT_HEADS, ATT_HEAD_DIM, keep).transpose(0, 3, 1, 2))
        x = _ffn(x, mod, 6, lp["norm_w"][2], stacked["ffn2_in"], stacked["ffn2_out"], li,
                 premix=(y, o, att, 5, stacked["w_out"]),
                 final_norm=norm_f if li == len(layers) - 1 else None)
        outs[0].append(ssd_new)
        outs[1].append(proj["xbc"].reshape(b, l, SSD_CONV_DIM)[:, l - (SSD_CONV - 1):])
        outs[2].append(_gla_state_extract(gla_new))
    return x, [jnp.stack(a) for a in outs]


def kernel(x_prompt, x_sample, c_prompt, c_sample, state_ssd, state_ssd_conv, state_gla, cache_attn_k, cache_attn_v,
           w_in, w_out, ssd_conv_w, ssd_conv_b, ssd_dt_bias, ssd_a_log, ssd_d, ssd_norm_w,
           gla_w_gate, gla_b_gate, gla_norm_w, norm_w, w_mod, b_mod,
           ffn1_w_in, ffn1_w_out, ffn2_w_in, ffn2_w_out, norm_f):
    bp, bs = x_prompt.shape[0], x_sample.shape[0]
    depth = w_in.shape[0]
    layers = [_layer_params(l, w_in, ssd_conv_w, ssd_conv_b, ssd_dt_bias, ssd_a_log, ssd_d, ssd_norm_w,
                            gla_w_gate, gla_b_gate, gla_norm_w, norm_w) for l in range(depth)]
    stacked = dict(w_out=w_out.astype(BF16), ffn1_in=ffn1_w_in.astype(BF16), ffn1_out=ffn1_w_out.astype(BF16),
                   ffn2_in=ffn2_w_in.astype(BF16), ffn2_out=ffn2_w_out.astype(BF16))
    npad = -(bp + bs) % SUBLANES
    c_all = jnp.concatenate([c_prompt, c_sample, jnp.zeros((npad, D_MODEL), F32)], axis=0)
    m_all = _modulation(c_all, w_mod.astype(BF16), b_mod)
    mods_p, mods_s = [], []
    for l in range(depth):
        m = m_all[l].reshape(-1, ADALN_MODS, 1, D_MODEL).transpose(1, 0, 2, 3)
        mods_p.append(m[:, :bp])
        mods_s.append(m[:, bp:bp + bs])
    kcache = cache_attn_k.transpose(0, 1, 3, 4, 2)
    vcache = cache_attn_v.transpose(0, 1, 3, 4, 2)
    y_p, (ssd_p, conv_p, gla_p, k_p, v_p) = _trunk(x_prompt, mods_p, layers, stacked, norm_f, None, sample=False)
    y_s, (ssd_s, conv_s, gla_s, k_s, v_s) = _trunk(
        x_sample, mods_s, layers, stacked, norm_f, (state_ssd, state_ssd_conv, state_gla, kcache, vcache),
        sample=True)
    return (y_p, y_s, ssd_p, ssd_s, conv_p, conv_s, gla_p, gla_s, k_p, k_s, v_p, v_s)
```

```python
import functools
import math

import numpy as np
import jax
import jax.numpy as jnp
from jax import lax
from jax.experimental import pallas as pl
from jax.experimental.pallas import tpu as pltpu

F32 = jnp.float32
BF16 = jnp.bfloat16

D_MODEL = 1024
DEPTH = 2
SSD_HEADS = 6
SSD_HEAD_DIM = 64
SSD_INNER = SSD_HEADS * SSD_HEAD_DIM
SSD_GROUPS = 2
SSD_STATE = 128
SSD_CONV = 4
SSD_CONV_DIM = SSD_INNER + 2 * SSD_GROUPS * SSD_STATE
GLA_HEADS = 4
GLA_HEAD_K = 32
GLA_HEAD_V = 64
GLA_DK = GLA_HEADS * GLA_HEAD_K
GLA_DV = GLA_HEADS * GLA_HEAD_V
GLA_GATE_RANK = 16
GLA_TAU = 16.0
ATT_HEADS = 6
ATT_HEAD_DIM = 64
ATT_DIM = ATT_HEADS * ATT_HEAD_DIM
DILATION_PATTERNS = ((128, 1), (512, 4), (2048, 16))
ATT_MAX_WINDOW = 2048
ATT_KEYS = 129
D_MIX = SSD_INNER + GLA_DV + ATT_DIM
IN_SPLITS = (SSD_INNER, SSD_CONV_DIM, SSD_HEADS, GLA_DK, GLA_DK, GLA_DV, GLA_DV, GLA_GATE_RANK,
             ATT_DIM, ATT_DIM, ATT_DIM)
D_FF = 2816
ADALN_MODS = 9
FFN_RES = 0.5
EPS = 1e-6

LANES = 128
SUBLANES = 8
VMEM_LIMIT = 56 * 1024 * 1024

PROJ_GROUPS = (("z", SSD_INNER), ("xbc", SSD_CONV_DIM), ("gq", GLA_DK), ("gk", GLA_DK), ("gv", GLA_DV),
               ("gr", GLA_DV), ("aq", ATT_DIM), ("ak", ATT_DIM), ("av", ATT_DIM), ("small", LANES))
PROJ_WIDTH = sum(w for _, w in PROJ_GROUPS)
PAIR_MAJOR = ("aq", "ak", "av")
KV_T = ("ak", "av")
SMALL_DT_OFF = 0
SMALL_GLR_OFF = 8

ROW_TILE = 512
FF_CHUNK = 256
SSD_CHUNK = 128
GLA_CHUNK = 128
GLA_STEP = 256
GLA_DIAG = 16
SAMPLE_SEQS = 16
ATT_BLOCK = 128
NEG = -1e30


def _dot(a, b):
    return jnp.dot(a, b, preferred_element_type=F32)


def _dot_nt(a, b):
    return lax.dot_general(a, b, (((1,), (1,)), ((), ())), preferred_element_type=F32)


def _sigmoid(x):
    return 1.0 / (1.0 + jnp.exp(-x))


def _silu(x):
    return x * _sigmoid(x)


def _softplus(x):
    return jnp.maximum(x, 0.0) + jnp.log1p(jnp.exp(-jnp.abs(x)))


def _split3_dot(m01, a):
    a1 = a.astype(BF16)
    r1 = a - a1.astype(F32)
    a2 = r1.astype(BF16)
    a3 = (r1 - a2.astype(F32)).astype(BF16)
    return _dot(m01, a1) + _dot(m01, a2) + _dot(m01, a3)


def _split2_dot(a, m01):
    a1 = a.astype(BF16)
    a2 = (a - a1.astype(F32)).astype(BF16)
    return _dot(a1, m01) + _dot(a2, m01)


def _rms_mod(x, nw, shift, scale):
    ms = jnp.mean(x * x, axis=-1, keepdims=True)
    y = x * lax.rsqrt(ms + EPS) * nw
    return y * (1.0 + scale) + shift


def _iota2(shape, axis):
    return lax.broadcasted_iota(jnp.int32, shape, axis)


def _params(sem):
    return pltpu.CompilerParams(dimension_semantics=sem, vmem_limit_bytes=VMEM_LIMIT)


def _const_spec(shape, layer=None):
    nd = len(shape)
    if layer is None:
        return pl.BlockSpec(shape, lambda *_: (0,) * nd, pipeline_mode=pl.Buffered(1))
    return pl.BlockSpec((None,) + tuple(shape), lambda *_: (layer,) + (0,) * nd, pipeline_mode=pl.Buffered(1))


def _mod_kernel(c_ref, w_ref, b_ref, o_ref):
    c = c_ref[...]
    o_ref[...] = _dot(_silu(c).astype(BF16), w_ref[...]) + b_ref[...]


def _modulation(c_all, w_mod, b_mod):
    n, d = c_all.shape
    depth, _, nout = w_mod.shape
    tn = D_MODEL
    return pl.pallas_call(
        _mod_kernel,
        grid=(depth, nout // tn),
        in_specs=[pl.BlockSpec((n, d), lambda l, j: (0, 0)),
                  pl.BlockSpec((None, d, tn), lambda l, j: (l, 0, j)),
                  pl.BlockSpec((None, 1, tn), lambda l, j: (l, 0, j))],
        out_specs=pl.BlockSpec((None, n, tn), lambda l, j: (l, 0, j)),
        out_shape=jax.ShapeDtypeStruct((depth, n, nout), F32),
        compiler_params=_params(("arbitrary", "arbitrary")),
        name="adaln_mod",
    )(c_all, w_mod, b_mod.reshape(depth, 1, nout))


def _row_tiling(b, l):
    if l >= ROW_TILE:
        assert l % ROW_TILE == 0
        return 1, ROW_TILE
    assert ROW_TILE % l == 0 and b % (ROW_TILE // l) == 0
    return ROW_TILE // l, l


def _x_spec(bb, ll, nlb):
    return pl.BlockSpec((bb, ll, D_MODEL), lambda i: (i // nlb, i % nlb, 0))


def _mod_spec(k, bb, nlb):
    return pl.BlockSpec((1, bb, 1, D_MODEL), lambda i: (k, i // nlb, 0, 0))


def _rows_spec(r, width):
    return pl.BlockSpec((r, width), lambda i: (i, 0))


def _pair_spec(r, width):
    return pl.BlockSpec((width // LANES, r, LANES), lambda i: (0, i, 0))


def _ffn_kernel(*refs, premix, final):
    refs = list(refs)
    x_ref = refs.pop(0)
    if premix:
        y_ref, o_ref, a_ref, g2_ref, wo_ref = refs[:5]
        refs = refs[5:]
    sh_ref, sc_ref, gt_ref, nw_ref, win_ref, wout_ref = refs[:6]
    refs = refs[6:]
    if final:
        nf_ref = refs.pop(0)
    out_ref, act_ref = refs
    bb, ll, d = x_ref.shape
    r = bb * ll
    x = x_ref[...]
    if premix:
        mix = (_dot(y_ref[...], wo_ref[0:SSD_INNER, :])
               + _dot(o_ref[...], wo_ref[SSD_INNER:SSD_INNER + GLA_DV, :])
               + sum(_dot(a_ref[p], wo_ref[SSD_INNER + GLA_DV + p * LANES:SSD_INNER + GLA_DV + (p + 1) * LANES, :])
                     for p in range(ATT_DIM // LANES)))
        x = x + g2_ref[0] * mix.reshape(bb, ll, d)
    h = _rms_mod(x, nw_ref[...], sh_ref[0], sc_ref[0]).reshape(r, d).astype(BF16)
    for c in range(D_FF // FF_CHUNK):
        g = _dot(h, win_ref[:, c * FF_CHUNK:(c + 1) * FF_CHUNK])
        u = _dot(h, win_ref[:, D_FF + c * FF_CHUNK:D_FF + (c + 1) * FF_CHUNK])
        act_ref[:, c * FF_CHUNK:(c + 1) * FF_CHUNK] = (_silu(g) * u).astype(BF16)
    y = _dot(act_ref[...], wout_ref[...])
    x = x + FFN_RES * gt_ref[0] * y.reshape(bb, ll, d)
    if final:
        ms = jnp.mean(x * x, axis=-1, keepdims=True)
        x = x * lax.rsqrt(ms + EPS) * nf_ref[...]
    out_ref[...] = x


def _ffn(x, mod, mod_base, norm_w, w_in, w_out, layer, premix=None, final_norm=None):
    b, l, d = x.shape
    bb, ll = _row_tiling(b, l)
    nlb = l // ll
    r = bb * ll
    nsteps = (b // bb) * nlb
    args, specs = [x], [_x_spec(bb, ll, nlb)]
    if premix is not None:
        y, o, a, gate_row, wo = premix
        args += [y, o, a, mod, wo]
        specs += [_rows_spec(r, SSD_INNER), _rows_spec(r, GLA_DV), _pair_spec(r, ATT_DIM),
                  _mod_spec(gate_row, bb, nlb), _const_spec((D_MIX, d), layer)]
    args += [mod, mod, mod, norm_w.reshape(1, d), w_in, w_out]
    specs += [_mod_spec(mod_base, bb, nlb), _mod_spec(mod_base + 1, bb, nlb), _mod_spec(mod_base + 2, bb, nlb),
              _const_spec((1, d)), _const_spec((d, 2 * D_FF), layer), _const_spec((D_FF, d), layer)]
    if final_norm is not None:
        args.append(final_norm.reshape(1, d))
        specs.append(_const_spec((1, d)))
    return pl.pallas_call(
        functools.partial(_ffn_kernel, premix=premix is not None, final=final_norm is not None),
        grid=(nsteps,),
        in_specs=specs,
        out_specs=_x_spec(bb, ll, nlb),
        out_shape=jax.ShapeDtypeStruct((b, l, d), F32),
        scratch_shapes=[pltpu.VMEM((r, D_FF), BF16)],
        compiler_params=_params(("arbitrary",)),
        name="ffn",
    )(*args)


def _inproj_kernel(x_ref, sh_ref, sc_ref, nw_ref, w_ref, *out_refs, first_kept):
    bb, ll, d = x_ref.shape
    h = _rms_mod(x_ref[...], nw_ref[...], sh_ref[0], sc_ref[0]).reshape(bb * ll, d).astype(BF16)
    results, start, run = {}, 0, []
    for name, width in PROJ_GROUPS:
        run.append((name, width))
        total = sum(w for _, w in run)
        if total % (2 * LANES) == 0:
            big = _dot(h, w_ref[:, start:start + total])
            o = 0
            for n, w in run:
                results[n] = big[:, o:o + w]
                o += w
            start, run = start + total, []
    assert not run
    for ref, (name, width) in zip(out_refs, PROJ_GROUPS):
        res = results[name]
        if name in PAIR_MAJOR:
            for p in range(width // LANES):
                ref[p] = res[:, p * LANES:(p + 1) * LANES]
        else:
            ref[...] = res
        if first_kept is not None and name in KV_T:
            t_ref = out_refs[len(PROJ_GROUPS) + KV_T.index(name)]

            @pl.when(pl.program_id(0) % first_kept[1] >= first_kept[0])
            def _(t_ref=t_ref, res=res):
                t_ref[0] = res.T


def _inproj(x, mod, norm_w, w_in_p, layer, keep_t=None):
    b, l, d = x.shape
    bb, ll = _row_tiling(b, l)
    nlb = l // ll
    r = bb * ll
    out_specs = [_pair_spec(r, w) if n in PAIR_MAJOR else _rows_spec(r, w) for n, w in PROJ_GROUPS]
    out_shape = [jax.ShapeDtypeStruct((w // LANES, b * l, LANES) if n in PAIR_MAJOR else (b * l, w), F32)
                 for n, w in PROJ_GROUPS]
    first_kept = None
    if keep_t is not None:
        assert bb == 1 and keep_t % ll == 0 and keep_t <= l
        skip = nlb - keep_t // ll
        first_kept = (skip, nlb)
        t_spec = pl.BlockSpec((1, ATT_DIM, ll), lambda i: (i // nlb, 0, jnp.maximum(i % nlb - skip, 0)))
        out_specs += [t_spec] * len(KV_T)
        out_shape += [jax.ShapeDtypeStruct((b, ATT_DIM, keep_t), F32)] * len(KV_T)
    return pl.pallas_call(
        functools.partial(_inproj_kernel, first_kept=first_kept),
        grid=((b // bb) * nlb,),
        in_specs=[_x_spec(bb, ll, nlb), _mod_spec(3, bb, nlb), _mod_spec(4, bb, nlb),
                  _const_spec((1, d)), _const_spec((d, PROJ_WIDTH), layer)],
        out_specs=out_specs,
        out_shape=out_shape,
        compiler_params=_params(("arbitrary",)),
        name="inproj",
    )(x, mod, mod, norm_w.reshape(1, d), w_in_p)


def _ssd_prepare(conv, small, dtb, alog, seg01, seglast01):
    xc = _silu(conv)
    xs = xc[:, 0:SSD_INNER]
    bm = xc[:, SSD_INNER:SSD_INNER + SSD_GROUPS * SSD_STATE]
    cm = xc[:, SSD_INNER + SSD_GROUPS * SSD_STATE:SSD_CONV_DIM]
    dt = _softplus(small + dtb)
    a = dt * (-jnp.exp(alog))
    cum = _split3_dot(seg01, a)
    if seglast01 is None:
        cum_last = jnp.broadcast_to(cum[cum.shape[0] - 1:, :], cum.shape)
    else:
        cum_last = _split3_dot(seglast01, cum)
    return xs, bm, cm, dt, cum, cum_last


def _ssd_diag(xs, bm, cm, dt, cum, cum_t, mask):
    ydiag, xdt = [], []
    cb = [_dot_nt(cm[:, g * SSD_STATE:(g + 1) * SSD_STATE].astype(BF16),
                  bm[:, g * SSD_STATE:(g + 1) * SSD_STATE].astype(BF16)) for g in range(SSD_GROUPS)]
    for h in range(SSD_HEADS):
        g = h // (SSD_HEADS // SSD_GROUPS)
        diff = cum[:, h:h + 1] - cum_t[h:h + 1, :]
        decay = jnp.exp(jnp.where(mask, diff, NEG))
        xh = xs[:, h * SSD_HEAD_DIM:(h + 1) * SSD_HEAD_DIM] * dt[:, h:h + 1]
        xdt.append(xh)
        ydiag.append(_dot((cb[g] * decay).astype(BF16), xh.astype(BF16)))
    return ydiag, xdt


def _ssd_finish(y, xs, z, dvec, normw):
    y = (y + dvec * xs) * _silu(z)
    sq = y * y
    half = SSD_INNER // SSD_GROUPS
    lane = _iota2(y.shape, 1)
    s0 = jnp.sum(jnp.where(lane < half, sq, 0.0), axis=-1, keepdims=True)
    s1 = jnp.sum(jnp.where(lane >= half, sq, 0.0), axis=-1, keepdims=True)
    ms = jnp.where(lane < half, s0, s1) * (1.0 / half)
    return y * lax.rsqrt(ms + EPS) * normw


def _ssd_prompt_kernel(z_ref, xbc_ref, small_ref, cw_ref, cb_ref, dtb_ref, alog_ref, dvec_ref, nw_ref,
                       y_ref, st_ref, h_scr, tail_scr):
    c = pl.program_id(1)
    rows = xbc_ref.shape[0]

    @pl.when(c == 0)
    def _():
        h_scr[...] = jnp.zeros_like(h_scr)
        tail_scr[0:SUBLANES, :] = jnp.zeros((SUBLANES, SSD_CONV_DIM), F32)

    u = xbc_ref[...]
    tail_scr[SUBLANES:SUBLANES + rows, :] = u
    conv = u * cw_ref[SSD_CONV - 1:SSD_CONV, :] + cb_ref[...]
    for j in range(1, SSD_CONV):
        conv = conv + tail_scr[SUBLANES - j:SUBLANES - j + rows, :] * cw_ref[SSD_CONV - 1 - j:SSD_CONV - j, :]
    tail_scr[0:SUBLANES, :] = u[rows - SUBLANES:, :]

    qi = _iota2((rows, rows), 0)
    si = _iota2((rows, rows), 1)
    mask = si <= qi
    seg01 = jnp.where(mask, 1.0, 0.0).astype(BF16)
    xs, bm, cm, dt, cum, cum_last = _ssd_prepare(conv, small_ref[...], dtb_ref[...], alog_ref[...], seg01, None)
    cum_t = cum.T
    ydiag, xdt = _ssd_diag(xs, bm, cm, dt, cum, cum_t, mask)
    to_end_t = jnp.exp(cum_last - cum).T
    xdt_t = jnp.concatenate(xdt, axis=1).T
    ys = []
    for h in range(SSD_HEADS):
        g = h // (SSD_HEADS // SSD_GROUPS)
        hprev = h_scr[h]
        cg = cm[:, g * SSD_STATE:(g + 1) * SSD_STATE].astype(BF16)
        yoff = _dot_nt(cg, hprev.astype(BF16)) * jnp.exp(cum[:, h:h + 1])
        ys.append(ydiag[h] + yoff)
        xw_t = xdt_t[h * SSD_HEAD_DIM:(h + 1) * SSD_HEAD_DIM, :] * to_end_t[h:h + 1, :]
        s_local = _dot(xw_t.astype(BF16), bm[:, g * SSD_STATE:(g + 1) * SSD_STATE].astype(BF16))
        h_scr[h] = jnp.exp(cum_last[0:1, h:h + 1]) * hprev + s_local
    y = _ssd_finish(jnp.concatenate(ys, axis=1), xs, z_ref[...], dvec_ref[...], nw_ref[...])
    y_ref[...] = y.astype(y_ref.dtype)

    @pl.when(c == pl.num_programs(1) - 1)
    def _():
        st_ref[0] = h_scr[...]


def _ssd_vec_specs():
    return [_const_spec((SSD_CONV, SSD_CONV_DIM)), _const_spec((1, SSD_CONV_DIM)), _const_spec((1, LANES)),
            _const_spec((1, LANES)), _const_spec((1, SSD_INNER)), _const_spec((1, SSD_INNER))]


def _ssd_prompt(z, xbc, small, lp, b, l):
    nc = l // SSD_CHUNK
    rs = lambda w: pl.BlockSpec((SSD_CHUNK, w), lambda bi, ci: (bi * nc + ci, 0))
    return pl.pallas_call(
        _ssd_prompt_kernel,
        grid=(b, nc),
        in_specs=[rs(SSD_INNER), rs(SSD_CONV_DIM), rs(LANES)] + _ssd_vec_specs(),
        out_specs=[rs(SSD_INNER),
                   pl.BlockSpec((1, SSD_HEADS, SSD_HEAD_DIM, SSD_STATE), lambda bi, ci: (bi, 0, 0, 0))],
        out_shape=[jax.ShapeDtypeStruct((b * l, SSD_INNER), BF16),
                   jax.ShapeDtypeStruct((b, SSD_HEADS, SSD_HEAD_DIM, SSD_STATE), F32)],
        scratch_shapes=[pltpu.VMEM((SSD_HEADS, SSD_HEAD_DIM, SSD_STATE), F32),
                        pltpu.VMEM((SUBLANES + SSD_CHUNK, SSD_CONV_DIM), F32)],
        compiler_params=_params(("arbitrary", "arbitrary")),
        name="ssd_prompt",
    )(z, xbc, small, lp["conv_w"], lp["conv_b"], lp["dtb"], lp["alog"], lp["dvec"], lp["ssd_nw"])


def _ssd_sample_kernel(z_ref, xbc_ref, small_ref, buf_ref, h0_ref, cw_ref, cb_ref, dtb_ref, alog_ref, dvec_ref,
                       nw_ref, y_ref, st_ref, *, seq):
    rows = xbc_ref.shape[0]
    nseq = rows // seq
    u = xbc_ref[...]
    bufp = buf_ref[...]
    tpos = _iota2((rows, SSD_CONV_DIM), 0) % seq
    conv = u * cw_ref[SSD_CONV - 1:SSD_CONV, :] + cb_ref[...]
    for j in range(1, SSD_CONV):
        uj = pltpu.roll(u, j, axis=0)
        back = (rows - (SSD_CONV - 1 - j)) % rows
        bj = pltpu.roll(bufp, back, axis=0) if back else bufp
        conv = conv + jnp.where(tpos < j, bj, uj) * cw_ref[SSD_CONV - 1 - j:SSD_CONV - j, :]

    qi = _iota2((rows, rows), 0)
    si = _iota2((rows, rows), 1)
    same = (qi // seq) == (si // seq)
    mask = same & (si <= qi)
    seg01 = jnp.where(mask, 1.0, 0.0).astype(BF16)
    last01 = jnp.where(si == (qi // seq) * seq + (seq - 1), 1.0, 0.0).astype(BF16)
    xs, bm, cm, dt, cum, cum_last = _ssd_prepare(conv, small_ref[...], dtb_ref[...], alog_ref[...], seg01, last01)
    cum_t = cum.T
    cum_last_t = cum_last.T
    ydiag, xdt = _ssd_diag(xs, bm, cm, dt, cum, cum_t, mask)
    to_end_t = jnp.exp(cum_last_t - cum_t)
    xdt_t = jnp.concatenate(xdt, axis=1).T
    hpg = SSD_HEADS // SSD_GROUPS
    grows = hpg * SSD_HEAD_DIM
    colseq = _iota2((grows, rows), 1) // seq
    yoff_t = []
    for g in range(SSD_GROUPS):
        cg = cm[:, g * SSD_STATE:(g + 1) * SSD_STATE].astype(BF16)
        bg = bm[:, g * SSD_STATE:(g + 1) * SSD_STATE].astype(BF16)
        xw_t = jnp.concatenate(
            [xdt_t[h * SSD_HEAD_DIM:(h + 1) * SSD_HEAD_DIM, :] * to_end_t[h:h + 1, :]
             for h in range(g * hpg, (g + 1) * hpg)], axis=0)
        acc = jnp.zeros((grows, rows), F32)
        for b in range(nseq):
            h0 = h0_ref[b, g * hpg:(g + 1) * hpg].reshape(grows, SSD_STATE)
            acc = jnp.where(colseq == b, _dot_nt(h0.astype(BF16), cg), acc)
            s_local = _dot(jnp.where(colseq == b, xw_t, 0.0).astype(BF16), bg)
            for hh in range(hpg):
                h = g * hpg + hh
                dec = jnp.exp(cum_last_t[h:h + 1, b * seq:b * seq + 1])
                st_ref[b, h] = (dec * h0[hh * SSD_HEAD_DIM:(hh + 1) * SSD_HEAD_DIM, :]
                                + s_local[hh * SSD_HEAD_DIM:(hh + 1) * SSD_HEAD_DIM, :])
        for hh in range(hpg):
            h = g * hpg + hh
            yoff_t.append(acc[hh * SSD_HEAD_DIM:(hh + 1) * SSD_HEAD_DIM, :] * jnp.exp(cum_t[h:h + 1, :]))
    yoff = jnp.concatenate(yoff_t, axis=0).T
    y = _ssd_finish(jnp.concatenate(ydiag, axis=1) + yoff, xs, z_ref[...], dvec_ref[...], nw_ref[...])
    y_ref[...] = y.astype(y_ref.dtype)


def _ssd_sample(z, xbc, small, bufp, h0_all, layer, lp, b, l):
    rows = SAMPLE_SEQS * l
    rs = lambda w: pl.BlockSpec((rows, w), lambda i: (i, 0))
    st = pl.BlockSpec((SAMPLE_SEQS, SSD_HEADS, SSD_HEAD_DIM, SSD_STATE), lambda i: (i, 0, 0, 0))
    st_in = pl.BlockSpec((None, SAMPLE_SEQS, SSD_HEADS, SSD_HEAD_DIM, SSD_STATE), lambda i: (layer, i, 0, 0, 0))
    return pl.pallas_call(
        functools.partial(_ssd_sample_kernel, seq=l),
        grid=(b // SAMPLE_SEQS,),
        in_specs=[rs(SSD_INNER), rs(SSD_CONV_DIM), rs(LANES), rs(SSD_CONV_DIM), st_in] + _ssd_vec_specs(),
        out_specs=[rs(SSD_INNER), st],
        out_shape=[jax.ShapeDtypeStruct((b * l, SSD_INNER), BF16),
                   jax.ShapeDtypeStruct((b, SSD_HEADS, SSD_HEAD_DIM, SSD_STATE), F32)],
        compiler_params=_params(("arbitrary",)),
        name="ssd_sample",
    )(z, xbc, small, bufp, h0_all, lp["conv_w"], lp["conv_b"], lp["dtb"], lp["alog"], lp["dvec"], lp["ssd_nw"])


def _gla_consts():
    rk = np.arange(GLA_DK)[:, None] // GLA_HEAD_K
    cv = np.arange(GLA_DV)[None, :] // GLA_HEAD_V
    expand = (rk == cv).astype(np.float32)
    rv = np.arange(GLA_DV)[:, None] // GLA_HEAD_V
    seg = (rv == cv).astype(np.float32) / GLA_HEAD_V
    return jnp.asarray(expand, BF16), jnp.asarray(seg, BF16)


def _gla_prepare(gq, gk, small, wg, bg, seg01):
    glin = _dot(small.astype(BF16), wg) + bg
    g = -_softplus(-glin) * (1.0 / GLA_TAU)
    gc = _split3_dot(seg01, g)
    q = gq * (GLA_HEAD_K ** -0.5)
    return q, gk, gc


def _gla_pairwise(q, k, v, gc, expand, diag):
    rows = q.shape[0]
    nb = rows // diag
    q4 = q.reshape(nb, 1, diag, GLA_DK)
    g4 = gc.reshape(nb, 1, diag, GLA_DK)
    k4 = k.reshape(nb, diag, 1, GLA_DK)
    gs4 = gc.reshape(nb, diag, 1, GLA_DK)
    shape = (nb, diag, diag, GLA_DK)
    si = lax.broadcasted_iota(jnp.int32, shape, 1)
    ti = lax.broadcasted_iota(jnp.int32, shape, 2)
    w = jnp.exp(jnp.where(si <= ti, g4 - gs4, NEG))
    m = (q4 * k4 * w).reshape(nb * diag * diag, GLA_DK)
    p = _dot(m.astype(BF16), expand).reshape(nb, diag, diag, GLA_DV)
    o = jnp.sum(p * v.reshape(nb, diag, 1, GLA_DV), axis=1)
    return o.reshape(rows, GLA_DV)


def _gla_finish(o, gr, seg, nw):
    ms = _split2_dot(o * o, seg)
    return o * lax.rsqrt(ms + EPS) * nw * _silu(gr)


def _head_stack(x, head_dim, heads):
    lane = _iota2(x.shape, 1) // head_dim
    return jnp.concatenate([jnp.where(lane == h, x, 0.0) for h in range(heads)], axis=0)


def _gla_prompt_kernel(gq_ref, gk_ref, gv_ref, gr_ref, small_ref, wg_ref, bg_ref, nw_ref, ex_ref, seg_ref,
                       o_ref, st_ref, s_scr, o_scr):
    c = pl.program_id(1)
    rows = GLA_CHUNK

    @pl.when(c == 0)
    def _():
        s_scr[...] = jnp.zeros_like(s_scr)

    qi = _iota2((rows, rows), 0)
    si = _iota2((rows, rows), 1)
    seg01 = jnp.where(si <= qi, 1.0, 0.0).astype(BF16)
    bd = (_iota2((GLA_DK, GLA_DV), 0) // GLA_HEAD_K) == (_iota2((GLA_DK, GLA_DV), 1) // GLA_HEAD_V)
    s_all = s_scr[...]
    for ci in range(gq_ref.shape[0] // rows):
        r0 = ci * rows
        q, k, gc = _gla_prepare(gq_ref[r0:r0 + rows, :], gk_ref[r0:r0 + rows, :], small_ref[r0:r0 + rows, :],
                                wg_ref[...], bg_ref[...], seg01)
        v = gv_ref[r0:r0 + rows, :]
        vb = v.astype(BF16)
        o_scr[r0:r0 + rows, :] = (_dot((q * jnp.exp(gc)).astype(BF16), s_all.astype(BF16))
                                  + _gla_pairwise(q, k, v, gc, ex_ref[...], GLA_DIAG))
        half = rows // 2
        while half >= GLA_DIAG:
            vlane = _iota2((half, GLA_DV), 1) // GLA_HEAD_V
            for blk in range(rows // (2 * half)):
                s0 = blk * 2 * half
                t0 = s0 + half
                ref = gc[t0 - 1:t0, :]
                qs = q[t0:t0 + half, :] * jnp.exp(gc[t0:t0 + half, :] - ref)
                ks = k[s0:t0, :] * jnp.exp(ref - gc[s0:t0, :])
                att = _dot_nt(_head_stack(qs, GLA_HEAD_K, GLA_HEADS).astype(BF16), ks.astype(BF16))
                pv = _dot(att.astype(BF16), vb[s0:t0, :])
                ot = jnp.zeros((half, GLA_DV), F32)
                for h in range(GLA_HEADS):
                    ot = jnp.where(vlane == h, pv[h * half:(h + 1) * half, :], ot)
                o_scr[r0 + t0:r0 + t0 + half, :] += ot
            half //= 2
        gc_t = gc.T
        dcol = gc_t[:, rows - 1:rows]
        kd_t = k.T * jnp.exp(dcol - gc_t)
        upd = _dot(kd_t.astype(BF16), vb)
        s_all = jnp.exp(dcol) * s_all + jnp.where(bd, upd, 0.0)
    o_ref[...] = _gla_finish(o_scr[...], gr_ref[...], seg_ref[...], nw_ref[...]).astype(o_ref.dtype)
    s_scr[...] = s_all

    @pl.when(c == pl.num_programs(1) - 1)
    def _():
        st_ref[0] = s_all


def _gla_vec_specs():
    return [_const_spec((LANES, GLA_DK)), _const_spec((1, GLA_DK)), _const_spec((1, GLA_DV)),
            _const_spec((GLA_DK, GLA_DV)), _const_spec((GLA_DV, GLA_DV))]


def _gla_prompt(gq, gk, gv, gr, small, lp, b, l):
    nc = l // GLA_STEP
    rs = lambda w: pl.BlockSpec((GLA_STEP, w), lambda bi, ci: (bi * nc + ci, 0))
    expand, seg = _gla_consts()
    return pl.pallas_call(
        _gla_prompt_kernel,
        grid=(b, nc),
        in_specs=[rs(GLA_DK), rs(GLA_DK), rs(GLA_DV), rs(GLA_DV), rs(LANES)] + _gla_vec_specs(),
        out_specs=[rs(GLA_DV), pl.BlockSpec((1, GLA_DK, GLA_DV), lambda bi, ci: (bi, 0, 0))],
        out_shape=[jax.ShapeDtypeStruct((b * l, GLA_DV), BF16),
                   jax.ShapeDtypeStruct((b, GLA_DK, GLA_DV), F32)],
        scratch_shapes=[pltpu.VMEM((GLA_DK, GLA_DV), F32), pltpu.VMEM((GLA_STEP, GLA_DV), F32)],
        compiler_params=_params(("arbitrary", "arbitrary")),
        name="gla_prompt",
    )(gq, gk, gv, gr, small, lp["gla_wg"], lp["gla_bg"], lp["gla_nw"], expand, seg)


def _gla_sample_kernel(gq_ref, gk_ref, gv_ref, gr_ref, small_ref, s0_ref, wg_ref, bg_ref, nw_ref, ex_ref, seg_ref,
                       o_ref, st_ref, *, seq):
    rows = gq_ref.shape[0]
    nseq = rows // seq
    qi = _iota2((rows, rows), 0)
    si = _iota2((rows, rows), 1)
    seg01 = jnp.where(((qi // seq) == (si // seq)) & (si <= qi), 1.0, 0.0).astype(BF16)
    last01 = jnp.where(si == (qi // seq) * seq + (seq - 1), 1.0, 0.0).astype(BF16)
    q, k, gc = _gla_prepare(gq_ref[...], gk_ref[...], small_ref[...], wg_ref[...], bg_ref[...], seg01)
    v = gv_ref[...]
    vb = v.astype(BF16)
    gc_last = _split3_dot(last01, gc)
    qg = (q * jnp.exp(gc)).astype(BF16)
    kd_t = (k * jnp.exp(gc_last - gc)).T
    dec_t = jnp.exp(gc_last).T
    colseq = _iota2((GLA_DK, rows), 1) // seq
    rowseq = _iota2((rows, GLA_DV), 0) // seq
    bd = (_iota2((GLA_DK, GLA_DV), 0) // GLA_HEAD_K) == (_iota2((GLA_DK, GLA_DV), 1) // GLA_HEAD_V)
    o = _gla_pairwise(q, k, v, gc, ex_ref[...], seq)
    for b in range(nseq):
        s0 = s0_ref[b]
        o = o + jnp.where(rowseq == b, _dot(qg, s0.astype(BF16)), 0.0)
        upd = _dot(jnp.where(colseq == b, kd_t, 0.0).astype(BF16), vb)
        st_ref[b] = dec_t[:, b * seq:b * seq + 1] * s0 + jnp.where(bd, upd, 0.0)
    o_ref[...] = _gla_finish(o, gr_ref[...], seg_ref[...], nw_ref[...]).astype(o_ref.dtype)


def _gla_sample(gq, gk, gv, gr, small, s0, lp, b, l):
    rows = SAMPLE_SEQS * l
    rs = lambda w: pl.BlockSpec((rows, w), lambda i: (i, 0))
    st = pl.BlockSpec((SAMPLE_SEQS, GLA_DK, GLA_DV), lambda i: (i, 0, 0))
    expand, seg = _gla_consts()
    return pl.pallas_call(
        functools.partial(_gla_sample_kernel, seq=l),
        grid=(b // SAMPLE_SEQS,),
        in_specs=[rs(GLA_DK), rs(GLA_DK), rs(GLA_DV), rs(GLA_DV), rs(LANES), st] + _gla_vec_specs(),
        out_specs=[rs(GLA_DV), st],
        out_shape=[jax.ShapeDtypeStruct((b * l, GLA_DV), BF16),
                   jax.ShapeDtypeStruct((b, GLA_DK, GLA_DV), F32)],
        compiler_params=_params(("arbitrary",)),
        name="gla_sample",
    )(gq, gk, gv, gr, small, s0, lp["gla_wg"], lp["gla_bg"], lp["gla_nw"], expand, seg)


def _gla_state_expand(s):
    b = s.shape[0]
    eye = jnp.eye(GLA_HEADS, dtype=s.dtype)
    return (s[:, :, :, None, :] * eye[None, :, None, :, None]).reshape(b, GLA_DK, GLA_DV)


def _gla_state_extract(s):
    b = s.shape[0]
    s5 = s.reshape(b, GLA_HEADS, GLA_HEAD_K, GLA_HEADS, GLA_HEAD_V)
    return jnp.stack([s5[:, h, :, h, :] for h in range(GLA_HEADS)], axis=1)


def _att_window(qs, ks, vs, valid, prev):
    lane_half = _iota2(qs[0].shape, 1) // ATT_HEAD_DIM
    heads = [(p, half) for p in range(len(qs)) for half in range(2)]
    scores = [_dot_nt(jnp.where(lane_half == half, qs[p], 0.0).astype(BF16), ks[p]) for p, half in heads]
    probs, stats = [], []
    for (p, half), s in zip(heads, scores):
        m_prev, l_prev, _ = prev[p]
        s = jnp.where(valid, s, NEG)
        smax = jnp.max(s, axis=-1, keepdims=True)
        mn = jnp.broadcast_to(smax, qs[p].shape) if m_prev is None else jnp.maximum(m_prev[half], smax)
        pr = jnp.exp2(s - jnp.concatenate([mn] * (s.shape[1] // LANES), axis=1))
        psum = jnp.sum(pr, axis=-1, keepdims=True)
        if m_prev is None:
            alpha, ln = None, jnp.broadcast_to(psum, qs[p].shape)
        else:
            alpha = jnp.exp2(m_prev[half] - mn)
            ln = l_prev[half] * alpha + psum
        probs.append(pr.astype(BF16))
        stats.append((mn, ln, alpha))
    pvs = [_dot(pr, vs[p]) for (p, half), pr in zip(heads, probs)]
    new = []
    for p in range(len(qs)):
        a_prev = prev[p][2]
        (m0, l0, al0), (m1, l1, al1) = stats[2 * p], stats[2 * p + 1]
        a0 = pvs[2 * p] if al0 is None else a_prev * al0 + pvs[2 * p]
        a1 = pvs[2 * p + 1] if al1 is None else a_prev * al1 + pvs[2 * p + 1]
        new.append(([m0, m1], [l0, l1], jnp.where(lane_half == 0, a0, a1)))
    return new


def _att_prompt_kernel(q_ref, k_ref, v_ref, out_ref, acc_scr, m_scr, l_scr):
    tq = ATT_BLOCK
    sup = out_ref.shape[1]
    base = pl.program_id(1) * sup
    rel = tq + _iota2((tq, 2 * tq), 0) - _iota2((tq, 2 * tq), 1)
    band = (rel >= 0) & (rel <= ATT_KEYS - 1)
    in_cur = _iota2((tq, 2 * tq), 1) >= tq
    strides = sorted((d for _, d in DILATION_PATTERNS), reverse=True)
    for idx, d in enumerate(strides):
        first, last = idx == 0, idx == len(strides) - 1

        def rows(start, d=d):
            return pl.ds(start, tq, stride=d) if d > 1 else pl.ds(start, tq)

        def body(sb, carry, d=d, first=first, last=last, rows=rows):
            if d > 1:
                r = sb % d
                mi = sb // d
                loc = r + d * tq * mi
                start_q = base + loc
            else:
                mi = sb
                loc = pl.multiple_of(sb * tq, tq)
                start_q = pl.multiple_of(base + loc, tq)
            mglob = base // (d * tq) + mi
            start_p = jnp.where(mglob == 0, start_q, start_q - d * tq)
            valid = band & ((mglob > 0) | in_cur)
            npair = ATT_HEADS // 2
            prev = []
            for p in range(npair):
                if first:
                    prev.append((None, None, None))
                else:
                    prev.append(([m_scr[2 * p + hf, rows(loc), :] for hf in range(2)],
                                 [l_scr[2 * p + hf, rows(loc), :] for hf in range(2)],
                                 acc_scr[p, rows(loc), :]))
            qs = [(q_ref[p, rows(start_q), :] * (ATT_HEAD_DIM ** -0.5 * math.log2(math.e))).astype(BF16)
                  for p in range(npair)]
            ks = [jnp.concatenate([k_ref[p, rows(start_p), :], k_ref[p, rows(start_q), :]], axis=0).astype(BF16)
                  for p in range(npair)]
            vs = [jnp.concatenate([v_ref[p, rows(start_p), :], v_ref[p, rows(start_q), :]], axis=0).astype(BF16)
                  for p in range(npair)]
            new = _att_window(qs, ks, vs, valid, prev)
            lane_half = _iota2((tq, LANES), 1) // ATT_HEAD_DIM
            for p in range(npair):
                m_new, l_new, a_new = new[p]
                if last:
                    den = jnp.where(lane_half == 0, l_new[0], l_new[1])
                    out_ref[p, rows(loc), :] = (a_new / den).astype(out_ref.dtype)
                else:
                    for hf in range(2):
                        m_scr[2 * p + hf, rows(loc), :] = m_new[hf]
                        l_scr[2 * p + hf, rows(loc), :] = l_new[hf]
                    acc_scr[p, rows(loc), :] = a_new
            return carry

        lax.fori_loop(0, sup // tq, body, 0)


def _att_prompt(aq, ak, av, b, l):
    npair = ATT_HEADS // 2
    sup = ATT_BLOCK * max(d for _, d in DILATION_PATTERNS)
    assert l % sup == 0
    seq = pl.BlockSpec((npair, l, LANES), lambda bi, j: (0, bi, 0), pipeline_mode=pl.Buffered(1))
    return pl.pallas_call(
        _att_prompt_kernel,
        grid=(b, l // sup),
        in_specs=[seq, seq, seq],
        out_specs=pl.BlockSpec((npair, sup, LANES), lambda bi, j: (0, bi * (l // sup) + j, 0)),
        out_shape=jax.ShapeDtypeStruct((npair, b * l, LANES), BF16),
        scratch_shapes=[pltpu.VMEM((npair, sup, LANES), F32), pltpu.VMEM((ATT_HEADS, sup, LANES), F32),
                        pltpu.VMEM((ATT_HEADS, sup, LANES), F32)],
        compiler_params=_params(("arbitrary", "arbitrary")),
        name="att_prompt",
    )(aq, ak, av)


def _att_counts(seq, nbuf):
    qpos = nbuf + np.arange(seq)[:, None]
    kpos = np.arange(nbuf + seq)[None, :]
    delta = qpos - kpos
    cnt = np.zeros(delta.shape, np.float32)
    for window, stride in DILATION_PATTERNS:
        cnt += ((delta >= 0) & (delta % stride == 0) & (delta <= window)).astype(np.float32)
    cnt = np.tile(cnt, (2, 1))
    new = np.zeros((2 * seq, LANES), np.float32)
    new[:, :seq] = cnt[:, nbuf:]
    return jnp.asarray(cnt[:, :nbuf]), jnp.asarray(new)


def _att_sample_kernel(q_ref, kn_ref, vn_ref, kc_ref, vc_ref, cc_ref, cn_ref, out_ref):
    seq = q_ref.shape[1]
    cc = cc_ref[...]
    cn = cn_ref[...]
    pad = jnp.zeros((LANES - seq, LANES), F32)
    lane_half = _iota2((seq, LANES), 1) // ATT_HEAD_DIM
    for p in range(ATT_HEADS // 2):
        q = q_ref[p] * (ATT_HEAD_DIM ** -0.5)
        q2 = jnp.concatenate([jnp.where(lane_half == 0, q, 0.0), jnp.where(lane_half == 1, q, 0.0)],
                             axis=0).astype(BF16)
        kt = kc_ref[0, 2 * p:2 * p + 2].reshape(LANES, -1).astype(BF16)
        vt = vc_ref[0, 2 * p:2 * p + 2].reshape(LANES, -1).astype(BF16)
        kn = jnp.concatenate([kn_ref[p], pad], axis=0).astype(BF16)
        vn = jnp.concatenate([vn_ref[p], pad], axis=0).astype(BF16)
        sc = jnp.where(cc > 0, _dot(q2, kt), NEG)
        sn = jnp.where(cn > 0, _dot_nt(q2, kn), NEG)
        m = jnp.maximum(jnp.max(sc, axis=-1, keepdims=True), jnp.max(sn, axis=-1, keepdims=True))
        pc = cc * jnp.exp(sc - m)
        pn = cn * jnp.exp(sn - m)
        den = jnp.sum(pc, axis=-1, keepdims=True) + jnp.sum(pn, axis=-1, keepdims=True)
        o = (_dot_nt(pc.astype(BF16), vt) + _dot(pn.astype(BF16), vn)) / den
        out_ref[p] = jnp.where(lane_half == 0, o[0:seq, :], o[seq:2 * seq, :]).astype(out_ref.dtype)


def _att_sample(aq, ak, av, kcache_t, vcache_t, layer, b, l):
    npair = ATT_HEADS // 2
    nbuf = kcache_t.shape[-1]
    cc, cn = _att_counts(l, nbuf)
    new = pl.BlockSpec((npair, l, LANES), lambda i: (0, i, 0))
    cache = pl.BlockSpec((None, 1, ATT_HEADS, ATT_HEAD_DIM, nbuf), lambda i: (layer, i, 0, 0, 0))
    return pl.pallas_call(
        _att_sample_kernel,
        grid=(b,),
        in_specs=[new, new, new, cache, cache, _const_spec((2 * l, nbuf)), _const_spec((2 * l, LANES))],
        out_specs=new,
        out_shape=jax.ShapeDtypeStruct((npair, b * l, LANES), BF16),
        compiler_params=_params(("arbitrary",)),
        name="att_sample",
    )(aq, ak, av, kcache_t, vcache_t, cc, cn)


def _pad_lanes(v, width, offset=0):
    out = jnp.zeros((1, width), F32)
    return out.at[0, offset:offset + v.shape[0]].set(v.astype(F32))


def _wt_kernel(w_ref, o_ref):
    o_ref[...] = w_ref[...].T.astype(o_ref.dtype)


def _proj_weights(w_in):
    depth = w_in.shape[0]
    wt = w_in.transpose(0, 2, 1)
    offs = np.concatenate([[0], np.cumsum(IN_SPLITS)])
    rows = {n: wt[:, offs[i]:offs[i + 1], :] for i, n in enumerate(
        ("z", "xbc", "dt", "gq", "gk", "gv", "gr", "glr", "aq", "ak", "av"))}
    zeros = lambda n: jnp.zeros((depth, n, D_MODEL), F32)
    rows["small"] = jnp.concatenate(
        [zeros(SMALL_DT_OFF), rows["dt"], zeros(SMALL_GLR_OFF - SMALL_DT_OFF - SSD_HEADS), rows["glr"],
         zeros(LANES - SMALL_GLR_OFF - GLA_GATE_RANK)], axis=1)
    wt_p = jnp.concatenate([rows[n] for n, _ in PROJ_GROUPS], axis=1)
    tn = 2 * LANES
    return pl.pallas_call(
        _wt_kernel,
        grid=(depth, PROJ_WIDTH // tn),
        in_specs=[pl.BlockSpec((None, tn, D_MODEL), lambda l, j: (l, j, 0))],
        out_specs=pl.BlockSpec((None, D_MODEL, tn), lambda l, j: (l, 0, j)),
        out_shape=jax.ShapeDtypeStruct((depth, D_MODEL, PROJ_WIDTH), BF16),
        compiler_params=_params(("arbitrary", "arbitrary")),
        name="proj_weights",
    )(wt_p)


def _layer_params(l, ssd_conv_w, ssd_conv_b, ssd_dt_bias, ssd_a_log, ssd_d, ssd_norm_w,
                  gla_w_gate, gla_b_gate, gla_norm_w, norm_w):
    wg = jnp.zeros((LANES, GLA_DK), F32).at[SMALL_GLR_OFF:SMALL_GLR_OFF + GLA_GATE_RANK, :].set(gla_w_gate[l])
    return dict(
        conv_w=ssd_conv_w[l], conv_b=ssd_conv_b[l].reshape(1, SSD_CONV_DIM),
        dtb=_pad_lanes(ssd_dt_bias[l], LANES, SMALL_DT_OFF), alog=_pad_lanes(ssd_a_log[l], LANES, SMALL_DT_OFF),
        dvec=jnp.repeat(ssd_d[l].astype(F32), SSD_HEAD_DIM).reshape(1, SSD_INNER),
        ssd_nw=ssd_norm_w[l].reshape(1, SSD_INNER),
        gla_wg=wg.astype(BF16), gla_bg=gla_b_gate[l].reshape(1, GLA_DK),
        gla_nw=jnp.tile(gla_norm_w[l], GLA_HEADS).reshape(1, GLA_DV),
        norm_w=norm_w[l],
    )


def _trunk(x, mods, layers, stacked, norm_f, states, sample):
    b, l, _ = x.shape
    keep = min(ATT_MAX_WINDOW, l)
    names = [n for n, _ in PROJ_GROUPS]
    outs = ([], [], [], [], [])
    for li, lp in enumerate(layers):
        mod = mods[li]
        x = _ffn(x, mod, 0, lp["norm_w"][0], stacked["ffn1_in"], stacked["ffn1_out"], li)
        if sample:
            proj = dict(zip(names, _inproj(x, mod, lp["norm_w"][1], stacked["w_in_p"], li)))
            st_ssd, st_conv, st_gla, kcache, vcache = states
            bufp = jnp.pad(st_conv[li], ((0, 0), (0, l - (SSD_CONV - 1)), (0, 0))).reshape(b * l, SSD_CONV_DIM)
            y, ssd_new = _ssd_sample(proj["z"], proj["xbc"], proj["small"], bufp, st_ssd, li, lp, b, l)
            o, gla_new = _gla_sample(proj["gq"], proj["gk"], proj["gv"], proj["gr"], proj["small"],
                                     _gla_state_expand(st_gla[li]), lp, b, l)
            att = _att_sample(proj["aq"], proj["ak"], proj["av"], kcache, vcache, li, b, l)
            for acc, name in ((outs[3], "ak"), (outs[4], "av")):
                kv = proj[name].reshape(ATT_HEADS // 2, b, l, 2, ATT_HEAD_DIM)[:, :, l - keep:]
                acc.append(kv.transpose(1, 2, 0, 3, 4).reshape(b, keep, ATT_HEADS, ATT_HEAD_DIM))
        else:
            res = _inproj(x, mod, lp["norm_w"][1], stacked["w_in_p"], li, keep_t=keep)
            proj = dict(zip(names, res))
            y, ssd_new = _ssd_prompt(proj["z"], proj["xbc"], proj["small"], lp, b, l)
            o, gla_new = _gla_prompt(proj["gq"], proj["gk"], proj["gv"], proj["gr"], proj["small"], lp, b, l)
            att = _att_prompt(proj["aq"], proj["ak"], proj["av"], b, l)
            for acc, kv_t in zip((outs[3], outs[4]), res[len(names):]):
                acc.append(kv_t.reshape(b, ATT_HEADS, ATT_HEAD_DIM, keep).transpose(0, 3, 1, 2))
        x = _ffn(x, mod, 6, lp["norm_w"][2], stacked["ffn2_in"], stacked["ffn2_out"], li,
                 premix=(y, o, att, 5, stacked["w_out"]),
                 final_norm=norm_f if li == len(layers) - 1 else None)
        outs[0].append(ssd_new)
        outs[1].append(proj["xbc"].reshape(b, l, SSD_CONV_DIM)[:, l - (SSD_CONV - 1):])
        outs[2].append(_gla_state_extract(gla_new))
    return x, [jnp.stack(a) for a in outs]


def kernel(x_prompt, x_sample, c_prompt, c_sample, state_ssd, state_ssd_conv, state_gla, cache_attn_k, cache_attn_v,
           w_in, w_out, ssd_conv_w, ssd_conv_b, ssd_dt_bias, ssd_a_log, ssd_d, ssd_norm_w,
           gla_w_gate, gla_b_gate, gla_norm_w, norm_w, w_mod, b_mod,
           ffn1_w_in, ffn1_w_out, ffn2_w_in, ffn2_w_out, norm_f):
    bp, bs = x_prompt.shape[0], x_sample.shape[0]
    depth = w_in.shape[0]
    layers = [_layer_params(l, ssd_conv_w, ssd_conv_b, ssd_dt_bias, ssd_a_log, ssd_d, ssd_norm_w,
                            gla_w_gate, gla_b_gate, gla_norm_w, norm_w) for l in range(depth)]
    stacked = dict(w_in_p=_proj_weights(w_in), w_out=w_out.astype(BF16), ffn1_in=ffn1_w_in.astype(BF16), ffn1_out=ffn1_w_out.astype(BF16),
                   ffn2_in=ffn2_w_in.astype(BF16), ffn2_out=ffn2_w_out.astype(BF16))
    npad = -(bp + bs) % SUBLANES
    c_all = jnp.concatenate([c_prompt, c_sample, jnp.zeros((npad, D_MODEL), F32)], axis=0)
    m_all = _modulation(c_all, w_mod.astype(BF16), b_mod)
    mods_p, mods_s = [], []
    for l in range(depth):
        m = m_all[l].reshape(-1, ADALN_MODS, 1, D_MODEL).transpose(1, 0, 2, 3)
        mods_p.append(m[:, :bp])
        mods_s.append(m[:, bp:bp + bs])
    kcache = cache_attn_k.transpose(0, 1, 3, 4, 2)
    vcache = cache_attn_v.transpose(0, 1, 3, 4, 2)
    y_p, (ssd_p, conv_p, gla_p, k_p, v_p) = _trunk(x_prompt, mods_p, layers, stacked, norm_f, None, sample=False)
    y_s, (ssd_s, conv_s, gla_s, k_s, v_s) = _trunk(
        x_sample, mods_s, layers, stacked, norm_f, (state_ssd, state_ssd_conv, state_gla, kcache, vcache),
        sample=True)
    return (y_p, y_s, ssd_p, ssd_s, conv_p, conv_s, gla_p, gla_s, k_p, k_s, v_p, v_s)
```

```python
import functools
import math

import numpy as np
import jax
import jax.numpy as jnp
from jax import lax
from jax.experimental import pallas as pl
from jax.experimental.pallas import tpu as pltpu

F32 = jnp.float32
BF16 = jnp.bfloat16

D_MODEL = 1024
DEPTH = 2
SSD_HEADS = 6
SSD_HEAD_DIM = 64
SSD_INNER = SSD_HEADS * SSD_HEAD_DIM
SSD_GROUPS = 2
SSD_STATE = 128
SSD_CONV = 4
SSD_CONV_DIM = SSD_INNER + 2 * SSD_GROUPS * SSD_STATE
GLA_HEADS = 4
GLA_HEAD_K = 32
GLA_HEAD_V = 64
GLA_DK = GLA_HEADS * GLA_HEAD_K
GLA_DV = GLA_HEADS * GLA_HEAD_V
GLA_GATE_RANK = 16
GLA_TAU = 16.0
ATT_HEADS = 6
ATT_HEAD_DIM = 64
ATT_DIM = ATT_HEADS * ATT_HEAD_DIM
DILATION_PATTERNS = ((128, 1), (512, 4), (2048, 16))
ATT_MAX_WINDOW = 2048
ATT_KEYS = 129
D_MIX = SSD_INNER + GLA_DV + ATT_DIM
IN_SPLITS = (SSD_INNER, SSD_CONV_DIM, SSD_HEADS, GLA_DK, GLA_DK, GLA_DV, GLA_DV, GLA_GATE_RANK,
             ATT_DIM, ATT_DIM, ATT_DIM)
D_FF = 2816
ADALN_MODS = 9
FFN_RES = 0.5
EPS = 1e-6

LANES = 128
SUBLANES = 8
VMEM_LIMIT = 56 * 1024 * 1024

PROJ_GROUPS = (("z", SSD_INNER), ("xbc", SSD_CONV_DIM), ("gq", GLA_DK), ("gk", GLA_DK), ("gv", GLA_DV),
               ("gr", GLA_DV), ("aq", ATT_DIM), ("ak", ATT_DIM), ("av", ATT_DIM), ("small", LANES))
PROJ_WIDTH = sum(w for _, w in PROJ_GROUPS)
PAIR_MAJOR = ("aq", "ak", "av")
KV_T = ("ak", "av")
SMALL_DT_OFF = 0
SMALL_GLR_OFF = 8

ROW_TILE = 512
FF_CHUNK = 256
SSD_CHUNK = 128
SSD_STEP = 512
GLA_CHUNK = 128
GLA_STEP = 512
GLA_DIAG = 16
SAMPLE_SEQS = 16
ATT_BLOCK = 128
ATT_SAMPLE_SEQS = 2
NEG = -1e30


def _dot(a, b):
    return jnp.dot(a, b, preferred_element_type=F32)


def _dot_nt(a, b):
    return lax.dot_general(a, b, (((1,), (1,)), ((), ())), preferred_element_type=F32)


def _sigmoid(x):
    return 1.0 / (1.0 + jnp.exp(-x))


def _silu(x):
    return x * _sigmoid(x)


def _softplus(x):
    return jnp.maximum(x, 0.0) + jnp.log1p(jnp.exp(-jnp.abs(x)))


def _split3_dot(m01, a):
    a1 = a.astype(BF16)
    r1 = a - a1.astype(F32)
    a2 = r1.astype(BF16)
    a3 = (r1 - a2.astype(F32)).astype(BF16)
    return _dot(m01, a1) + _dot(m01, a2) + _dot(m01, a3)


def _split2_dot(a, m01):
    a1 = a.astype(BF16)
    a2 = (a - a1.astype(F32)).astype(BF16)
    return _dot(a1, m01) + _dot(a2, m01)


def _rms_mod(x, nw, shift, scale):
    ms = jnp.mean(x * x, axis=-1, keepdims=True)
    y = x * lax.rsqrt(ms + EPS) * nw
    return y * (1.0 + scale) + shift


def _iota2(shape, axis):
    return lax.broadcasted_iota(jnp.int32, shape, axis)


def _params(sem):
    return pltpu.CompilerParams(dimension_semantics=sem, vmem_limit_bytes=VMEM_LIMIT)


def _const_spec(shape, layer=None):
    nd = len(shape)
    if layer is None:
        return pl.BlockSpec(shape, lambda *_: (0,) * nd, pipeline_mode=pl.Buffered(1))
    return pl.BlockSpec((None,) + tuple(shape), lambda *_: (layer,) + (0,) * nd, pipeline_mode=pl.Buffered(1))


def _mod_kernel(c_ref, w_ref, b_ref, o_ref):
    c = c_ref[...]
    o_ref[...] = _dot(_silu(c).astype(BF16), w_ref[...]) + b_ref[...]


def _modulation(c_all, w_mod, b_mod):
    n, d = c_all.shape
    depth, _, nout = w_mod.shape
    tn = D_MODEL
    return pl.pallas_call(
        _mod_kernel,
        grid=(depth, nout // tn),
        in_specs=[pl.BlockSpec((n, d), lambda l, j: (0, 0)),
                  pl.BlockSpec((None, d, tn), lambda l, j: (l, 0, j)),
                  pl.BlockSpec((None, 1, tn), lambda l, j: (l, 0, j))],
        out_specs=pl.BlockSpec((None, n, tn), lambda l, j: (l, 0, j)),
        out_shape=jax.ShapeDtypeStruct((depth, n, nout), F32),
        compiler_params=_params(("arbitrary", "arbitrary")),
        name="adaln_mod",
    )(c_all, w_mod, b_mod.reshape(depth, 1, nout))


def _row_tiling(b, l):
    if l >= ROW_TILE:
        assert l % ROW_TILE == 0
        return 1, ROW_TILE
    assert ROW_TILE % l == 0 and b % (ROW_TILE // l) == 0
    return ROW_TILE // l, l


def _x_spec(bb, ll, nlb):
    return pl.BlockSpec((bb, ll, D_MODEL), lambda i: (i // nlb, i % nlb, 0))


def _mod_spec(k, bb, nlb):
    return pl.BlockSpec((1, bb, 1, D_MODEL), lambda i: (k, i // nlb, 0, 0))


def _rows_spec(r, width):
    return pl.BlockSpec((r, width), lambda i: (i, 0))


def _pair_spec(r, width):
    return pl.BlockSpec((width // LANES, r, LANES), lambda i: (0, i, 0))


def _ffn_kernel(*refs, premix, final):
    refs = list(refs)
    x_ref = refs.pop(0)
    if premix:
        y_ref, o_ref, a_ref, g2_ref, wo_ref = refs[:5]
        refs = refs[5:]
    sh_ref, sc_ref, gt_ref, nw_ref, win_ref, wout_ref = refs[:6]
    refs = refs[6:]
    if final:
        nf_ref = refs.pop(0)
    out_ref, act_ref = refs
    bb, ll, d = x_ref.shape
    r = bb * ll
    x = x_ref[...]
    if premix:
        mix = (_dot(y_ref[...], wo_ref[0:SSD_INNER, :])
               + _dot(o_ref[...], wo_ref[SSD_INNER:SSD_INNER + GLA_DV, :])
               + sum(_dot(a_ref[p], wo_ref[SSD_INNER + GLA_DV + p * LANES:SSD_INNER + GLA_DV + (p + 1) * LANES, :])
                     for p in range(ATT_DIM // LANES)))
        x = x + g2_ref[0] * mix.reshape(bb, ll, d)
    h = _rms_mod(x, nw_ref[...], sh_ref[0], sc_ref[0]).reshape(r, d).astype(BF16)
    for c in range(D_FF // FF_CHUNK):
        g = _dot(h, win_ref[:, c * FF_CHUNK:(c + 1) * FF_CHUNK])
        u = _dot(h, win_ref[:, D_FF + c * FF_CHUNK:D_FF + (c + 1) * FF_CHUNK])
        act_ref[:, c * FF_CHUNK:(c + 1) * FF_CHUNK] = (_silu(g) * u).astype(BF16)
    y = _dot(act_ref[...], wout_ref[...])
    x = x + FFN_RES * gt_ref[0] * y.reshape(bb, ll, d)
    if final:
        ms = jnp.mean(x * x, axis=-1, keepdims=True)
        x = x * lax.rsqrt(ms + EPS) * nf_ref[...]
    out_ref[...] = x


def _ffn(x, mod, mod_base, norm_w, w_in, w_out, layer, premix=None, final_norm=None):
    b, l, d = x.shape
    bb, ll = _row_tiling(b, l)
    nlb = l // ll
    r = bb * ll
    nsteps = (b // bb) * nlb
    args, specs = [x], [_x_spec(bb, ll, nlb)]
    if premix is not None:
        y, o, a, gate_row, wo = premix
        args += [y, o, a, mod, wo]
        specs += [_rows_spec(r, SSD_INNER), _rows_spec(r, GLA_DV), _pair_spec(r, ATT_DIM),
                  _mod_spec(gate_row, bb, nlb), _const_spec((D_MIX, d), layer)]
    args += [mod, mod, mod, norm_w.reshape(1, d), w_in, w_out]
    specs += [_mod_spec(mod_base, bb, nlb), _mod_spec(mod_base + 1, bb, nlb), _mod_spec(mod_base + 2, bb, nlb),
              _const_spec((1, d)), _const_spec((d, 2 * D_FF), layer), _const_spec((D_FF, d), layer)]
    if final_norm is not None:
        args.append(final_norm.reshape(1, d))
        specs.append(_const_spec((1, d)))
    return pl.pallas_call(
        functools.partial(_ffn_kernel, premix=premix is not None, final=final_norm is not None),
        grid=(nsteps,),
        in_specs=specs,
        out_specs=_x_spec(bb, ll, nlb),
        out_shape=jax.ShapeDtypeStruct((b, l, d), F32),
        scratch_shapes=[pltpu.VMEM((r, D_FF), BF16)],
        compiler_params=_params(("arbitrary",)),
        name="ffn",
    )(*args)


def _inproj_kernel(x_ref, sh_ref, sc_ref, nw_ref, w_ref, *out_refs, first_kept):
    bb, ll, d = x_ref.shape
    h = _rms_mod(x_ref[...], nw_ref[...], sh_ref[0], sc_ref[0]).reshape(bb * ll, d).astype(BF16)
    results, start, run = {}, 0, []
    for name, width in PROJ_GROUPS:
        run.append((name, width))
        total = sum(w for _, w in run)
        if total % (2 * LANES) == 0:
            big = _dot(h, w_ref[:, start:start + total])
            o = 0
            for n, w in run:
                results[n] = big[:, o:o + w]
                o += w
            start, run = start + total, []
    assert not run
    for ref, (name, width) in zip(out_refs, PROJ_GROUPS):
        res = results[name]
        if name in PAIR_MAJOR:
            for p in range(width // LANES):
                ref[p] = res[:, p * LANES:(p + 1) * LANES]
        else:
            ref[...] = res
        if first_kept is not None and name in KV_T:
            t_ref = out_refs[len(PROJ_GROUPS) + KV_T.index(name)]

            @pl.when(pl.program_id(0) % first_kept[1] >= first_kept[0])
            def _(t_ref=t_ref, res=res):
                t_ref[0] = res.T


def _inproj(x, mod, norm_w, w_in_p, layer, keep_t=None):
    b, l, d = x.shape
    bb, ll = _row_tiling(b, l)
    nlb = l // ll
    r = bb * ll
    out_specs = [_pair_spec(r, w) if n in PAIR_MAJOR else _rows_spec(r, w) for n, w in PROJ_GROUPS]
    out_shape = [jax.ShapeDtypeStruct((w // LANES, b * l, LANES) if n in PAIR_MAJOR else (b * l, w), F32)
                 for n, w in PROJ_GROUPS]
    first_kept = None
    if keep_t is not None:
        assert bb == 1 and keep_t % ll == 0 and keep_t <= l
        skip = nlb - keep_t // ll
        first_kept = (skip, nlb)
        t_spec = pl.BlockSpec((1, ATT_DIM, ll), lambda i: (i // nlb, 0, jnp.maximum(i % nlb - skip, 0)))
        out_specs += [t_spec] * len(KV_T)
        out_shape += [jax.ShapeDtypeStruct((b, ATT_DIM, keep_t), F32)] * len(KV_T)
    return pl.pallas_call(
        functools.partial(_inproj_kernel, first_kept=first_kept),
        grid=((b // bb) * nlb,),
        in_specs=[_x_spec(bb, ll, nlb), _mod_spec(3, bb, nlb), _mod_spec(4, bb, nlb),
                  _const_spec((1, d)), _const_spec((d, PROJ_WIDTH), layer)],
        out_specs=out_specs,
        out_shape=out_shape,
        compiler_params=_params(("arbitrary",)),
        name="inproj",
    )(x, mod, mod, norm_w.reshape(1, d), w_in_p)


def _ssd_prepare(conv, small, dtb, alog, seg01, seglast01):
    xc = _silu(conv)
    xs = xc[:, 0:SSD_INNER]
    bm = xc[:, SSD_INNER:SSD_INNER + SSD_GROUPS * SSD_STATE]
    cm = xc[:, SSD_INNER + SSD_GROUPS * SSD_STATE:SSD_CONV_DIM]
    dt = _softplus(small + dtb)
    a = dt * (-jnp.exp(alog))
    cum = _split3_dot(seg01, a)
    if seglast01 is None:
        cum_last = jnp.broadcast_to(cum[cum.shape[0] - 1:, :], cum.shape)
    else:
        cum_last = _split3_dot(seglast01, cum)
    return xs, bm, cm, dt, cum, cum_last


def _ssd_diag(xs, bm, cm, dt, cum, cum_t, mask):
    ydiag, xdt = [], []
    cb = [_dot_nt(cm[:, g * SSD_STATE:(g + 1) * SSD_STATE].astype(BF16),
                  bm[:, g * SSD_STATE:(g + 1) * SSD_STATE].astype(BF16)) for g in range(SSD_GROUPS)]
    for h in range(SSD_HEADS):
        g = h // (SSD_HEADS // SSD_GROUPS)
        diff = cum[:, h:h + 1] - cum_t[h:h + 1, :]
        decay = jnp.exp(jnp.where(mask, diff, NEG))
        xh = xs[:, h * SSD_HEAD_DIM:(h + 1) * SSD_HEAD_DIM] * dt[:, h:h + 1]
        xdt.append(xh)
        ydiag.append(_dot((cb[g] * decay).astype(BF16), xh.astype(BF16)))
    return ydiag, xdt


def _ssd_finish(y, xs, z, dvec, normw):
    y = (y + dvec * xs) * _silu(z)
    sq = y * y
    half = SSD_INNER // SSD_GROUPS
    lane = _iota2(y.shape, 1)
    s0 = jnp.sum(jnp.where(lane < half, sq, 0.0), axis=-1, keepdims=True)
    s1 = jnp.sum(jnp.where(lane >= half, sq, 0.0), axis=-1, keepdims=True)
    ms = jnp.where(lane < half, s0, s1) * (1.0 / half)
    return y * lax.rsqrt(ms + EPS) * normw


def _ssd_prompt_kernel(z_ref, xbc_ref, small_ref, cw_ref, cb_ref, dtb_ref, alog_ref, dvec_ref, nw_ref,
                       y_ref, st_ref, h_scr, tail_scr):
    c = pl.program_id(1)
    step = xbc_ref.shape[0]
    rows = SSD_CHUNK
    hpg = SSD_HEADS // SSD_GROUPS

    @pl.when(c == 0)
    def _():
        h_scr[...] = jnp.zeros_like(h_scr)
        tail_scr[0:SUBLANES, :] = jnp.zeros((SUBLANES, SSD_CONV_DIM), F32)

    u = xbc_ref[...]
    tail_scr[SUBLANES:SUBLANES + step, :] = u
    conv = u * cw_ref[SSD_CONV - 1:SSD_CONV, :] + cb_ref[...]
    for j in range(1, SSD_CONV):
        conv = conv + tail_scr[SUBLANES - j:SUBLANES - j + step, :] * cw_ref[SSD_CONV - 1 - j:SSD_CONV - j, :]
    tail_scr[0:SUBLANES, :] = u[step - SUBLANES:, :]

    qi = _iota2((rows, rows), 0)
    si = _iota2((rows, rows), 1)
    mask = si <= qi
    seg01 = jnp.where(mask, 1.0, 0.0).astype(BF16)
    prep, diag, local = [], [], []
    for r0 in range(0, step, rows):
        prep.append(_ssd_prepare(conv[r0:r0 + rows, :], small_ref[r0:r0 + rows, :], dtb_ref[...], alog_ref[...],
                                 seg01, None))
    for xs, bm, cm, dt, cum, cum_last in prep:
        diag.append(_ssd_diag(xs, bm, cm, dt, cum, cum.T, mask))
    for (xs, bm, cm, dt, cum, cum_last), (ydiag, xdt) in zip(prep, diag):
        to_end_t = jnp.exp(cum_last - cum).T
        xdt_t = jnp.concatenate(xdt, axis=1).T
        s_local = []
        for h in range(SSD_HEADS):
            xw_t = xdt_t[h * SSD_HEAD_DIM:(h + 1) * SSD_HEAD_DIM, :] * to_end_t[h:h + 1, :]
            bg = bm[:, (h // hpg) * SSD_STATE:(h // hpg + 1) * SSD_STATE].astype(BF16)
            s_local.append(_dot(xw_t.astype(BF16), bg))
        local.append(s_local)
    hs = [h_scr[h] for h in range(SSD_HEADS)]
    for ci, ((xs, bm, cm, dt, cum, cum_last), (ydiag, xdt)) in enumerate(zip(prep, diag)):
        ys = []
        for h in range(SSD_HEADS):
            cg = cm[:, (h // hpg) * SSD_STATE:(h // hpg + 1) * SSD_STATE].astype(BF16)
            yoff = _dot_nt(cg, hs[h].astype(BF16)) * jnp.exp(cum[:, h:h + 1])
            ys.append(ydiag[h] + yoff)
            hs[h] = jnp.exp(cum_last[0:1, h:h + 1]) * hs[h] + local[ci][h]
        r0 = ci * rows
        y = _ssd_finish(jnp.concatenate(ys, axis=1), xs, z_ref[r0:r0 + rows, :], dvec_ref[...], nw_ref[...])
        y_ref[r0:r0 + rows, :] = y.astype(y_ref.dtype)
    for h in range(SSD_HEADS):
        h_scr[h] = hs[h]

    @pl.when(c == pl.num_programs(1) - 1)
    def _():
        for h in range(SSD_HEADS):
            st_ref[0, h] = hs[h]


def _ssd_vec_specs():
    return [_const_spec((SSD_CONV, SSD_CONV_DIM)), _const_spec((1, SSD_CONV_DIM)), _const_spec((1, LANES)),
            _const_spec((1, LANES)), _const_spec((1, SSD_INNER)), _const_spec((1, SSD_INNER))]


def _ssd_prompt(z, xbc, small, lp, b, l):
    nc = l // SSD_STEP
    rs = lambda w: pl.BlockSpec((SSD_STEP, w), lambda bi, ci: (bi * nc + ci, 0))
    return pl.pallas_call(
        _ssd_prompt_kernel,
        grid=(b, nc),
        in_specs=[rs(SSD_INNER), rs(SSD_CONV_DIM), rs(LANES)] + _ssd_vec_specs(),
        out_specs=[rs(SSD_INNER),
                   pl.BlockSpec((1, SSD_HEADS, SSD_HEAD_DIM, SSD_STATE), lambda bi, ci: (bi, 0, 0, 0))],
        out_shape=[jax.ShapeDtypeStruct((b * l, SSD_INNER), BF16),
                   jax.ShapeDtypeStruct((b, SSD_HEADS, SSD_HEAD_DIM, SSD_STATE), F32)],
        scratch_shapes=[pltpu.VMEM((SSD_HEADS, SSD_HEAD_DIM, SSD_STATE), F32),
                        pltpu.VMEM((SUBLANES + SSD_STEP, SSD_CONV_DIM), F32)],
        compiler_params=_params(("arbitrary", "arbitrary")),
        name="ssd_prompt",
    )(z, xbc, small, lp["conv_w"], lp["conv_b"], lp["dtb"], lp["alog"], lp["dvec"], lp["ssd_nw"])


def _ssd_sample_kernel(z_ref, xbc_ref, small_ref, buf_ref, h0_ref, cw_ref, cb_ref, dtb_ref, alog_ref, dvec_ref,
                       nw_ref, y_ref, st_ref, *, seq):
    rows = xbc_ref.shape[0]
    nseq = rows // seq
    u = xbc_ref[...]
    bufp = buf_ref[...]
    tpos = _iota2((rows, SSD_CONV_DIM), 0) % seq
    conv = u * cw_ref[SSD_CONV - 1:SSD_CONV, :] + cb_ref[...]
    for j in range(1, SSD_CONV):
        uj = pltpu.roll(u, j, axis=0)
        back = (rows - (SSD_CONV - 1 - j)) % rows
        bj = pltpu.roll(bufp, back, axis=0) if back else bufp
        conv = conv + jnp.where(tpos < j, bj, uj) * cw_ref[SSD_CONV - 1 - j:SSD_CONV - j, :]

    qi = _iota2((rows, rows), 0)
    si = _iota2((rows, rows), 1)
    same = (qi // seq) == (si // seq)
    mask = same & (si <= qi)
    seg01 = jnp.where(mask, 1.0, 0.0).astype(BF16)
    last01 = jnp.where(si == (qi // seq) * seq + (seq - 1), 1.0, 0.0).astype(BF16)
    xs, bm, cm, dt, cum, cum_last = _ssd_prepare(conv, small_ref[...], dtb_ref[...], alog_ref[...], seg01, last01)
    cum_t = cum.T
    cum_last_t = cum_last.T
    ydiag, xdt = _ssd_diag(xs, bm, cm, dt, cum, cum_t, mask)
    to_end_t = jnp.exp(cum_last_t - cum_t)
    xdt_t = jnp.concatenate(xdt, axis=1).T
    hpg = SSD_HEADS // SSD_GROUPS
    grows = hpg * SSD_HEAD_DIM
    colseq = _iota2((grows, rows), 1) // seq
    yoff_t = []
    for g in range(SSD_GROUPS):
        cg = cm[:, g * SSD_STATE:(g + 1) * SSD_STATE].astype(BF16)
        bg = bm[:, g * SSD_STATE:(g + 1) * SSD_STATE].astype(BF16)
        xw_t = jnp.concatenate(
            [xdt_t[h * SSD_HEAD_DIM:(h + 1) * SSD_HEAD_DIM, :] * to_end_t[h:h + 1, :]
             for h in range(g * hpg, (g + 1) * hpg)], axis=0)
        acc = jnp.zeros((grows, rows), F32)
        for b in range(nseq):
            h0 = h0_ref[b, g * hpg:(g + 1) * hpg].reshape(grows, SSD_STATE)
            acc = jnp.where(colseq == b, _dot_nt(h0.astype(BF16), cg), acc)
            s_local = _dot(jnp.where(colseq == b, xw_t, 0.0).astype(BF16), bg)
            for hh in range(hpg):
                h = g * hpg + hh
                dec = jnp.exp(cum_last_t[h:h + 1, b * seq:b * seq + 1])
                st_ref[b, h] = (dec * h0[hh * SSD_HEAD_DIM:(hh + 1) * SSD_HEAD_DIM, :]
                                + s_local[hh * SSD_HEAD_DIM:(hh + 1) * SSD_HEAD_DIM, :])
        for hh in range(hpg):
            h = g * hpg + hh
            yoff_t.append(acc[hh * SSD_HEAD_DIM:(hh + 1) * SSD_HEAD_DIM, :] * jnp.exp(cum_t[h:h + 1, :]))
    yoff = jnp.concatenate(yoff_t, axis=0).T
    y = _ssd_finish(jnp.concatenate(ydiag, axis=1) + yoff, xs, z_ref[...], dvec_ref[...], nw_ref[...])
    y_ref[...] = y.astype(y_ref.dtype)


def _ssd_sample(z, xbc, small, bufp, h0_all, layer, lp, b, l):
    rows = SAMPLE_SEQS * l
    rs = lambda w: pl.BlockSpec((rows, w), lambda i: (i, 0))
    st = pl.BlockSpec((SAMPLE_SEQS, SSD_HEADS, SSD_HEAD_DIM, SSD_STATE), lambda i: (i, 0, 0, 0))
    st_in = pl.BlockSpec((None, SAMPLE_SEQS, SSD_HEADS, SSD_HEAD_DIM, SSD_STATE), lambda i: (layer, i, 0, 0, 0))
    return pl.pallas_call(
        functools.partial(_ssd_sample_kernel, seq=l),
        grid=(b // SAMPLE_SEQS,),
        in_specs=[rs(SSD_INNER), rs(SSD_CONV_DIM), rs(LANES), rs(SSD_CONV_DIM), st_in] + _ssd_vec_specs(),
        out_specs=[rs(SSD_INNER), st],
        out_shape=[jax.ShapeDtypeStruct((b * l, SSD_INNER), BF16),
                   jax.ShapeDtypeStruct((b, SSD_HEADS, SSD_HEAD_DIM, SSD_STATE), F32)],
        compiler_params=_params(("arbitrary",)),
        name="ssd_sample",
    )(z, xbc, small, bufp, h0_all, lp["conv_w"], lp["conv_b"], lp["dtb"], lp["alog"], lp["dvec"], lp["ssd_nw"])


def _gla_consts():
    rk = np.arange(GLA_DK)[:, None] // GLA_HEAD_K
    cv = np.arange(GLA_DV)[None, :] // GLA_HEAD_V
    expand = (rk == cv).astype(np.float32)
    rv = np.arange(GLA_DV)[:, None] // GLA_HEAD_V
    seg = (rv == cv).astype(np.float32) / GLA_HEAD_V
    return jnp.asarray(expand, BF16), jnp.asarray(seg, BF16)


def _gla_prepare(gq, gk, small, wg, bg, seg01):
    glin = _dot(small.astype(BF16), wg) + bg
    g = -_softplus(-glin) * (1.0 / GLA_TAU)
    gc = _split3_dot(seg01, g)
    q = gq * (GLA_HEAD_K ** -0.5)
    return q, gk, gc


def _gla_pairwise(q, k, v, gc, expand, diag):
    rows = q.shape[0]
    nb = rows // diag
    q4 = q.reshape(nb, 1, diag, GLA_DK)
    g4 = gc.reshape(nb, 1, diag, GLA_DK)
    k4 = k.reshape(nb, diag, 1, GLA_DK)
    gs4 = gc.reshape(nb, diag, 1, GLA_DK)
    shape = (nb, diag, diag, GLA_DK)
    si = lax.broadcasted_iota(jnp.int32, shape, 1)
    ti = lax.broadcasted_iota(jnp.int32, shape, 2)
    w = jnp.exp(jnp.where(si <= ti, g4 - gs4, NEG))
    m = (q4 * k4 * w).reshape(nb * diag * diag, GLA_DK)
    p = _dot(m.astype(BF16), expand).reshape(nb, diag, diag, GLA_DV)
    o = jnp.sum(p * v.reshape(nb, diag, 1, GLA_DV), axis=1)
    return o.reshape(rows, GLA_DV)


def _gla_finish(o, gr, seg, nw):
    ms = _split2_dot(o * o, seg)
    return o * lax.rsqrt(ms + EPS) * nw * _silu(gr)


def _head_stack(x, head_dim, heads):
    lane = _iota2(x.shape, 1) // head_dim
    return jnp.concatenate([jnp.where(lane == h, x, 0.0) for h in range(heads)], axis=0)


def _gla_prompt_kernel(gq_ref, gk_ref, gv_ref, gr_ref, small_ref, wg_ref, bg_ref, nw_ref, ex_ref, seg_ref,
                       o_ref, st_ref, s_scr):
    c = pl.program_id(1)
    rows = GLA_CHUNK

    @pl.when(c == 0)
    def _():
        s_scr[...] = jnp.zeros_like(s_scr)

    qi = _iota2((rows, rows), 0)
    si = _iota2((rows, rows), 1)
    seg01 = jnp.where(si <= qi, 1.0, 0.0).astype(BF16)
    bd = (_iota2((GLA_DK, GLA_DV), 0) // GLA_HEAD_K) == (_iota2((GLA_DK, GLA_DV), 1) // GLA_HEAD_V)
    s_all = s_scr[...]
    nch = gq_ref.shape[0] // rows
    prep = []
    for ci in range(nch):
        r0 = ci * rows
        q, k, gc = _gla_prepare(gq_ref[r0:r0 + rows, :], gk_ref[r0:r0 + rows, :], small_ref[r0:r0 + rows, :],
                                wg_ref[...], bg_ref[...], seg01)
        v = gv_ref[r0:r0 + rows, :]
        prep.append((q, k, gc, v, v.astype(BF16)))
    levels = []
    half = rows // 2
    while half >= GLA_DIAG:
        levels.append(half)
        half //= 2
    atts = {}
    for ci, (q, k, gc, v, vb) in enumerate(prep):
        for half in levels:
            for blk in range(rows // (2 * half)):
                s0 = blk * 2 * half
                t0 = s0 + half
                ref = gc[t0 - 1:t0, :]
                qs = q[t0:t0 + half, :] * jnp.exp(gc[t0:t0 + half, :] - ref)
                ks = k[s0:t0, :] * jnp.exp(ref - gc[s0:t0, :])
                atts[ci, half, blk] = _dot_nt(_head_stack(qs, GLA_HEAD_K, GLA_HEADS).astype(BF16),
                                              ks.astype(BF16)).astype(BF16)
    outs = [_gla_pairwise(q, k, v, gc, ex_ref[...], GLA_DIAG) for q, k, gc, v, vb in prep]
    for ci, (q, k, gc, v, vb) in enumerate(prep):
        for half in levels:
            vlane = _iota2((half, GLA_DV), 1) // GLA_HEAD_V
            pieces = []
            for blk in range(rows // (2 * half)):
                s0 = blk * 2 * half
                pv = _dot(atts[ci, half, blk], vb[s0:s0 + half, :])
                ot = jnp.zeros((half, GLA_DV), F32)
                for h in range(GLA_HEADS):
                    ot = jnp.where(vlane == h, pv[h * half:(h + 1) * half, :], ot)
                pieces += [jnp.zeros((half, GLA_DV), F32), ot]
            outs[ci] = outs[ci] + jnp.concatenate(pieces, axis=0)
    for ci, (q, k, gc, v, vb) in enumerate(prep):
        outs[ci] = outs[ci] + _dot((q * jnp.exp(gc)).astype(BF16), s_all.astype(BF16))
        gc_t = gc.T
        dcol = gc_t[:, rows - 1:rows]
        kd_t = k.T * jnp.exp(dcol - gc_t)
        upd = _dot(kd_t.astype(BF16), vb)
        s_all = jnp.exp(dcol) * s_all + jnp.where(bd, upd, 0.0)
    o_ref[...] = _gla_finish(jnp.concatenate(outs, axis=0), gr_ref[...], seg_ref[...],
                             nw_ref[...]).astype(o_ref.dtype)
    s_scr[...] = s_all

    @pl.when(c == pl.num_programs(1) - 1)
    def _():
        st_ref[0] = s_all


def _gla_vec_specs():
    return [_const_spec((LANES, GLA_DK)), _const_spec((1, GLA_DK)), _const_spec((1, GLA_DV)),
            _const_spec((GLA_DK, GLA_DV)), _const_spec((GLA_DV, GLA_DV))]


def _gla_prompt(gq, gk, gv, gr, small, lp, b, l):
    nc = l // GLA_STEP
    rs = lambda w: pl.BlockSpec((GLA_STEP, w), lambda bi, ci: (bi * nc + ci, 0))
    expand, seg = _gla_consts()
    return pl.pallas_call(
        _gla_prompt_kernel,
        grid=(b, nc),
        in_specs=[rs(GLA_DK), rs(GLA_DK), rs(GLA_DV), rs(GLA_DV), rs(LANES)] + _gla_vec_specs(),
        out_specs=[rs(GLA_DV), pl.BlockSpec((1, GLA_DK, GLA_DV), lambda bi, ci: (bi, 0, 0))],
        out_shape=[jax.ShapeDtypeStruct((b * l, GLA_DV), BF16),
                   jax.ShapeDtypeStruct((b, GLA_DK, GLA_DV), F32)],
        scratch_shapes=[pltpu.VMEM((GLA_DK, GLA_DV), F32)],
        compiler_params=_params(("arbitrary", "arbitrary")),
        name="gla_prompt",
    )(gq, gk, gv, gr, small, lp["gla_wg"], lp["gla_bg"], lp["gla_nw"], expand, seg)


def _gla_sample_kernel(gq_ref, gk_ref, gv_ref, gr_ref, small_ref, s0_ref, wg_ref, bg_ref, nw_ref, ex_ref, seg_ref,
                       o_ref, st_ref, *, seq):
    rows = gq_ref.shape[0]
    nseq = rows // seq
    qi = _iota2((rows, rows), 0)
    si = _iota2((rows, rows), 1)
    seg01 = jnp.where(((qi // seq) == (si // seq)) & (si <= qi), 1.0, 0.0).astype(BF16)
    last01 = jnp.where(si == (qi // seq) * seq + (seq - 1), 1.0, 0.0).astype(BF16)
    q, k, gc = _gla_prepare(gq_ref[...], gk_ref[...], small_ref[...], wg_ref[...], bg_ref[...], seg01)
    v = gv_ref[...]
    vb = v.astype(BF16)
    gc_last = _split3_dot(last01, gc)
    qg = (q * jnp.exp(gc)).astype(BF16)
    kd_t = (k * jnp.exp(gc_last - gc)).T
    dec_t = jnp.exp(gc_last).T
    colseq = _iota2((GLA_DK, rows), 1) // seq
    rowseq = _iota2((rows, GLA_DV), 0) // seq
    bd = (_iota2((GLA_DK, GLA_DV), 0) // GLA_HEAD_K) == (_iota2((GLA_DK, GLA_DV), 1) // GLA_HEAD_V)
    o = _gla_pairwise(q, k, v, gc, ex_ref[...], seq)
    for b in range(nseq):
        s0 = s0_ref[b]
        o = o + jnp.where(rowseq == b, _dot(qg, s0.astype(BF16)), 0.0)
        upd = _dot(jnp.where(colseq == b, kd_t, 0.0).astype(BF16), vb)
        st_ref[b] = dec_t[:, b * seq:b * seq + 1] * s0 + jnp.where(bd, upd, 0.0)
    o_ref[...] = _gla_finish(o, gr_ref[...], seg_ref[...], nw_ref[...]).astype(o_ref.dtype)


def _gla_sample(gq, gk, gv, gr, small, s0, lp, b, l):
    rows = SAMPLE_SEQS * l
    rs = lambda w: pl.BlockSpec((rows, w), lambda i: (i, 0))
    st = pl.BlockSpec((SAMPLE_SEQS, GLA_DK, GLA_DV), lambda i: (i, 0, 0))
    expand, seg = _gla_consts()
    return pl.pallas_call(
        functools.partial(_gla_sample_kernel, seq=l),
        grid=(b // SAMPLE_SEQS,),
        in_specs=[rs(GLA_DK), rs(GLA_DK), rs(GLA_DV), rs(GLA_DV), rs(LANES), st] + _gla_vec_specs(),
        out_specs=[rs(GLA_DV), st],
        out_shape=[jax.ShapeDtypeStruct((b * l, GLA_DV), BF16),
                   jax.ShapeDtypeStruct((b, GLA_DK, GLA_DV), F32)],
        compiler_params=_params(("arbitrary",)),
        name="gla_sample",
    )(gq, gk, gv, gr, small, s0, lp["gla_wg"], lp["gla_bg"], lp["gla_nw"], expand, seg)


def _gla_state_expand(s):
    b = s.shape[0]
    eye = jnp.eye(GLA_HEADS, dtype=s.dtype)
    return (s[:, :, :, None, :] * eye[None, :, None, :, None]).reshape(b, GLA_DK, GLA_DV)


def _gla_state_extract(s):
    b = s.shape[0]
    s5 = s.reshape(b, GLA_HEADS, GLA_HEAD_K, GLA_HEADS, GLA_HEAD_V)
    return jnp.stack([s5[:, h, :, h, :] for h in range(GLA_HEADS)], axis=1)


def _att_window(qs, ks, vs, valid, prev):
    lane_half = _iota2(qs[0].shape, 1) // ATT_HEAD_DIM
    heads = [(p, half) for p in range(len(qs)) for half in range(2)]
    scores = [_dot_nt(jnp.where(lane_half == half, qs[p], 0.0).astype(BF16), ks[p]) for p, half in heads]
    probs, stats = [], []
    for (p, half), s in zip(heads, scores):
        m_prev, l_prev, _ = prev[p]
        s = jnp.where(valid, s, NEG)
        smax = jnp.max(s, axis=-1, keepdims=True)
        mn = jnp.broadcast_to(smax, qs[p].shape) if m_prev is None else jnp.maximum(m_prev[half], smax)
        pr = jnp.exp2(s - jnp.concatenate([mn] * (s.shape[1] // LANES), axis=1))
        psum = jnp.sum(pr, axis=-1, keepdims=True)
        if m_prev is None:
            alpha, ln = None, jnp.broadcast_to(psum, qs[p].shape)
        else:
            alpha = jnp.exp2(m_prev[half] - mn)
            ln = l_prev[half] * alpha + psum
        probs.append(pr.astype(BF16))
        stats.append((mn, ln, alpha))
    pvs = [_dot(pr, vs[p]) for (p, half), pr in zip(heads, probs)]
    new = []
    for p in range(len(qs)):
        a_prev = prev[p][2]
        (m0, l0, al0), (m1, l1, al1) = stats[2 * p], stats[2 * p + 1]
        a0 = pvs[2 * p] if al0 is None else a_prev * al0 + pvs[2 * p]
        a1 = pvs[2 * p + 1] if al1 is None else a_prev * al1 + pvs[2 * p + 1]
        new.append(([m0, m1], [l0, l1], jnp.where(lane_half == 0, a0, a1)))
    return new


def _att_prompt_kernel(q_ref, k_ref, v_ref, out_ref, acc_scr, m_scr, l_scr):
    tq = ATT_BLOCK
    sup = out_ref.shape[1]
    base = pl.program_id(1) * sup
    rel = tq + _iota2((tq, 2 * tq), 0) - _iota2((tq, 2 * tq), 1)
    band = (rel >= 0) & (rel <= ATT_KEYS - 1)
    in_cur = _iota2((tq, 2 * tq), 1) >= tq
    strides = sorted((d for _, d in DILATION_PATTERNS), reverse=True)
    for idx, d in enumerate(strides):
        first, last = idx == 0, idx == len(strides) - 1

        def rows(start, d=d):
            return pl.ds(start, tq, stride=d) if d > 1 else pl.ds(start, tq)

        def body(sb, carry, d=d, first=first, last=last, rows=rows):
            if d > 1:
                r = sb % d
                mi = sb // d
                loc = r + d * tq * mi
                start_q = base + loc
            else:
                mi = sb
                loc = pl.multiple_of(sb * tq, tq)
                start_q = pl.multiple_of(base + loc, tq)
            mglob = base // (d * tq) + mi
            start_p = jnp.where(mglob == 0, start_q, start_q - d * tq)
            valid = band & ((mglob > 0) | in_cur)
            npair = ATT_HEADS // 2
            prev = []
            for p in range(npair):
                if first:
                    prev.append((None, None, None))
                else:
                    prev.append(([m_scr[2 * p + hf, rows(loc), :] for hf in range(2)],
                                 [l_scr[2 * p + hf, rows(loc), :] for hf in range(2)],
                                 acc_scr[p, rows(loc), :]))
            qs = [(q_ref[p, rows(start_q), :] * (ATT_HEAD_DIM ** -0.5 * math.log2(math.e))).astype(BF16)
                  for p in range(npair)]
            ks = [jnp.concatenate([k_ref[p, rows(start_p), :], k_ref[p, rows(start_q), :]], axis=0).astype(BF16)
                  for p in range(npair)]
            vs = [jnp.concatenate([v_ref[p, rows(start_p), :], v_ref[p, rows(start_q), :]], axis=0).astype(BF16)
                  for p in range(npair)]
            new = _att_window(qs, ks, vs, valid, prev)
            lane_half = _iota2((tq, LANES), 1) // ATT_HEAD_DIM
            for p in range(npair):
                m_new, l_new, a_new = new[p]
                if last:
                    den = jnp.where(lane_half == 0, l_new[0], l_new[1])
                    out_ref[p, rows(loc), :] = (a_new / den).astype(out_ref.dtype)
                else:
                    for hf in range(2):
                        m_scr[2 * p + hf, rows(loc), :] = m_new[hf]
                        l_scr[2 * p + hf, rows(loc), :] = l_new[hf]
                    acc_scr[p, rows(loc), :] = a_new
            return carry

        lax.fori_loop(0, sup // tq, body, 0)


def _att_prompt(aq, ak, av, b, l):
    npair = ATT_HEADS // 2
    sup = ATT_BLOCK * max(d for _, d in DILATION_PATTERNS)
    assert l % sup == 0
    seq = pl.BlockSpec((npair, l, LANES), lambda bi, j: (0, bi, 0), pipeline_mode=pl.Buffered(1))
    return pl.pallas_call(
        _att_prompt_kernel,
        grid=(b, l // sup),
        in_specs=[seq, seq, seq],
        out_specs=pl.BlockSpec((npair, sup, LANES), lambda bi, j: (0, bi * (l // sup) + j, 0)),
        out_shape=jax.ShapeDtypeStruct((npair, b * l, LANES), BF16),
        scratch_shapes=[pltpu.VMEM((npair, sup, LANES), F32), pltpu.VMEM((ATT_HEADS, sup, LANES), F32),
                        pltpu.VMEM((ATT_HEADS, sup, LANES), F32)],
        compiler_params=_params(("arbitrary", "arbitrary")),
        name="att_prompt",
    )(aq, ak, av)


def _att_counts(seq, nbuf):
    qpos = nbuf + np.arange(seq)[:, None]
    kpos = np.arange(nbuf + seq)[None, :]
    delta = qpos - kpos
    cnt = np.zeros(delta.shape, np.float32)
    for window, stride in DILATION_PATTERNS:
        cnt += ((delta >= 0) & (delta % stride == 0) & (delta <= window)).astype(np.float32)
    cnt = np.tile(cnt, (2, 1))
    new = np.zeros((2 * seq, LANES), np.float32)
    new[:, :seq] = cnt[:, nbuf:]
    return jnp.asarray(cnt[:, :nbuf]), jnp.asarray(new)


def _att_sample_kernel(q_ref, kn_ref, vn_ref, kc_ref, vc_ref, cc_ref, cn_ref, out_ref):
    nseq = kc_ref.shape[0]
    seq = q_ref.shape[1] // nseq
    cc = cc_ref[...]
    cn = cn_ref[...]
    pad = jnp.zeros((LANES - seq, LANES), F32)
    lane_half = _iota2((seq, LANES), 1) // ATT_HEAD_DIM
    for b in range(nseq):
        rows = slice(b * seq, (b + 1) * seq)
        for p in range(ATT_HEADS // 2):
            q = q_ref[p, rows, :] * (ATT_HEAD_DIM ** -0.5)
            q2 = jnp.concatenate([jnp.where(lane_half == 0, q, 0.0), jnp.where(lane_half == 1, q, 0.0)],
                                 axis=0).astype(BF16)
            kt = kc_ref[b, 2 * p:2 * p + 2].reshape(LANES, -1).astype(BF16)
            vt = vc_ref[b, 2 * p:2 * p + 2].reshape(LANES, -1).astype(BF16)
            kn = jnp.concatenate([kn_ref[p, rows, :], pad], axis=0).astype(BF16)
            vn = jnp.concatenate([vn_ref[p, rows, :], pad], axis=0).astype(BF16)
            sc = jnp.where(cc > 0, _dot(q2, kt), NEG)
            sn = jnp.where(cn > 0, _dot_nt(q2, kn), NEG)
            m = jnp.maximum(jnp.max(sc, axis=-1, keepdims=True), jnp.max(sn, axis=-1, keepdims=True))
            pc = cc * jnp.exp(sc - m)
            pn = cn * jnp.exp(sn - m)
            den = jnp.sum(pc, axis=-1, keepdims=True) + jnp.sum(pn, axis=-1, keepdims=True)
            o = (_dot_nt(pc.astype(BF16), vt) + _dot(pn.astype(BF16), vn)) / den
            out_ref[p, rows, :] = jnp.where(lane_half == 0, o[0:seq, :], o[seq:2 * seq, :]).astype(out_ref.dtype)


def _att_sample(aq, ak, av, kcache_t, vcache_t, layer, b, l):
    npair = ATT_HEADS // 2
    nbuf = kcache_t.shape[-1]
    cc, cn = _att_counts(l, nbuf)
    ns = ATT_SAMPLE_SEQS
    assert b % ns == 0
    new = pl.BlockSpec((npair, ns * l, LANES), lambda i: (0, i, 0))
    cache = pl.BlockSpec((None, ns, ATT_HEADS, ATT_HEAD_DIM, nbuf), lambda i: (layer, i, 0, 0, 0))
    return pl.pallas_call(
        _att_sample_kernel,
        grid=(b // ns,),
        in_specs=[new, new, new, cache, cache, _const_spec((2 * l, nbuf)), _const_spec((2 * l, LANES))],
        out_specs=new,
        out_shape=jax.ShapeDtypeStruct((npair, b * l, LANES), BF16),
        compiler_params=_params(("arbitrary",)),
        name="att_sample",
    )(aq, ak, av, kcache_t, vcache_t, cc, cn)


def _pad_lanes(v, width, offset=0):
    out = jnp.zeros((1, width), F32)
    return out.at[0, offset:offset + v.shape[0]].set(v.astype(F32))


def _wt_kernel(w_ref, o_ref):
    o_ref[...] = w_ref[...].T.astype(o_ref.dtype)


def _proj_weights(w_in):
    depth = w_in.shape[0]
    wt = w_in.transpose(0, 2, 1)
    offs = np.concatenate([[0], np.cumsum(IN_SPLITS)])
    rows = {n: wt[:, offs[i]:offs[i + 1], :] for i, n in enumerate(
        ("z", "xbc", "dt", "gq", "gk", "gv", "gr", "glr", "aq", "ak", "av"))}
    zeros = lambda n: jnp.zeros((depth, n, D_MODEL), F32)
    rows["small"] = jnp.concatenate(
        [zeros(SMALL_DT_OFF), rows["dt"], zeros(SMALL_GLR_OFF - SMALL_DT_OFF - SSD_HEADS), rows["glr"],
         zeros(LANES - SMALL_GLR_OFF - GLA_GATE_RANK)], axis=1)
    wt_p = jnp.concatenate([rows[n] for n, _ in PROJ_GROUPS], axis=1)
    tn = 2 * LANES
    return pl.pallas_call(
        _wt_kernel,
        grid=(depth, PROJ_WIDTH // tn),
        in_specs=[pl.BlockSpec((None, tn, D_MODEL), lambda l, j: (l, j, 0))],
        out_specs=pl.BlockSpec((None, D_MODEL, tn), lambda l, j: (l, 0, j)),
        out_shape=jax.ShapeDtypeStruct((depth, D_MODEL, PROJ_WIDTH), BF16),
        compiler_params=_params(("arbitrary", "arbitrary")),
        name="proj_weights",
    )(wt_p)


def _layer_params(l, ssd_conv_w, ssd_conv_b, ssd_dt_bias, ssd_a_log, ssd_d, ssd_norm_w,
                  gla_w_gate, gla_b_gate, gla_norm_w, norm_w):
    wg = jnp.zeros((LANES, GLA_DK), F32).at[SMALL_GLR_OFF:SMALL_GLR_OFF + GLA_GATE_RANK, :].set(gla_w_gate[l])
    return dict(
        conv_w=ssd_conv_w[l], conv_b=ssd_conv_b[l].reshape(1, SSD_CONV_DIM),
        dtb=_pad_lanes(ssd_dt_bias[l], LANES, SMALL_DT_OFF), alog=_pad_lanes(ssd_a_log[l], LANES, SMALL_DT_OFF),
        dvec=jnp.repeat(ssd_d[l].astype(F32), SSD_HEAD_DIM).reshape(1, SSD_INNER),
        ssd_nw=ssd_norm_w[l].reshape(1, SSD_INNER),
        gla_wg=wg.astype(BF16), gla_bg=gla_b_gate[l].reshape(1, GLA_DK),
        gla_nw=jnp.tile(gla_norm_w[l], GLA_HEADS).reshape(1, GLA_DV),
        norm_w=norm_w[l],
    )


def _trunk(x, mods, layers, stacked, norm_f, states, sample):
    b, l, _ = x.shape
    keep = min(ATT_MAX_WINDOW, l)
    names = [n for n, _ in PROJ_GROUPS]
    outs = ([], [], [], [], [])
    for li, lp in enumerate(layers):
        mod = mods[li]
        x = _ffn(x, mod, 0, lp["norm_w"][0], stacked["ffn1_in"], stacked["ffn1_out"], li)
        if sample:
            proj = dict(zip(names, _inproj(x, mod, lp["norm_w"][1], stacked["w_in_p"], li)))
            st_ssd, st_conv, st_gla, kcache, vcache = states
            bufp = jnp.pad(st_conv[li], ((0, 0), (0, l - (SSD_CONV - 1)), (0, 0))).reshape(b * l, SSD_CONV_DIM)
            y, ssd_new = _ssd_sample(proj["z"], proj["xbc"], proj["small"], bufp, st_ssd, li, lp, b, l)
            o, gla_new = _gla_sample(proj["gq"], proj["gk"], proj["gv"], proj["gr"], proj["small"],
                                     _gla_state_expand(st_gla[li]), lp, b, l)
            att = _att_sample(proj["aq"], proj["ak"], proj["av"], kcache, vcache, li, b, l)
            for acc, name in ((outs[3], "ak"), (outs[4], "av")):
                kv = proj[name].reshape(ATT_HEADS // 2, b, l, 2, ATT_HEAD_DIM)[:, :, l - keep:]
                acc.append(kv.transpose(1, 2, 0, 3, 4).reshape(b, keep, ATT_HEADS, ATT_HEAD_DIM))
        else:
            res = _inproj(x, mod, lp["norm_w"][1], stacked["w_in_p"], li, keep_t=keep)
            proj = dict(zip(names, res))
            y, ssd_new = _ssd_prompt(proj["z"], proj["xbc"], proj["small"], lp, b, l)
            o, gla_new = _gla_prompt(proj["gq"], proj["gk"], proj["gv"], proj["gr"], proj["small"], lp, b, l)
            att = _att_prompt(proj["aq"], proj["ak"], proj["av"], b, l)
            for acc, kv_t in zip((outs[3], outs[4]), res[len(names):]):
                acc.append(kv_t.reshape(b, ATT_HEADS, ATT_HEAD_DIM, keep).transpose(0, 3, 1, 2))
        x = _ffn(x, mod, 6, lp["norm_w"][2], stacked["ffn2_in"], stacked["ffn2_out"], li,
                 premix=(y, o, att, 5, stacked["w_out"]),
                 final_norm=norm_f if li == len(layers) - 1 else None)
        outs[0].append(ssd_new)
        outs[1].append(proj["xbc"].reshape(b, l, SSD_CONV_DIM)[:, l - (SSD_CONV - 1):])
        outs[2].append(_gla_state_extract(gla_new))
    return x, [jnp.stack(a) for a in outs]


def kernel(x_prompt, x_sample, c_prompt, c_sample, state_ssd, state_ssd_conv, state_gla, cache_attn_k, cache_attn_v,
           w_in, w_out, ssd_conv_w, ssd_conv_b, ssd_dt_bias, ssd_a_log, ssd_d, ssd_norm_w,
           gla_w_gate, gla_b_gate, gla_norm_w, norm_w, w_mod, b_mod,
           ffn1_w_in, ffn1_w_out, ffn2_w_in, ffn2_w_out, norm_f):
    bp, bs = x_prompt.shape[0], x_sample.shape[0]
    depth = w_in.shape[0]
    layers = [_layer_params(l, ssd_conv_w, ssd_conv_b, ssd_dt_bias, ssd_a_log, ssd_d, ssd_norm_w,
                            gla_w_gate, gla_b_gate, gla_norm_w, norm_w) for l in range(depth)]
    stacked = dict(w_in_p=_proj_weights(w_in), w_out=w_out.astype(BF16), ffn1_in=ffn1_w_in.astype(BF16), ffn1_out=ffn1_w_out.astype(BF16),
                   ffn2_in=ffn2_w_in.astype(BF16), ffn2_out=ffn2_w_out.astype(BF16))
    npad = -(bp + bs) % SUBLANES
    c_all = jnp.concatenate([c_prompt, c_sample, jnp.zeros((npad, D_MODEL), F32)], axis=0)
    m_all = _modulation(c_all, w_mod.astype(BF16), b_mod)
    mods_p, mods_s = [], []
    for l in range(depth):
        m = m_all[l].reshape(-1, ADALN_MODS, 1, D_MODEL).transpose(1, 0, 2, 3)
        mods_p.append(m[:, :bp])
        mods_s.append(m[:, bp:bp + bs])
    kcache = cache_attn_k.transpose(0, 1, 3, 4, 2)
    vcache = cache_attn_v.transpose(0, 1, 3, 4, 2)
    y_p, (ssd_p, conv_p, gla_p, k_p, v_p) = _trunk(x_prompt, mods_p, layers, stacked, norm_f, None, sample=False)
    y_s, (ssd_s, conv_s, gla_s, k_s, v_s) = _trunk(
        x_sample, mods_s, layers, stacked, norm_f, (state_ssd, state_ssd_conv, state_gla, kcache, vcache),
        sample=True)
    return (y_p, y_s, ssd_p, ssd_s, conv_p, conv_s, gla_p, gla_s, k_p, k_s, v_p, v_s)
```

```python
import functools
import math

import numpy as np
import jax
import jax.numpy as jnp
from jax import lax
from jax.experimental import pallas as pl
from jax.experimental.pallas import tpu as pltpu

F32 = jnp.float32
BF16 = jnp.bfloat16

D_MODEL = 1024
DEPTH = 2
SSD_HEADS = 6
SSD_HEAD_DIM = 64
SSD_INNER = SSD_HEADS * SSD_HEAD_DIM
SSD_GROUPS = 2
SSD_STATE = 128
SSD_CONV = 4
SSD_CONV_DIM = SSD_INNER + 2 * SSD_GROUPS * SSD_STATE
GLA_HEADS = 4
GLA_HEAD_K = 32
GLA_HEAD_V = 64
GLA_DK = GLA_HEADS * GLA_HEAD_K
GLA_DV = GLA_HEADS * GLA_HEAD_V
GLA_GATE_RANK = 16
GLA_TAU = 16.0
ATT_HEADS = 6
ATT_HEAD_DIM = 64
ATT_DIM = ATT_HEADS * ATT_HEAD_DIM
DILATION_PATTERNS = ((128, 1), (512, 4), (2048, 16))
ATT_MAX_WINDOW = 2048
ATT_KEYS = 129
D_MIX = SSD_INNER + GLA_DV + ATT_DIM
IN_SPLITS = (SSD_INNER, SSD_CONV_DIM, SSD_HEADS, GLA_DK, GLA_DK, GLA_DV, GLA_DV, GLA_GATE_RANK,
             ATT_DIM, ATT_DIM, ATT_DIM)
D_FF = 2816
ADALN_MODS = 9
FFN_RES = 0.5
EPS = 1e-6

LANES = 128
SUBLANES = 8
VMEM_LIMIT = 56 * 1024 * 1024

PROJ_GROUPS = (("z", SSD_INNER), ("xbc", SSD_CONV_DIM), ("gq", GLA_DK), ("gk", GLA_DK), ("gv", GLA_DV),
               ("gr", GLA_DV), ("aq", ATT_DIM), ("ak", ATT_DIM), ("av", ATT_DIM), ("small", LANES))
PROJ_WIDTH = sum(w for _, w in PROJ_GROUPS)
PAIR_MAJOR = ("aq", "ak", "av")
KV_T = ("ak", "av")
SMALL_DT_OFF = 0
SMALL_GLR_OFF = 8

ROW_TILE = 512
FF_CHUNK = 256
SSD_CHUNK = 128
SSD_STEP = 512
GLA_CHUNK = 128
GLA_STEP = 512
GLA_DIAG = 16
SAMPLE_SEQS = 16
ATT_BLOCK = 128
ATT_SAMPLE_SEQS = 2
ATT_UNROLL = 2
NEG = -1e30


def _dot(a, b):
    return jnp.dot(a, b, preferred_element_type=F32)


def _dot_nt(a, b):
    return lax.dot_general(a, b, (((1,), (1,)), ((), ())), preferred_element_type=F32)


def _sigmoid(x):
    return 1.0 / (1.0 + jnp.exp(-x))


def _silu(x):
    return x * _sigmoid(x)


def _softplus(x):
    return jnp.maximum(x, 0.0) + jnp.log1p(jnp.exp(-jnp.abs(x)))


def _split3_dot(m01, a):
    a1 = a.astype(BF16)
    r1 = a - a1.astype(F32)
    a2 = r1.astype(BF16)
    a3 = (r1 - a2.astype(F32)).astype(BF16)
    return _dot(m01, a1) + _dot(m01, a2) + _dot(m01, a3)


def _split2_dot(a, m01):
    a1 = a.astype(BF16)
    a2 = (a - a1.astype(F32)).astype(BF16)
    return _dot(a1, m01) + _dot(a2, m01)


def _rms_mod(x, nw, shift, scale):
    ms = jnp.mean(x * x, axis=-1, keepdims=True)
    y = x * lax.rsqrt(ms + EPS) * nw
    return y * (1.0 + scale) + shift


def _iota2(shape, axis):
    return lax.broadcasted_iota(jnp.int32, shape, axis)


def _params(sem):
    return pltpu.CompilerParams(dimension_semantics=sem, vmem_limit_bytes=VMEM_LIMIT)


def _const_spec(shape, layer=None):
    nd = len(shape)
    if layer is None:
        return pl.BlockSpec(shape, lambda *_: (0,) * nd, pipeline_mode=pl.Buffered(1))
    return pl.BlockSpec((None,) + tuple(shape), lambda *_: (layer,) + (0,) * nd, pipeline_mode=pl.Buffered(1))


def _mod_kernel(c_ref, w_ref, b_ref, o_ref):
    c = c_ref[...]
    o_ref[...] = _dot(_silu(c).astype(BF16), w_ref[...]) + b_ref[...]


def _modulation(c_all, w_mod, b_mod):
    n, d = c_all.shape
    depth, _, nout = w_mod.shape
    tn = D_MODEL
    return pl.pallas_call(
        _mod_kernel,
        grid=(depth, nout // tn),
        in_specs=[pl.BlockSpec((n, d), lambda l, j: (0, 0)),
                  pl.BlockSpec((None, d, tn), lambda l, j: (l, 0, j)),
                  pl.BlockSpec((None, 1, tn), lambda l, j: (l, 0, j))],
        out_specs=pl.BlockSpec((None, n, tn), lambda l, j: (l, 0, j)),
        out_shape=jax.ShapeDtypeStruct((depth, n, nout), F32),
        compiler_params=_params(("arbitrary", "arbitrary")),
        name="adaln_mod",
    )(c_all, w_mod, b_mod.reshape(depth, 1, nout))


def _row_tiling(b, l):
    if l >= ROW_TILE:
        assert l % ROW_TILE == 0
        return 1, ROW_TILE
    assert ROW_TILE % l == 0 and b % (ROW_TILE // l) == 0
    return ROW_TILE // l, l


def _x_spec(bb, ll, nlb):
    return pl.BlockSpec((bb, ll, D_MODEL), lambda i: (i // nlb, i % nlb, 0))


def _mod_spec(k, bb, nlb):
    return pl.BlockSpec((1, bb, 1, D_MODEL), lambda i: (k, i // nlb, 0, 0))


def _rows_spec(r, width):
    return pl.BlockSpec((r, width), lambda i: (i, 0))


def _pair_spec(r, width):
    return pl.BlockSpec((width // LANES, r, LANES), lambda i: (0, i, 0))


def _ffn_kernel(*refs, premix, final):
    refs = list(refs)
    x_ref = refs.pop(0)
    if premix:
        y_ref, o_ref, a_ref, g2_ref, wo_ref = refs[:5]
        refs = refs[5:]
    sh_ref, sc_ref, gt_ref, nw_ref, win_ref, wout_ref = refs[:6]
    refs = refs[6:]
    if final:
        nf_ref = refs.pop(0)
    out_ref, act_ref = refs
    bb, ll, d = x_ref.shape
    r = bb * ll
    x = x_ref[...]
    if premix:
        mix = (_dot(y_ref[...], wo_ref[0:SSD_INNER, :])
               + _dot(o_ref[...], wo_ref[SSD_INNER:SSD_INNER + GLA_DV, :])
               + sum(_dot(a_ref[p], wo_ref[SSD_INNER + GLA_DV + p * LANES:SSD_INNER + GLA_DV + (p + 1) * LANES, :])
                     for p in range(ATT_DIM // LANES)))
        x = x + g2_ref[0] * mix.reshape(bb, ll, d)
    h = _rms_mod(x, nw_ref[...], sh_ref[0], sc_ref[0]).reshape(r, d).astype(BF16)
    for c in range(D_FF // FF_CHUNK):
        g = _dot(h, win_ref[:, c * FF_CHUNK:(c + 1) * FF_CHUNK])
        u = _dot(h, win_ref[:, D_FF + c * FF_CHUNK:D_FF + (c + 1) * FF_CHUNK])
        act_ref[:, c * FF_CHUNK:(c + 1) * FF_CHUNK] = (_silu(g) * u).astype(BF16)
    y = _dot(act_ref[...], wout_ref[...])
    x = x + FFN_RES * gt_ref[0] * y.reshape(bb, ll, d)
    if final:
        ms = jnp.mean(x * x, axis=-1, keepdims=True)
        x = x * lax.rsqrt(ms + EPS) * nf_ref[...]
    out_ref[...] = x


def _ffn(x, mod, mod_base, norm_w, w_in, w_out, layer, premix=None, final_norm=None):
    b, l, d = x.shape
    bb, ll = _row_tiling(b, l)
    nlb = l // ll
    r = bb * ll
    nsteps = (b // bb) * nlb
    args, specs = [x], [_x_spec(bb, ll, nlb)]
    if premix is not None:
        y, o, a, gate_row, wo = premix
        args += [y, o, a, mod, wo]
        specs += [_rows_spec(r, SSD_INNER), _rows_spec(r, GLA_DV), _pair_spec(r, ATT_DIM),
                  _mod_spec(gate_row, bb, nlb), _const_spec((D_MIX, d), layer)]
    args += [mod, mod, mod, norm_w.reshape(1, d), w_in, w_out]
    specs += [_mod_spec(mod_base, bb, nlb), _mod_spec(mod_base + 1, bb, nlb), _mod_spec(mod_base + 2, bb, nlb),
              _const_spec((1, d)), _const_spec((d, 2 * D_FF), layer), _const_spec((D_FF, d), layer)]
    if final_norm is not None:
        args.append(final_norm.reshape(1, d))
        specs.append(_const_spec((1, d)))
    return pl.pallas_call(
        functools.partial(_ffn_kernel, premix=premix is not None, final=final_norm is not None),
        grid=(nsteps,),
        in_specs=specs,
        out_specs=_x_spec(bb, ll, nlb),
        out_shape=jax.ShapeDtypeStruct((b, l, d), F32),
        scratch_shapes=[pltpu.VMEM((r, D_FF), BF16)],
        compiler_params=_params(("arbitrary",)),
        name="ffn",
    )(*args)


def _inproj_kernel(x_ref, sh_ref, sc_ref, nw_ref, w_ref, *out_refs, first_kept):
    bb, ll, d = x_ref.shape
    h = _rms_mod(x_ref[...], nw_ref[...], sh_ref[0], sc_ref[0]).reshape(bb * ll, d).astype(BF16)
    results, start, run = {}, 0, []
    for name, width in PROJ_GROUPS:
        run.append((name, width))
        total = sum(w for _, w in run)
        if total % (2 * LANES) == 0:
            big = _dot(h, w_ref[:, start:start + total])
            o = 0
            for n, w in run:
                results[n] = big[:, o:o + w]
                o += w
            start, run = start + total, []
    assert not run
    for ref, (name, width) in zip(out_refs, PROJ_GROUPS):
        res = results[name]
        if name in PAIR_MAJOR:
            for p in range(width // LANES):
                ref[p] = res[:, p * LANES:(p + 1) * LANES]
        else:
            ref[...] = res
        if first_kept is not None and name in KV_T:
            t_ref = out_refs[len(PROJ_GROUPS) + KV_T.index(name)]

            @pl.when(pl.program_id(0) % first_kept[1] >= first_kept[0])
            def _(t_ref=t_ref, res=res):
                t_ref[0] = res.T


def _inproj(x, mod, norm_w, w_in_p, layer, keep_t=None):
    b, l, d = x.shape
    bb, ll = _row_tiling(b, l)
    nlb = l // ll
    r = bb * ll
    out_specs = [_pair_spec(r, w) if n in PAIR_MAJOR else _rows_spec(r, w) for n, w in PROJ_GROUPS]
    out_shape = [jax.ShapeDtypeStruct((w // LANES, b * l, LANES) if n in PAIR_MAJOR else (b * l, w), F32)
                 for n, w in PROJ_GROUPS]
    first_kept = None
    if keep_t is not None:
        assert bb == 1 and keep_t % ll == 0 and keep_t <= l
        skip = nlb - keep_t // ll
        first_kept = (skip, nlb)
        t_spec = pl.BlockSpec((1, ATT_DIM, ll), lambda i: (i // nlb, 0, jnp.maximum(i % nlb - skip, 0)))
        out_specs += [t_spec] * len(KV_T)
        out_shape += [jax.ShapeDtypeStruct((b, ATT_DIM, keep_t), F32)] * len(KV_T)
    return pl.pallas_call(
        functools.partial(_inproj_kernel, first_kept=first_kept),
        grid=((b // bb) * nlb,),
        in_specs=[_x_spec(bb, ll, nlb), _mod_spec(3, bb, nlb), _mod_spec(4, bb, nlb),
                  _const_spec((1, d)), _const_spec((d, PROJ_WIDTH), layer)],
        out_specs=out_specs,
        out_shape=out_shape,
        compiler_params=_params(("arbitrary",)),
        name="inproj",
    )(x, mod, mod, norm_w.reshape(1, d), w_in_p)


def _ssd_prepare(conv, small, dtb, alog, seg01, seglast01):
    xc = _silu(conv)
    xs = xc[:, 0:SSD_INNER]
    bm = xc[:, SSD_INNER:SSD_INNER + SSD_GROUPS * SSD_STATE]
    cm = xc[:, SSD_INNER + SSD_GROUPS * SSD_STATE:SSD_CONV_DIM]
    dt = _softplus(small + dtb)
    a = dt * (-jnp.exp(alog))
    cum = _split3_dot(seg01, a)
    if seglast01 is None:
        cum_last = jnp.broadcast_to(cum[cum.shape[0] - 1:, :], cum.shape)
    else:
        cum_last = _split3_dot(seglast01, cum)
    return xs, bm, cm, dt, cum, cum_last


def _ssd_diag(xs, bm, cm, dt, cum, cum_t, mask):
    ydiag, xdt = [], []
    cb = [_dot_nt(cm[:, g * SSD_STATE:(g + 1) * SSD_STATE].astype(BF16),
                  bm[:, g * SSD_STATE:(g + 1) * SSD_STATE].astype(BF16)) for g in range(SSD_GROUPS)]
    for h in range(SSD_HEADS):
        g = h // (SSD_HEADS // SSD_GROUPS)
        diff = cum[:, h:h + 1] - cum_t[h:h + 1, :]
        decay = jnp.exp(jnp.where(mask, diff, NEG))
        xh = xs[:, h * SSD_HEAD_DIM:(h + 1) * SSD_HEAD_DIM] * dt[:, h:h + 1]
        xdt.append(xh)
        ydiag.append(_dot((cb[g] * decay).astype(BF16), xh.astype(BF16)))
    return ydiag, xdt


def _ssd_finish(y, xs, z, dvec, normw):
    y = (y + dvec * xs) * _silu(z)
    sq = y * y
    half = SSD_INNER // SSD_GROUPS
    lane = _iota2(y.shape, 1)
    s0 = jnp.sum(jnp.where(lane < half, sq, 0.0), axis=-1, keepdims=True)
    s1 = jnp.sum(jnp.where(lane >= half, sq, 0.0), axis=-1, keepdims=True)
    ms = jnp.where(lane < half, s0, s1) * (1.0 / half)
    return y * lax.rsqrt(ms + EPS) * normw


def _ssd_prompt_kernel(z_ref, xbc_ref, small_ref, cw_ref, cb_ref, dtb_ref, alog_ref, dvec_ref, nw_ref,
                       y_ref, st_ref, h_scr, tail_scr):
    c = pl.program_id(1)
    step = xbc_ref.shape[0]
    rows = SSD_CHUNK
    hpg = SSD_HEADS // SSD_GROUPS

    @pl.when(c == 0)
    def _():
        h_scr[...] = jnp.zeros_like(h_scr)
        tail_scr[0:SUBLANES, :] = jnp.zeros((SUBLANES, SSD_CONV_DIM), F32)

    u = xbc_ref[...]
    tail_scr[SUBLANES:SUBLANES + step, :] = u
    conv = u * cw_ref[SSD_CONV - 1:SSD_CONV, :] + cb_ref[...]
    for j in range(1, SSD_CONV):
        conv = conv + tail_scr[SUBLANES - j:SUBLANES - j + step, :] * cw_ref[SSD_CONV - 1 - j:SSD_CONV - j, :]
    tail_scr[0:SUBLANES, :] = u[step - SUBLANES:, :]

    qi = _iota2((rows, rows), 0)
    si = _iota2((rows, rows), 1)
    mask = si <= qi
    seg01 = jnp.where(mask, 1.0, 0.0).astype(BF16)
    prep, diag, local = [], [], []
    for r0 in range(0, step, rows):
        prep.append(_ssd_prepare(conv[r0:r0 + rows, :], small_ref[r0:r0 + rows, :], dtb_ref[...], alog_ref[...],
                                 seg01, None))
    for xs, bm, cm, dt, cum, cum_last in prep:
        diag.append(_ssd_diag(xs, bm, cm, dt, cum, cum.T, mask))
    for (xs, bm, cm, dt, cum, cum_last), (ydiag, xdt) in zip(prep, diag):
        to_end_t = jnp.exp(cum_last - cum).T
        xdt_t = jnp.concatenate(xdt, axis=1).T
        s_local = []
        for h in range(SSD_HEADS):
            xw_t = xdt_t[h * SSD_HEAD_DIM:(h + 1) * SSD_HEAD_DIM, :] * to_end_t[h:h + 1, :]
            bg = bm[:, (h // hpg) * SSD_STATE:(h // hpg + 1) * SSD_STATE].astype(BF16)
            s_local.append(_dot(xw_t.astype(BF16), bg))
        local.append(s_local)
    hs = [h_scr[h] for h in range(SSD_HEADS)]
    for ci, ((xs, bm, cm, dt, cum, cum_last), (ydiag, xdt)) in enumerate(zip(prep, diag)):
        ys = []
        for h in range(SSD_HEADS):
            cg = cm[:, (h // hpg) * SSD_STATE:(h // hpg + 1) * SSD_STATE].astype(BF16)
            yoff = _dot_nt(cg, hs[h].astype(BF16)) * jnp.exp(cum[:, h:h + 1])
            ys.append(ydiag[h] + yoff)
            hs[h] = jnp.exp(cum_last[0:1, h:h + 1]) * hs[h] + local[ci][h]
        r0 = ci * rows
        y = _ssd_finish(jnp.concatenate(ys, axis=1), xs, z_ref[r0:r0 + rows, :], dvec_ref[...], nw_ref[...])
        y_ref[r0:r0 + rows, :] = y.astype(y_ref.dtype)
    for h in range(SSD_HEADS):
        h_scr[h] = hs[h]

    @pl.when(c == pl.num_programs(1) - 1)
    def _():
        for h in range(SSD_HEADS):
            st_ref[0, h] = hs[h]


def _ssd_vec_specs():
    return [_const_spec((SSD_CONV, SSD_CONV_DIM)), _const_spec((1, SSD_CONV_DIM)), _const_spec((1, LANES)),
            _const_spec((1, LANES)), _const_spec((1, SSD_INNER)), _const_spec((1, SSD_INNER))]


def _ssd_prompt(z, xbc, small, lp, b, l):
    nc = l // SSD_STEP
    rs = lambda w: pl.BlockSpec((SSD_STEP, w), lambda bi, ci: (bi * nc + ci, 0))
    return pl.pallas_call(
        _ssd_prompt_kernel,
        grid=(b, nc),
        in_specs=[rs(SSD_INNER), rs(SSD_CONV_DIM), rs(LANES)] + _ssd_vec_specs(),
        out_specs=[rs(SSD_INNER),
                   pl.BlockSpec((1, SSD_HEADS, SSD_HEAD_DIM, SSD_STATE), lambda bi, ci: (bi, 0, 0, 0))],
        out_shape=[jax.ShapeDtypeStruct((b * l, SSD_INNER), BF16),
                   jax.ShapeDtypeStruct((b, SSD_HEADS, SSD_HEAD_DIM, SSD_STATE), F32)],
        scratch_shapes=[pltpu.VMEM((SSD_HEADS, SSD_HEAD_DIM, SSD_STATE), F32),
                        pltpu.VMEM((SUBLANES + SSD_STEP, SSD_CONV_DIM), F32)],
        compiler_params=_params(("arbitrary", "arbitrary")),
        name="ssd_prompt",
    )(z, xbc, small, lp["conv_w"], lp["conv_b"], lp["dtb"], lp["alog"], lp["dvec"], lp["ssd_nw"])


def _ssd_sample_kernel(z_ref, xbc_ref, small_ref, buf_ref, h0_ref, cw_ref, cb_ref, dtb_ref, alog_ref, dvec_ref,
                       nw_ref, y_ref, st_ref, *, seq):
    rows = xbc_ref.shape[0]
    nseq = rows // seq
    u = xbc_ref[...]
    bufp = buf_ref[...]
    tpos = _iota2((rows, SSD_CONV_DIM), 0) % seq
    conv = u * cw_ref[SSD_CONV - 1:SSD_CONV, :] + cb_ref[...]
    for j in range(1, SSD_CONV):
        uj = pltpu.roll(u, j, axis=0)
        back = (rows - (SSD_CONV - 1 - j)) % rows
        bj = pltpu.roll(bufp, back, axis=0) if back else bufp
        conv = conv + jnp.where(tpos < j, bj, uj) * cw_ref[SSD_CONV - 1 - j:SSD_CONV - j, :]

    qi = _iota2((rows, rows), 0)
    si = _iota2((rows, rows), 1)
    same = (qi // seq) == (si // seq)
    mask = same & (si <= qi)
    seg01 = jnp.where(mask, 1.0, 0.0).astype(BF16)
    last01 = jnp.where(si == (qi // seq) * seq + (seq - 1), 1.0, 0.0).astype(BF16)
    xs, bm, cm, dt, cum, cum_last = _ssd_prepare(conv, small_ref[...], dtb_ref[...], alog_ref[...], seg01, last01)
    cum_t = cum.T
    cum_last_t = cum_last.T
    ydiag, xdt = _ssd_diag(xs, bm, cm, dt, cum, cum_t, mask)
    to_end_t = jnp.exp(cum_last_t - cum_t)
    xdt_t = jnp.concatenate(xdt, axis=1).T
    hpg = SSD_HEADS // SSD_GROUPS
    grows = hpg * SSD_HEAD_DIM
    colseq = _iota2((grows, rows), 1) // seq
    yoff_t = []
    for g in range(SSD_GROUPS):
        cg = cm[:, g * SSD_STATE:(g + 1) * SSD_STATE].astype(BF16)
        bg = bm[:, g * SSD_STATE:(g + 1) * SSD_STATE].astype(BF16)
        xw_t = jnp.concatenate(
            [xdt_t[h * SSD_HEAD_DIM:(h + 1) * SSD_HEAD_DIM, :] * to_end_t[h:h + 1, :]
             for h in range(g * hpg, (g + 1) * hpg)], axis=0)
        acc = jnp.zeros((grows, rows), F32)
        for b in range(nseq):
            h0 = h0_ref[b, g * hpg:(g + 1) * hpg].reshape(grows, SSD_STATE)
            acc = jnp.where(colseq == b, _dot_nt(h0.astype(BF16), cg), acc)
            s_local = _dot(jnp.where(colseq == b, xw_t, 0.0).astype(BF16), bg)
            for hh in range(hpg):
                h = g * hpg + hh
                dec = jnp.exp(cum_last_t[h:h + 1, b * seq:b * seq + 1])
                st_ref[b, h] = (dec * h0[hh * SSD_HEAD_DIM:(hh + 1) * SSD_HEAD_DIM, :]
                                + s_local[hh * SSD_HEAD_DIM:(hh + 1) * SSD_HEAD_DIM, :])
        for hh in range(hpg):
            h = g * hpg + hh
            yoff_t.append(acc[hh * SSD_HEAD_DIM:(hh + 1) * SSD_HEAD_DIM, :] * jnp.exp(cum_t[h:h + 1, :]))
    yoff = jnp.concatenate(yoff_t, axis=0).T
    y = _ssd_finish(jnp.concatenate(ydiag, axis=1) + yoff, xs, z_ref[...], dvec_ref[...], nw_ref[...])
    y_ref[...] = y.astype(y_ref.dtype)


def _ssd_sample(z, xbc, small, bufp, h0_all, layer, lp, b, l):
    rows = SAMPLE_SEQS * l
    rs = lambda w: pl.BlockSpec((rows, w), lambda i: (i, 0))
    st = pl.BlockSpec((SAMPLE_SEQS, SSD_HEADS, SSD_HEAD_DIM, SSD_STATE), lambda i: (i, 0, 0, 0))
    st_in = pl.BlockSpec((None, SAMPLE_SEQS, SSD_HEADS, SSD_HEAD_DIM, SSD_STATE), lambda i: (layer, i, 0, 0, 0))
    return pl.pallas_call(
        functools.partial(_ssd_sample_kernel, seq=l),
        grid=(b // SAMPLE_SEQS,),
        in_specs=[rs(SSD_INNER), rs(SSD_CONV_DIM), rs(LANES), rs(SSD_CONV_DIM), st_in] + _ssd_vec_specs(),
        out_specs=[rs(SSD_INNER), st],
        out_shape=[jax.ShapeDtypeStruct((b * l, SSD_INNER), BF16),
                   jax.ShapeDtypeStruct((b, SSD_HEADS, SSD_HEAD_DIM, SSD_STATE), F32)],
        compiler_params=_params(("arbitrary",)),
        name="ssd_sample",
    )(z, xbc, small, bufp, h0_all, lp["conv_w"], lp["conv_b"], lp["dtb"], lp["alog"], lp["dvec"], lp["ssd_nw"])


def _gla_consts():
    rk = np.arange(GLA_DK)[:, None] // GLA_HEAD_K
    cv = np.arange(GLA_DV)[None, :] // GLA_HEAD_V
    expand = (rk == cv).astype(np.float32)
    rv = np.arange(GLA_DV)[:, None] // GLA_HEAD_V
    seg = (rv == cv).astype(np.float32) / GLA_HEAD_V
    return jnp.asarray(expand, BF16), jnp.asarray(seg, BF16)


def _gla_prepare(gq, gk, small, wg, bg, seg01):
    glin = _dot(small.astype(BF16), wg) + bg
    g = -_softplus(-glin) * (1.0 / GLA_TAU)
    gc = _split3_dot(seg01, g)
    q = gq * (GLA_HEAD_K ** -0.5)
    return q, gk, gc


def _gla_pairwise(q, k, v, gc, expand, diag):
    rows = q.shape[0]
    nb = rows // diag
    q4 = q.reshape(nb, 1, diag, GLA_DK)
    g4 = gc.reshape(nb, 1, diag, GLA_DK)
    k4 = k.reshape(nb, diag, 1, GLA_DK)
    gs4 = gc.reshape(nb, diag, 1, GLA_DK)
    shape = (nb, diag, diag, GLA_DK)
    si = lax.broadcasted_iota(jnp.int32, shape, 1)
    ti = lax.broadcasted_iota(jnp.int32, shape, 2)
    w = jnp.exp(jnp.where(si <= ti, g4 - gs4, NEG))
    m = (q4 * k4 * w).reshape(nb * diag * diag, GLA_DK)
    p = _dot(m.astype(BF16), expand).reshape(nb, diag, diag, GLA_DV)
    o = jnp.sum(p * v.reshape(nb, diag, 1, GLA_DV), axis=1)
    return o.reshape(rows, GLA_DV)


def _gla_finish(o, gr, seg, nw):
    ms = _split2_dot(o * o, seg)
    return o * lax.rsqrt(ms + EPS) * nw * _silu(gr)


def _head_stack(x, head_dim, heads):
    lane = _iota2(x.shape, 1) // head_dim
    return jnp.concatenate([jnp.where(lane == h, x, 0.0) for h in range(heads)], axis=0)


def _gla_prompt_kernel(gq_ref, gk_ref, gv_ref, gr_ref, small_ref, wg_ref, bg_ref, nw_ref, ex_ref, seg_ref,
                       o_ref, st_ref, s_scr):
    c = pl.program_id(1)
    rows = GLA_CHUNK

    @pl.when(c == 0)
    def _():
        s_scr[...] = jnp.zeros_like(s_scr)

    qi = _iota2((rows, rows), 0)
    si = _iota2((rows, rows), 1)
    seg01 = jnp.where(si <= qi, 1.0, 0.0).astype(BF16)
    bd = (_iota2((GLA_DK, GLA_DV), 0) // GLA_HEAD_K) == (_iota2((GLA_DK, GLA_DV), 1) // GLA_HEAD_V)
    s_all = s_scr[...]
    nch = gq_ref.shape[0] // rows
    prep = []
    for ci in range(nch):
        r0 = ci * rows
        q, k, gc = _gla_prepare(gq_ref[r0:r0 + rows, :], gk_ref[r0:r0 + rows, :], small_ref[r0:r0 + rows, :],
                                wg_ref[...], bg_ref[...], seg01)
        v = gv_ref[r0:r0 + rows, :]
        prep.append((q, k, gc, v, v.astype(BF16)))
    levels = []
    half = rows // 2
    while half >= GLA_DIAG:
        levels.append(half)
        half //= 2
    atts = {}
    for ci, (q, k, gc, v, vb) in enumerate(prep):
        for half in levels:
            for blk in range(rows // (2 * half)):
                s0 = blk * 2 * half
                t0 = s0 + half
                ref = gc[t0 - 1:t0, :]
                qs = q[t0:t0 + half, :] * jnp.exp(gc[t0:t0 + half, :] - ref)
                ks = k[s0:t0, :] * jnp.exp(ref - gc[s0:t0, :])
                atts[ci, half, blk] = _dot_nt(_head_stack(qs, GLA_HEAD_K, GLA_HEADS).astype(BF16),
                                              ks.astype(BF16)).astype(BF16)
    outs = [_gla_pairwise(q, k, v, gc, ex_ref[...], GLA_DIAG) for q, k, gc, v, vb in prep]
    for ci, (q, k, gc, v, vb) in enumerate(prep):
        for half in levels:
            vlane = _iota2((half, GLA_DV), 1) // GLA_HEAD_V
            pieces = []
            for blk in range(rows // (2 * half)):
                s0 = blk * 2 * half
                pv = _dot(atts[ci, half, blk], vb[s0:s0 + half, :])
                ot = jnp.zeros((half, GLA_DV), F32)
                for h in range(GLA_HEADS):
                    ot = jnp.where(vlane == h, pv[h * half:(h + 1) * half, :], ot)
                pieces += [jnp.zeros((half, GLA_DV), F32), ot]
            outs[ci] = outs[ci] + jnp.concatenate(pieces, axis=0)
    for ci, (q, k, gc, v, vb) in enumerate(prep):
        outs[ci] = outs[ci] + _dot((q * jnp.exp(gc)).astype(BF16), s_all.astype(BF16))
        gc_t = gc.T
        dcol = gc_t[:, rows - 1:rows]
        kd_t = k.T * jnp.exp(dcol - gc_t)
        upd = _dot(kd_t.astype(BF16), vb)
        s_all = jnp.exp(dcol) * s_all + jnp.where(bd, upd, 0.0)
    o_ref[...] = _gla_finish(jnp.concatenate(outs, axis=0), gr_ref[...], seg_ref[...],
                             nw_ref[...]).astype(o_ref.dtype)
    s_scr[...] = s_all

    @pl.when(c == pl.num_programs(1) - 1)
    def _():
        st_ref[0] = s_all


def _gla_vec_specs():
    return [_const_spec((LANES, GLA_DK)), _const_spec((1, GLA_DK)), _const_spec((1, GLA_DV)),
            _const_spec((GLA_DK, GLA_DV)), _const_spec((GLA_DV, GLA_DV))]


def _gla_prompt(gq, gk, gv, gr, small, lp, b, l):
    nc = l // GLA_STEP
    rs = lambda w: pl.BlockSpec((GLA_STEP, w), lambda bi, ci: (bi * nc + ci, 0))
    expand, seg = _gla_consts()
    return pl.pallas_call(
        _gla_prompt_kernel,
        grid=(b, nc),
        in_specs=[rs(GLA_DK), rs(GLA_DK), rs(GLA_DV), rs(GLA_DV), rs(LANES)] + _gla_vec_specs(),
        out_specs=[rs(GLA_DV), pl.BlockSpec((1, GLA_DK, GLA_DV), lambda bi, ci: (bi, 0, 0))],
        out_shape=[jax.ShapeDtypeStruct((b * l, GLA_DV), BF16),
                   jax.ShapeDtypeStruct((b, GLA_DK, GLA_DV), F32)],
        scratch_shapes=[pltpu.VMEM((GLA_DK, GLA_DV), F32)],
        compiler_params=_params(("arbitrary", "arbitrary")),
        name="gla_prompt",
    )(gq, gk, gv, gr, small, lp["gla_wg"], lp["gla_bg"], lp["gla_nw"], expand, seg)


def _gla_sample_kernel(gq_ref, gk_ref, gv_ref, gr_ref, small_ref, s0_ref, wg_ref, bg_ref, nw_ref, ex_ref, seg_ref,
                       o_ref, st_ref, *, seq):
    rows = gq_ref.shape[0]
    nseq = rows // seq
    qi = _iota2((rows, rows), 0)
    si = _iota2((rows, rows), 1)
    seg01 = jnp.where(((qi // seq) == (si // seq)) & (si <= qi), 1.0, 0.0).astype(BF16)
    last01 = jnp.where(si == (qi // seq) * seq + (seq - 1), 1.0, 0.0).astype(BF16)
    q, k, gc = _gla_prepare(gq_ref[...], gk_ref[...], small_ref[...], wg_ref[...], bg_ref[...], seg01)
    v = gv_ref[...]
    vb = v.astype(BF16)
    gc_last = _split3_dot(last01, gc)
    qg = (q * jnp.exp(gc)).astype(BF16)
    kd_t = (k * jnp.exp(gc_last - gc)).T
    dec_t = jnp.exp(gc_last).T
    colseq = _iota2((GLA_DK, rows), 1) // seq
    rowseq = _iota2((rows, GLA_DV), 0) // seq
    bd = (_iota2((GLA_DK, GLA_DV), 0) // GLA_HEAD_K) == (_iota2((GLA_DK, GLA_DV), 1) // GLA_HEAD_V)
    o = _gla_pairwise(q, k, v, gc, ex_ref[...], seq)
    for b in range(nseq):
        s0 = s0_ref[b]
        o = o + jnp.where(rowseq == b, _dot(qg, s0.astype(BF16)), 0.0)
        upd = _dot(jnp.where(colseq == b, kd_t, 0.0).astype(BF16), vb)
        st_ref[b] = dec_t[:, b * seq:b * seq + 1] * s0 + jnp.where(bd, upd, 0.0)
    o_ref[...] = _gla_finish(o, gr_ref[...], seg_ref[...], nw_ref[...]).astype(o_ref.dtype)


def _gla_sample(gq, gk, gv, gr, small, s0, lp, b, l):
    rows = SAMPLE_SEQS * l
    rs = lambda w: pl.BlockSpec((rows, w), lambda i: (i, 0))
    st = pl.BlockSpec((SAMPLE_SEQS, GLA_DK, GLA_DV), lambda i: (i, 0, 0))
    expand, seg = _gla_consts()
    return pl.pallas_call(
        functools.partial(_gla_sample_kernel, seq=l),
        grid=(b // SAMPLE_SEQS,),
        in_specs=[rs(GLA_DK), rs(GLA_DK), rs(GLA_DV), rs(GLA_DV), rs(LANES), st] + _gla_vec_specs(),
        out_specs=[rs(GLA_DV), st],
        out_shape=[jax.ShapeDtypeStruct((b * l, GLA_DV), BF16),
                   jax.ShapeDtypeStruct((b, GLA_DK, GLA_DV), F32)],
        compiler_params=_params(("arbitrary",)),
        name="gla_sample",
    )(gq, gk, gv, gr, small, s0, lp["gla_wg"], lp["gla_bg"], lp["gla_nw"], expand, seg)


def _gla_state_expand(s):
    b = s.shape[0]
    eye = jnp.eye(GLA_HEADS, dtype=s.dtype)
    return (s[:, :, :, None, :] * eye[None, :, None, :, None]).reshape(b, GLA_DK, GLA_DV)


def _gla_state_extract(s):
    b = s.shape[0]
    s5 = s.reshape(b, GLA_HEADS, GLA_HEAD_K, GLA_HEADS, GLA_HEAD_V)
    return jnp.stack([s5[:, h, :, h, :] for h in range(GLA_HEADS)], axis=1)


def _att_window(qs, ks, vs, valids, prev):
    lane_half = _iota2(qs[0].shape, 1) // ATT_HEAD_DIM
    heads = [(p, half) for p in range(len(qs)) for half in range(2)]
    scores = [_dot_nt(jnp.where(lane_half == half, qs[p], 0.0).astype(BF16), ks[p]) for p, half in heads]
    probs, stats = [], []
    for (p, half), s in zip(heads, scores):
        m_prev, l_prev, _ = prev[p]
        s = jnp.where(valids[p], s, NEG)
        smax = jnp.max(s, axis=-1, keepdims=True)
        mn = jnp.broadcast_to(smax, qs[p].shape) if m_prev is None else jnp.maximum(m_prev[half], smax)
        pr = jnp.exp2(s - jnp.concatenate([mn] * (s.shape[1] // LANES), axis=1))
        psum = jnp.sum(pr, axis=-1, keepdims=True)
        if m_prev is None:
            alpha, ln = None, jnp.broadcast_to(psum, qs[p].shape)
        else:
            alpha = jnp.exp2(m_prev[half] - mn)
            ln = l_prev[half] * alpha + psum
        probs.append(pr.astype(BF16))
        stats.append((mn, ln, alpha))
    pvs = [_dot(pr, vs[p]) for (p, half), pr in zip(heads, probs)]
    new = []
    for p in range(len(qs)):
        a_prev = prev[p][2]
        (m0, l0, al0), (m1, l1, al1) = stats[2 * p], stats[2 * p + 1]
        a0 = pvs[2 * p] if al0 is None else a_prev * al0 + pvs[2 * p]
        a1 = pvs[2 * p + 1] if al1 is None else a_prev * al1 + pvs[2 * p + 1]
        new.append(([m0, m1], [l0, l1], jnp.where(lane_half == 0, a0, a1)))
    return new


def _att_prompt_kernel(q_ref, k_ref, v_ref, out_ref, acc_scr, m_scr, l_scr):
    tq = ATT_BLOCK
    sup = out_ref.shape[1]
    base = pl.program_id(1) * sup
    rel = tq + _iota2((tq, 2 * tq), 0) - _iota2((tq, 2 * tq), 1)
    band = (rel >= 0) & (rel <= ATT_KEYS - 1)
    in_cur = _iota2((tq, 2 * tq), 1) >= tq
    strides = sorted((d for _, d in DILATION_PATTERNS), reverse=True)
    for idx, d in enumerate(strides):
        first, last = idx == 0, idx == len(strides) - 1

        def rows(start, d=d):
            return pl.ds(start, tq, stride=d) if d > 1 else pl.ds(start, tq)

        def body(it, carry, d=d, first=first, last=last, rows=rows):
            npair = ATT_HEADS // 2
            locs, qs, ks, vs, valids, prev = [], [], [], [], [], []
            for j in range(ATT_UNROLL):
                sb = it * ATT_UNROLL + j
                if d > 1:
                    r = sb % d
                    mi = sb // d
                    loc = r + d * tq * mi
                    start_q = base + loc
                else:
                    mi = sb
                    loc = pl.multiple_of(sb * tq, tq)
                    start_q = pl.multiple_of(base + loc, tq)
                mglob = base // (d * tq) + mi
                start_p = jnp.where(mglob == 0, start_q, start_q - d * tq)
                valid = band & ((mglob > 0) | in_cur)
                locs.append(loc)
                for p in range(npair):
                    if first:
                        prev.append((None, None, None))
                    else:
                        prev.append(([m_scr[2 * p + hf, rows(loc), :] for hf in range(2)],
                                     [l_scr[2 * p + hf, rows(loc), :] for hf in range(2)],
                                     acc_scr[p, rows(loc), :]))
                    qs.append((q_ref[p, rows(start_q), :] * (ATT_HEAD_DIM ** -0.5 * math.log2(math.e))).astype(BF16))
                    ks.append(jnp.concatenate([k_ref[p, rows(start_p), :], k_ref[p, rows(start_q), :]],
                                              axis=0).astype(BF16))
                    vs.append(jnp.concatenate([v_ref[p, rows(start_p), :], v_ref[p, rows(start_q), :]],
                                              axis=0).astype(BF16))
                    valids.append(valid)
            new = _att_window(qs, ks, vs, valids, prev)
            lane_half = _iota2((tq, LANES), 1) // ATT_HEAD_DIM
            for j, loc in enumerate(locs):
                for p in range(npair):
                    m_new, l_new, a_new = new[j * npair + p]
                    if last:
                        den = jnp.where(lane_half == 0, l_new[0], l_new[1])
                        out_ref[p, rows(loc), :] = (a_new / den).astype(out_ref.dtype)
                    else:
                        for hf in range(2):
                            m_scr[2 * p + hf, rows(loc), :] = m_new[hf]
                            l_scr[2 * p + hf, rows(loc), :] = l_new[hf]
                        acc_scr[p, rows(loc), :] = a_new
            return carry

        lax.fori_loop(0, sup // (tq * ATT_UNROLL), body, 0)


def _att_prompt(aq, ak, av, b, l):
    npair = ATT_HEADS // 2
    sup = ATT_BLOCK * max(d for _, d in DILATION_PATTERNS)
    assert l % sup == 0
    seq = pl.BlockSpec((npair, l, LANES), lambda bi, j: (0, bi, 0), pipeline_mode=pl.Buffered(1))
    return pl.pallas_call(
        _att_prompt_kernel,
        grid=(b, l // sup),
        in_specs=[seq, seq, seq],
        out_specs=pl.BlockSpec((npair, sup, LANES), lambda bi, j: (0, bi * (l // sup) + j, 0)),
        out_shape=jax.ShapeDtypeStruct((npair, b * l, LANES), BF16),
        scratch_shapes=[pltpu.VMEM((npair, sup, LANES), F32), pltpu.VMEM((ATT_HEADS, sup, LANES), F32),
                        pltpu.VMEM((ATT_HEADS, sup, LANES), F32)],
        compiler_params=_params(("arbitrary", "arbitrary")),
        name="att_prompt",
    )(aq, ak, av)


def _att_counts(seq, nbuf):
    qpos = nbuf + np.arange(seq)[:, None]
    kpos = np.arange(nbuf + seq)[None, :]
    delta = qpos - kpos
    cnt = np.zeros(delta.shape, np.float32)
    for window, stride in DILATION_PATTERNS:
        cnt += ((delta >= 0) & (delta % stride == 0) & (delta <= window)).astype(np.float32)
    cnt = np.tile(cnt, (2, 1))
    new = np.zeros((2 * seq, LANES), np.float32)
    new[:, :seq] = cnt[:, nbuf:]
    return jnp.asarray(cnt[:, :nbuf]), jnp.asarray(new)


def _att_sample_kernel(q_ref, kn_ref, vn_ref, kc_ref, vc_ref, cc_ref, cn_ref, out_ref):
    nseq = kc_ref.shape[0]
    seq = q_ref.shape[1] // nseq
    cc = cc_ref[...]
    cn = cn_ref[...]
    pad = jnp.zeros((LANES - seq, LANES), F32)
    lane_half = _iota2((seq, LANES), 1) // ATT_HEAD_DIM
    for b in range(nseq):
        rows = slice(b * seq, (b + 1) * seq)
        for p in range(ATT_HEADS // 2):
            q = q_ref[p, rows, :] * (ATT_HEAD_DIM ** -0.5)
            q2 = jnp.concatenate([jnp.where(lane_half == 0, q, 0.0), jnp.where(lane_half == 1, q, 0.0)],
                                 axis=0).astype(BF16)
            kt = kc_ref[b, 2 * p:2 * p + 2].reshape(LANES, -1).astype(BF16)
            vt = vc_ref[b, 2 * p:2 * p + 2].reshape(LANES, -1).astype(BF16)
            kn = jnp.concatenate([kn_ref[p, rows, :], pad], axis=0).astype(BF16)
            vn = jnp.concatenate([vn_ref[p, rows, :], pad], axis=0).astype(BF16)
            sc = jnp.where(cc > 0, _dot(q2, kt), NEG)
            sn = jnp.where(cn > 0, _dot_nt(q2, kn), NEG)
            m = jnp.maximum(jnp.max(sc, axis=-1, keepdims=True), jnp.max(sn, axis=-1, keepdims=True))
            pc = cc * jnp.exp(sc - m)
            pn = cn * jnp.exp(sn - m)
            den = jnp.sum(pc, axis=-1, keepdims=True) + jnp.sum(pn, axis=-1, keepdims=True)
            o = (_dot_nt(pc.astype(BF16), vt) + _dot(pn.astype(BF16), vn)) / den
            out_ref[p, rows, :] = jnp.where(lane_half == 0, o[0:seq, :], o[seq:2 * seq, :]).astype(out_ref.dtype)


def _att_sample(aq, ak, av, kcache_t, vcache_t, layer, b, l):
    npair = ATT_HEADS // 2
    nbuf = kcache_t.shape[-1]
    cc, cn = _att_counts(l, nbuf)
    ns = ATT_SAMPLE_SEQS
    assert b % ns == 0
    new = pl.BlockSpec((npair, ns * l, LANES), lambda i: (0, i, 0))
    cache = pl.BlockSpec((None, ns, ATT_HEADS, ATT_HEAD_DIM, nbuf), lambda i: (layer, i, 0, 0, 0))
    return pl.pallas_call(
        _att_sample_kernel,
        grid=(b // ns,),
        in_specs=[new, new, new, cache, cache, _const_spec((2 * l, nbuf)), _const_spec((2 * l, LANES))],
        out_specs=new,
        out_shape=jax.ShapeDtypeStruct((npair, b * l, LANES), BF16),
        compiler_params=_params(("arbitrary",)),
        name="att_sample",
    )(aq, ak, av, kcache_t, vcache_t, cc, cn)


def _pad_lanes(v, width, offset=0):
    out = jnp.zeros((1, width), F32)
    return out.at[0, offset:offset + v.shape[0]].set(v.astype(F32))


def _wt_kernel(w_ref, dt_ref, glr_ref, o_ref):
    g = pl.program_id(0)
    last = pl.num_programs(0) - 1
    depth = o_ref.shape[0]

    @pl.when(g < last)
    def _():
        for l in range(depth):
            o_ref[l] = w_ref[:, l, :].T.astype(o_ref.dtype)

    @pl.when(g == last)
    def _():
        row = _iota2((SUBLANES, D_MODEL), 0)
        pad = jnp.zeros((LANES - SMALL_GLR_OFF - GLA_GATE_RANK, D_MODEL), F32)
        for l in range(depth):
            dt = jnp.where(row < SSD_HEADS, dt_ref[:, l, :], 0.0)
            o_ref[l] = jnp.concatenate([dt, glr_ref[:, l, :], pad], axis=0).T.astype(o_ref.dtype)


def _proj_weights(w_in):
    depth = w_in.shape[0]
    assert SMALL_DT_OFF == 0 and SMALL_GLR_OFF == SUBLANES and SSD_HEADS <= SUBLANES
    wt = w_in.transpose(2, 0, 1)
    offs = dict(zip(("z", "xbc", "dt", "gq", "gk", "gv", "gr", "glr", "aq", "ak", "av"),
                    np.concatenate([[0], np.cumsum(IN_SPLITS)])[:-1]))
    src = [int(offs[n]) + i * LANES for n, w in PROJ_GROUPS if n != "small" for i in range(w // LANES)]
    steps = [(g, src[g] - src[g - 1] - LANES) for g in range(1, len(src)) if src[g] - src[g - 1] != LANES]
    assert src[0] == 0 and PROJ_GROUPS[-1][0] == "small"

    def w_map(g):
        off = LANES * g
        for g0, delta in steps:
            off = off + jnp.where(g >= g0, delta, 0)
        return (jnp.where(g < len(src), off, 0), 0, 0)

    el = lambda n: (pl.Element(n), pl.Element(depth), pl.Element(D_MODEL))
    return pl.pallas_call(
        _wt_kernel,
        grid=(len(src) + 1,),
        in_specs=[pl.BlockSpec(el(LANES), w_map),
                  pl.BlockSpec(el(SUBLANES), lambda g: (int(offs["dt"]), 0, 0)),
                  pl.BlockSpec(el(GLA_GATE_RANK), lambda g: (int(offs["glr"]), 0, 0))],
        out_specs=pl.BlockSpec((depth, D_MODEL, LANES), lambda g: (0, 0, g)),
        out_shape=jax.ShapeDtypeStruct((depth, D_MODEL, PROJ_WIDTH), BF16),
        compiler_params=_params(("arbitrary",)),
        name="proj_weights",
    )(wt, wt, wt)


def _layer_params(l, ssd_conv_w, ssd_conv_b, ssd_dt_bias, ssd_a_log, ssd_d, ssd_norm_w,
                  gla_w_gate, gla_b_gate, gla_norm_w, norm_w):
    wg = jnp.zeros((LANES, GLA_DK), F32).at[SMALL_GLR_OFF:SMALL_GLR_OFF + GLA_GATE_RANK, :].set(gla_w_gate[l])
    return dict(
        conv_w=ssd_conv_w[l], conv_b=ssd_conv_b[l].reshape(1, SSD_CONV_DIM),
        dtb=_pad_lanes(ssd_dt_bias[l], LANES, SMALL_DT_OFF), alog=_pad_lanes(ssd_a_log[l], LANES, SMALL_DT_OFF),
        dvec=jnp.repeat(ssd_d[l].astype(F32), SSD_HEAD_DIM).reshape(1, SSD_INNER),
        ssd_nw=ssd_norm_w[l].reshape(1, SSD_INNER),
        gla_wg=wg.astype(BF16), gla_bg=gla_b_gate[l].reshape(1, GLA_DK),
        gla_nw=jnp.tile(gla_norm_w[l], GLA_HEADS).reshape(1, GLA_DV),
        norm_w=norm_w[l],
    )


def _trunk(x, mods, layers, stacked, norm_f, states, sample):
    b, l, _ = x.shape
    keep = min(ATT_MAX_WINDOW, l)
    names = [n for n, _ in PROJ_GROUPS]
    outs = ([], [], [], [], [])
    for li, lp in enumerate(layers):
        mod = mods[li]
        x = _ffn(x, mod, 0, lp["norm_w"][0], stacked["ffn1_in"], stacked["ffn1_out"], li)
        if sample:
            proj = dict(zip(names, _inproj(x, mod, lp["norm_w"][1], stacked["w_in_p"], li)))
            st_ssd, st_conv, st_gla, kcache, vcache = states
            bufp = jnp.pad(st_conv[li], ((0, 0), (0, l - (SSD_CONV - 1)), (0, 0))).reshape(b * l, SSD_CONV_DIM)
            y, ssd_new = _ssd_sample(proj["z"], proj["xbc"], proj["small"], bufp, st_ssd, li, lp, b, l)
            o, gla_new = _gla_sample(proj["gq"], proj["gk"], proj["gv"], proj["gr"], proj["small"],
                                     _gla_state_expand(st_gla[li]), lp, b, l)
            att = _att_sample(proj["aq"], proj["ak"], proj["av"], kcache, vcache, li, b, l)
            for acc, name in ((outs[3], "ak"), (outs[4], "av")):
                kv = proj[name].reshape(ATT_HEADS // 2, b, l, 2, ATT_HEAD_DIM)[:, :, l - keep:]
                acc.append(kv.transpose(1, 2, 0, 3, 4).reshape(b, keep, ATT_HEADS, ATT_HEAD_DIM))
        else:
            res = _inproj(x, mod, lp["norm_w"][1], stacked["w_in_p"], li, keep_t=keep)
            proj = dict(zip(names, res))
            y, ssd_new = _ssd_prompt(proj["z"], proj["xbc"], proj["small"], lp, b, l)
            o, gla_new = _gla_prompt(proj["gq"], proj["gk"], proj["gv"], proj["gr"], proj["small"], lp, b, l)
            att = _att_prompt(proj["aq"], proj["ak"], proj["av"], b, l)
            for acc, kv_t in zip((outs[3], outs[4]), res[len(names):]):
                acc.append(kv_t.reshape(b, ATT_HEADS, ATT_HEAD_DIM, keep).transpose(0, 3, 1, 2))
        x = _ffn(x, mod, 6, lp["norm_w"][2], stacked["ffn2_in"], stacked["ffn2_out"], li,
                 premix=(y, o, att, 5, stacked["w_out"]),
                 final_norm=norm_f if li == len(layers) - 1 else None)
        outs[0].append(ssd_new)
        outs[1].append(proj["xbc"].reshape(b, l, SSD_CONV_DIM)[:, l - (SSD_CONV - 1):])
        outs[2].append(_gla_state_extract(gla_new))
    return x, [jnp.stack(a) for a in outs]


def kernel(x_prompt, x_sample, c_prompt, c_sample, state_ssd, state_ssd_conv, state_gla, cache_attn_k, cache_attn_v,
           w_in, w_out, ssd_conv_w, ssd_conv_b, ssd_dt_bias, ssd_a_log, ssd_d, ssd_norm_w,
           gla_w_gate, gla_b_gate, gla_norm_w, norm_w, w_mod, b_mod,
           ffn1_w_in, ffn1_w_out, ffn2_w_in, ffn2_w_out, norm_f):
    bp, bs = x_prompt.shape[0], x_sample.shape[0]
    depth = w_in.shape[0]
    layers = [_layer_params(l, ssd_conv_w, ssd_conv_b, ssd_dt_bias, ssd_a_log, ssd_d, ssd_norm_w,
                            gla_w_gate, gla_b_gate, gla_norm_w, norm_w) for l in range(depth)]
    stacked = dict(w_in_p=_proj_weights(w_in), w_out=w_out.astype(BF16), ffn1_in=ffn1_w_in.astype(BF16), ffn1_out=ffn1_w_out.astype(BF16),
                   ffn2_in=ffn2_w_in.astype(BF16), ffn2_out=ffn2_w_out.astype(BF16))
    npad = -(bp + bs) % SUBLANES
    c_all = jnp.concatenate([c_prompt, c_sample, jnp.zeros((npad, D_MODEL), F32)], axis=0)
    m_all = _modulation(c_all, w_mod.astype(BF16), b_mod)
    mods_p, mods_s = [], []
    for l in range(depth):
        m = m_all[l].reshape(-1, ADALN_MODS, 1, D_MODEL).transpose(1, 0, 2, 3)
        mods_p.append(m[:, :bp])
        mods_s.append(m[:, bp:bp + bs])
    kcache = cache_attn_k.transpose(0, 1, 3, 4, 2)
    vcache = cache_attn_v.transpose(0, 1, 3, 4, 2)
    y_p, (ssd_p, conv_p, gla_p, k_p, v_p) = _trunk(x_prompt, mods_p, layers, stacked, norm_f, None, sample=False)
    y_s, (ssd_s, conv_s, gla_s, k_s, v_s) = _trunk(
        x_sample, mods_s, layers, stacked, norm_f, (state_ssd, state_ssd_conv, state_gla, kcache, vcache),
        sample=True)
    return (y_p, y_s, ssd_p, ssd_s, conv_p, conv_s, gla_p, gla_s, k_p, k_s, v_p, v_s)
```

```python
import functools
import math

import numpy as np
import jax
import jax.numpy as jnp
from jax import lax
from jax.experimental import pallas as pl
from jax.experimental.pallas import tpu as pltpu

F32 = jnp.float32
BF16 = jnp.bfloat16

D_MODEL = 1024
DEPTH = 2
SSD_HEADS = 6
SSD_HEAD_DIM = 64
SSD_INNER = SSD_HEADS * SSD_HEAD_DIM
SSD_GROUPS = 2
SSD_STATE = 128
SSD_CONV = 4
SSD_CONV_DIM = SSD_INNER + 2 * SSD_GROUPS * SSD_STATE
GLA_HEADS = 4
GLA_HEAD_K = 32
GLA_HEAD_V = 64
GLA_DK = GLA_HEADS * GLA_HEAD_K
GLA_DV = GLA_HEADS * GLA_HEAD_V
GLA_GATE_RANK = 16
GLA_TAU = 16.0
ATT_HEADS = 6
ATT_HEAD_DIM = 64
ATT_DIM = ATT_HEADS * ATT_HEAD_DIM
DILATION_PATTERNS = ((128, 1), (512, 4), (2048, 16))
ATT_MAX_WINDOW = 2048
ATT_KEYS = 129
D_MIX = SSD_INNER + GLA_DV + ATT_DIM
IN_SPLITS = (SSD_INNER, SSD_CONV_DIM, SSD_HEADS, GLA_DK, GLA_DK, GLA_DV, GLA_DV, GLA_GATE_RANK,
             ATT_DIM, ATT_DIM, ATT_DIM)
D_FF = 2816
ADALN_MODS = 9
FFN_RES = 0.5
EPS = 1e-6

LANES = 128
SUBLANES = 8
VMEM_LIMIT = 56 * 1024 * 1024

PROJ_GROUPS = (("z", SSD_INNER), ("xbc", SSD_CONV_DIM), ("gq", GLA_DK), ("gk", GLA_DK), ("gv", GLA_DV),
               ("gr", GLA_DV), ("aq", ATT_DIM), ("ak", ATT_DIM), ("av", ATT_DIM), ("small", LANES))
PROJ_WIDTH = sum(w for _, w in PROJ_GROUPS)
PAIR_MAJOR = ("aq", "ak", "av")
KV_T = ("ak", "av")
SMALL_DT_OFF = 0
SMALL_GLR_OFF = 8

ROW_TILE = 512
FF_CHUNK = 256
SSD_CHUNK = 128
SSD_STEP = 512
GLA_CHUNK = 128
GLA_STEP = 512
GLA_DIAG = 16
SAMPLE_SEQS = 16
ATT_BLOCK = 128
ATT_SAMPLE_SEQS = 2
ATT_UNROLL = 2
NEG = -1e30


def _dot(a, b):
    return jnp.dot(a, b, preferred_element_type=F32)


def _dot_nt(a, b):
    return lax.dot_general(a, b, (((1,), (1,)), ((), ())), preferred_element_type=F32)


def _sigmoid(x):
    return 1.0 / (1.0 + jnp.exp(-x))


def _silu(x):
    return x * _sigmoid(x)


def _softplus(x):
    return jnp.maximum(x, 0.0) + jnp.log1p(jnp.exp(-jnp.abs(x)))


def _split3_dot(m01, a):
    a1 = a.astype(BF16)
    r1 = a - a1.astype(F32)
    a2 = r1.astype(BF16)
    a3 = (r1 - a2.astype(F32)).astype(BF16)
    return _dot(m01, a1) + _dot(m01, a2) + _dot(m01, a3)


def _split2_dot(a, m01):
    a1 = a.astype(BF16)
    a2 = (a - a1.astype(F32)).astype(BF16)
    return _dot(a1, m01) + _dot(a2, m01)


def _rms_mod(x, nw, shift, scale):
    ms = jnp.mean(x * x, axis=-1, keepdims=True)
    y = x * lax.rsqrt(ms + EPS) * nw
    return y * (1.0 + scale) + shift


def _iota2(shape, axis):
    return lax.broadcasted_iota(jnp.int32, shape, axis)


def _params(sem):
    return pltpu.CompilerParams(dimension_semantics=sem, vmem_limit_bytes=VMEM_LIMIT)


def _const_spec(shape, layer=None):
    nd = len(shape)
    if layer is None:
        return pl.BlockSpec(shape, lambda *_: (0,) * nd, pipeline_mode=pl.Buffered(1))
    return pl.BlockSpec((None,) + tuple(shape), lambda *_: (layer,) + (0,) * nd, pipeline_mode=pl.Buffered(1))


def _mod_kernel(c_ref, w_ref, b_ref, o_ref):
    c = c_ref[...]
    o_ref[...] = _dot(_silu(c).astype(BF16), w_ref[...]) + b_ref[...]


def _modulation(c_all, w_mod, b_mod):
    n, d = c_all.shape
    depth, _, nout = w_mod.shape
    tn = D_MODEL
    return pl.pallas_call(
        _mod_kernel,
        grid=(depth, nout // tn),
        in_specs=[pl.BlockSpec((n, d), lambda l, j: (0, 0)),
                  pl.BlockSpec((None, d, tn), lambda l, j: (l, 0, j)),
                  pl.BlockSpec((None, 1, tn), lambda l, j: (l, 0, j))],
        out_specs=pl.BlockSpec((None, n, tn), lambda l, j: (l, 0, j)),
        out_shape=jax.ShapeDtypeStruct((depth, n, nout), F32),
        compiler_params=_params(("arbitrary", "arbitrary")),
        name="adaln_mod",
    )(c_all, w_mod, b_mod.reshape(depth, 1, nout))


def _row_tiling(b, l):
    if l >= ROW_TILE:
        assert l % ROW_TILE == 0
        return 1, ROW_TILE
    assert ROW_TILE % l == 0 and b % (ROW_TILE // l) == 0
    return ROW_TILE // l, l


def _x_spec(bb, ll, nlb):
    return pl.BlockSpec((bb, ll, D_MODEL), lambda i: (i // nlb, i % nlb, 0))


def _mod_spec(k, bb, nlb):
    return pl.BlockSpec((1, bb, 1, D_MODEL), lambda i: (k, i // nlb, 0, 0))


def _rows_spec(r, width):
    return pl.BlockSpec((r, width), lambda i: (i, 0))


def _pair_spec(r, width):
    return pl.BlockSpec((width // LANES, r, LANES), lambda i: (0, i, 0))


def _ffn_kernel(*refs, premix, final):
    refs = list(refs)
    x_ref = refs.pop(0)
    if premix:
        y_ref, o_ref, a_ref, g2_ref, wo_ref = refs[:5]
        refs = refs[5:]
    sh_ref, sc_ref, gt_ref, nw_ref, win_ref, wout_ref = refs[:6]
    refs = refs[6:]
    if final:
        nf_ref = refs.pop(0)
    out_ref, act_ref = refs
    bb, ll, d = x_ref.shape
    r = bb * ll
    x = x_ref[...]
    if premix:
        mix = (_dot(y_ref[...], wo_ref[0:SSD_INNER, :])
               + _dot(o_ref[...], wo_ref[SSD_INNER:SSD_INNER + GLA_DV, :])
               + sum(_dot(a_ref[p], wo_ref[SSD_INNER + GLA_DV + p * LANES:SSD_INNER + GLA_DV + (p + 1) * LANES, :])
                     for p in range(ATT_DIM // LANES)))
        x = x + g2_ref[0] * mix.reshape(bb, ll, d)
    h = _rms_mod(x, nw_ref[...], sh_ref[0], sc_ref[0]).reshape(r, d).astype(BF16)
    for c in range(D_FF // FF_CHUNK):
        g = _dot(h, win_ref[:, c * FF_CHUNK:(c + 1) * FF_CHUNK])
        u = _dot(h, win_ref[:, D_FF + c * FF_CHUNK:D_FF + (c + 1) * FF_CHUNK])
        act_ref[:, c * FF_CHUNK:(c + 1) * FF_CHUNK] = (_silu(g) * u).astype(BF16)
    y = _dot(act_ref[...], wout_ref[...])
    x = x + FFN_RES * gt_ref[0] * y.reshape(bb, ll, d)
    if final:
        ms = jnp.mean(x * x, axis=-1, keepdims=True)
        x = x * lax.rsqrt(ms + EPS) * nf_ref[...]
    out_ref[...] = x


def _ffn(x, mod, mod_base, norm_w, w_in, w_out, layer, premix=None, final_norm=None):
    b, l, d = x.shape
    bb, ll = _row_tiling(b, l)
    nlb = l // ll
    r = bb * ll
    nsteps = (b // bb) * nlb
    args, specs = [x], [_x_spec(bb, ll, nlb)]
    if premix is not None:
        y, o, a, gate_row, wo = premix
        args += [y, o, a, mod, wo]
        specs += [_rows_spec(r, SSD_INNER), _rows_spec(r, GLA_DV), _pair_spec(r, ATT_DIM),
                  _mod_spec(gate_row, bb, nlb), _const_spec((D_MIX, d), layer)]
    args += [mod, mod, mod, norm_w.reshape(1, d), w_in, w_out]
    specs += [_mod_spec(mod_base, bb, nlb), _mod_spec(mod_base + 1, bb, nlb), _mod_spec(mod_base + 2, bb, nlb),
              _const_spec((1, d)), _const_spec((d, 2 * D_FF), layer), _const_spec((D_FF, d), layer)]
    if final_norm is not None:
        args.append(final_norm.reshape(1, d))
        specs.append(_const_spec((1, d)))
    return pl.pallas_call(
        functools.partial(_ffn_kernel, premix=premix is not None, final=final_norm is not None),
        grid=(nsteps,),
        in_specs=specs,
        out_specs=_x_spec(bb, ll, nlb),
        out_shape=jax.ShapeDtypeStruct((b, l, d), F32),
        scratch_shapes=[pltpu.VMEM((r, D_FF), BF16)],
        compiler_params=_params(("arbitrary",)),
        name="ffn",
    )(*args)


def _inproj_kernel(x_ref, sh_ref, sc_ref, nw_ref, w_ref, *out_refs, first_kept, n_alias):
    out_refs = out_refs[n_alias:]
    bb, ll, d = x_ref.shape
    h = _rms_mod(x_ref[...], nw_ref[...], sh_ref[0], sc_ref[0]).reshape(bb * ll, d).astype(BF16)
    results, start, run = {}, 0, []
    for name, width in PROJ_GROUPS:
        run.append((name, width))
        total = sum(w for _, w in run)
        if total % (2 * LANES) == 0:
            big = _dot(h, w_ref[:, start:start + total])
            o = 0
            for n, w in run:
                results[n] = big[:, o:o + w]
                o += w
            start, run = start + total, []
    assert not run
    for ref, (name, width) in zip(out_refs, PROJ_GROUPS):
        res = results[name]
        if name in PAIR_MAJOR:
            for p in range(width // LANES):
                ref[p] = res[:, p * LANES:(p + 1) * LANES]
        else:
            ref[...] = res
        if first_kept is not None and name in KV_T:
            t_ref = out_refs[len(PROJ_GROUPS) + KV_T.index(name)]

            @pl.when(pl.program_id(0) % first_kept[1] >= first_kept[0])
            def _(t_ref=t_ref, res=res):
                rt = res.T
                if len(t_ref.shape) == 4:
                    for lyr in range(t_ref.shape[0]):
                        t_ref[lyr, 0] = rt
                else:
                    t_ref[0] = rt


def _inproj(x, mod, norm_w, w_in_p, layer, keep_t=None, kv_prev=None):
    b, l, d = x.shape
    bb, ll = _row_tiling(b, l)
    nlb = l // ll
    r = bb * ll
    out_specs = [_pair_spec(r, w) if n in PAIR_MAJOR else _rows_spec(r, w) for n, w in PROJ_GROUPS]
    out_shape = [jax.ShapeDtypeStruct((w // LANES, b * l, LANES) if n in PAIR_MAJOR else (b * l, w), F32)
                 for n, w in PROJ_GROUPS]
    first_kept = None
    if keep_t is not None:
        assert bb == 1 and keep_t % ll == 0 and keep_t <= l
        skip = nlb - keep_t // ll
        first_kept = (skip, nlb)
        depth = w_in_p.shape[0]
        pos = lambda i: jnp.maximum(i % nlb - skip, 0)
        if kv_prev is None:
            t_spec = pl.BlockSpec((depth, 1, ATT_DIM, ll), lambda i: (0, i // nlb, 0, pos(i)))
        else:
            t_spec = pl.BlockSpec((None, 1, ATT_DIM, ll), lambda i: (layer, i // nlb, 0, pos(i)))
        out_specs += [t_spec] * len(KV_T)
        out_shape += [jax.ShapeDtypeStruct((depth, b, ATT_DIM, keep_t), F32)] * len(KV_T)
    args = [x, mod, mod, norm_w.reshape(1, d), w_in_p]
    in_specs = [_x_spec(bb, ll, nlb), _mod_spec(3, bb, nlb), _mod_spec(4, bb, nlb),
                _const_spec((1, d)), _const_spec((d, PROJ_WIDTH), layer)]
    aliases = {}
    if kv_prev is not None:
        for j, arr in enumerate(kv_prev):
            aliases[len(args)] = len(PROJ_GROUPS) + j
            args.append(arr)
            in_specs.append(pl.BlockSpec(memory_space=pl.ANY))
    return pl.pallas_call(
        functools.partial(_inproj_kernel, first_kept=first_kept, n_alias=len(aliases)),
        grid=((b // bb) * nlb,),
        in_specs=in_specs,
        out_specs=out_specs,
        out_shape=out_shape,
        input_output_aliases=aliases,
        compiler_params=_params(("arbitrary",)),
        name="inproj",
    )(*args)


def _ssd_prepare(conv, small, dtb, alog, seg01, seglast01):
    xc = _silu(conv)
    xs = xc[:, 0:SSD_INNER]
    bm = xc[:, SSD_INNER:SSD_INNER + SSD_GROUPS * SSD_STATE]
    cm = xc[:, SSD_INNER + SSD_GROUPS * SSD_STATE:SSD_CONV_DIM]
    dt = _softplus(small + dtb)
    a = dt * (-jnp.exp(alog))
    cum = _split3_dot(seg01, a)
    if seglast01 is None:
        cum_last = jnp.broadcast_to(cum[cum.shape[0] - 1:, :], cum.shape)
    else:
        cum_last = _split3_dot(seglast01, cum)
    return xs, bm, cm, dt, cum, cum_last


def _ssd_diag(xs, bm, cm, dt, cum, cum_t, mask):
    ydiag, xdt = [], []
    cb = [_dot_nt(cm[:, g * SSD_STATE:(g + 1) * SSD_STATE].astype(BF16),
                  bm[:, g * SSD_STATE:(g + 1) * SSD_STATE].astype(BF16)) for g in range(SSD_GROUPS)]
    for h in range(SSD_HEADS):
        g = h // (SSD_HEADS // SSD_GROUPS)
        diff = cum[:, h:h + 1] - cum_t[h:h + 1, :]
        decay = jnp.exp(jnp.where(mask, diff, NEG))
        xh = xs[:, h * SSD_HEAD_DIM:(h + 1) * SSD_HEAD_DIM] * dt[:, h:h + 1]
        xdt.append(xh)
        ydiag.append(_dot((cb[g] * decay).astype(BF16), xh.astype(BF16)))
    return ydiag, xdt


def _ssd_finish(y, xs, z, dvec, normw):
    y = (y + dvec * xs) * _silu(z)
    sq = y * y
    half = SSD_INNER // SSD_GROUPS
    lane = _iota2(y.shape, 1)
    s0 = jnp.sum(jnp.where(lane < half, sq, 0.0), axis=-1, keepdims=True)
    s1 = jnp.sum(jnp.where(lane >= half, sq, 0.0), axis=-1, keepdims=True)
    ms = jnp.where(lane < half, s0, s1) * (1.0 / half)
    return y * lax.rsqrt(ms + EPS) * normw


def _ssd_prompt_kernel(z_ref, xbc_ref, small_ref, cw_ref, cb_ref, dtb_ref, alog_ref, dvec_ref, nw_ref,
                       y_ref, st_ref, h_scr, tail_scr):
    c = pl.program_id(1)
    step = xbc_ref.shape[0]
    rows = SSD_CHUNK
    hpg = SSD_HEADS // SSD_GROUPS

    @pl.when(c == 0)
    def _():
        h_scr[...] = jnp.zeros_like(h_scr)
        tail_scr[0:SUBLANES, :] = jnp.zeros((SUBLANES, SSD_CONV_DIM), F32)

    u = xbc_ref[...]
    tail_scr[SUBLANES:SUBLANES + step, :] = u
    conv = u * cw_ref[SSD_CONV - 1:SSD_CONV, :] + cb_ref[...]
    for j in range(1, SSD_CONV):
        conv = conv + tail_scr[SUBLANES - j:SUBLANES - j + step, :] * cw_ref[SSD_CONV - 1 - j:SSD_CONV - j, :]
    tail_scr[0:SUBLANES, :] = u[step - SUBLANES:, :]

    qi = _iota2((rows, rows), 0)
    si = _iota2((rows, rows), 1)
    mask = si <= qi
    seg01 = jnp.where(mask, 1.0, 0.0).astype(BF16)
    prep, diag, local = [], [], []
    for r0 in range(0, step, rows):
        prep.append(_ssd_prepare(conv[r0:r0 + rows, :], small_ref[r0:r0 + rows, :], dtb_ref[...], alog_ref[...],
                                 seg01, None))
    for xs, bm, cm, dt, cum, cum_last in prep:
        diag.append(_ssd_diag(xs, bm, cm, dt, cum, cum.T, mask))
    for (xs, bm, cm, dt, cum, cum_last), (ydiag, xdt) in zip(prep, diag):
        to_end_t = jnp.exp(cum_last - cum).T
        xdt_t = jnp.concatenate(xdt, axis=1).T
        s_local = []
        for h in range(SSD_HEADS):
            xw_t = xdt_t[h * SSD_HEAD_DIM:(h + 1) * SSD_HEAD_DIM, :] * to_end_t[h:h + 1, :]
            bg = bm[:, (h // hpg) * SSD_STATE:(h // hpg + 1) * SSD_STATE].astype(BF16)
            s_local.append(_dot(xw_t.astype(BF16), bg))
        local.append(s_local)
    hs = [h_scr[h] for h in range(SSD_HEADS)]
    for ci, ((xs, bm, cm, dt, cum, cum_last), (ydiag, xdt)) in enumerate(zip(prep, diag)):
        ys = []
        for h in range(SSD_HEADS):
            cg = cm[:, (h // hpg) * SSD_STATE:(h // hpg + 1) * SSD_STATE].astype(BF16)
            yoff = _dot_nt(cg, hs[h].astype(BF16)) * jnp.exp(cum[:, h:h + 1])
            ys.append(ydiag[h] + yoff)
            hs[h] = jnp.exp(cum_last[0:1, h:h + 1]) * hs[h] + local[ci][h]
        r0 = ci * rows
        y = _ssd_finish(jnp.concatenate(ys, axis=1), xs, z_ref[r0:r0 + rows, :], dvec_ref[...], nw_ref[...])
        y_ref[r0:r0 + rows, :] = y.astype(y_ref.dtype)
    for h in range(SSD_HEADS):
        h_scr[h] = hs[h]

    @pl.when(c == pl.num_programs(1) - 1)
    def _():
        for h in range(SSD_HEADS):
            st_ref[0, h] = hs[h]


def _ssd_vec_specs():
    return [_const_spec((SSD_CONV, SSD_CONV_DIM)), _const_spec((1, SSD_CONV_DIM)), _const_spec((1, LANES)),
            _const_spec((1, LANES)), _const_spec((1, SSD_INNER)), _const_spec((1, SSD_INNER))]


def _ssd_prompt(z, xbc, small, lp, b, l):
    nc = l // SSD_STEP
    rs = lambda w: pl.BlockSpec((SSD_STEP, w), lambda bi, ci: (bi * nc + ci, 0))
    return pl.pallas_call(
        _ssd_prompt_kernel,
        grid=(b, nc),
        in_specs=[rs(SSD_INNER), rs(SSD_CONV_DIM), rs(LANES)] + _ssd_vec_specs(),
        out_specs=[rs(SSD_INNER),
                   pl.BlockSpec((1, SSD_HEADS, SSD_HEAD_DIM, SSD_STATE), lambda bi, ci: (bi, 0, 0, 0))],
        out_shape=[jax.ShapeDtypeStruct((b * l, SSD_INNER), BF16),
                   jax.ShapeDtypeStruct((b, SSD_HEADS, SSD_HEAD_DIM, SSD_STATE), F32)],
        scratch_shapes=[pltpu.VMEM((SSD_HEADS, SSD_HEAD_DIM, SSD_STATE), F32),
                        pltpu.VMEM((SUBLANES + SSD_STEP, SSD_CONV_DIM), F32)],
        compiler_params=_params(("arbitrary", "arbitrary")),
        name="ssd_prompt",
    )(z, xbc, small, lp["conv_w"], lp["conv_b"], lp["dtb"], lp["alog"], lp["dvec"], lp["ssd_nw"])


def _ssd_sample_kernel(z_ref, xbc_ref, small_ref, buf_ref, h0_ref, cw_ref, cb_ref, dtb_ref, alog_ref, dvec_ref,
                       nw_ref, *rest, seq):
    y_ref, st_ref = rest[-2:]
    st_slabs = [st_ref.at[lyr] for lyr in range(st_ref.shape[0])] if len(st_ref.shape) == 5 else [st_ref]
    rows = xbc_ref.shape[0]
    nseq = rows // seq
    u = xbc_ref[...]
    bufp = buf_ref[...]
    tpos = _iota2((rows, SSD_CONV_DIM), 0) % seq
    conv = u * cw_ref[SSD_CONV - 1:SSD_CONV, :] + cb_ref[...]
    for j in range(1, SSD_CONV):
        uj = pltpu.roll(u, j, axis=0)
        back = (rows - (SSD_CONV - 1 - j)) % rows
        bj = pltpu.roll(bufp, back, axis=0) if back else bufp
        conv = conv + jnp.where(tpos < j, bj, uj) * cw_ref[SSD_CONV - 1 - j:SSD_CONV - j, :]

    qi = _iota2((rows, rows), 0)
    si = _iota2((rows, rows), 1)
    same = (qi // seq) == (si // seq)
    mask = same & (si <= qi)
    seg01 = jnp.where(mask, 1.0, 0.0).astype(BF16)
    last01 = jnp.where(si == (qi // seq) * seq + (seq - 1), 1.0, 0.0).astype(BF16)
    xs, bm, cm, dt, cum, cum_last = _ssd_prepare(conv, small_ref[...], dtb_ref[...], alog_ref[...], seg01, last01)
    cum_t = cum.T
    cum_last_t = cum_last.T
    ydiag, xdt = _ssd_diag(xs, bm, cm, dt, cum, cum_t, mask)
    to_end_t = jnp.exp(cum_last_t - cum_t)
    xdt_t = jnp.concatenate(xdt, axis=1).T
    hpg = SSD_HEADS // SSD_GROUPS
    grows = hpg * SSD_HEAD_DIM
    colseq = _iota2((grows, rows), 1) // seq
    yoff_t = []
    for g in range(SSD_GROUPS):
        cg = cm[:, g * SSD_STATE:(g + 1) * SSD_STATE].astype(BF16)
        bg = bm[:, g * SSD_STATE:(g + 1) * SSD_STATE].astype(BF16)
        xw_t = jnp.concatenate(
            [xdt_t[h * SSD_HEAD_DIM:(h + 1) * SSD_HEAD_DIM, :] * to_end_t[h:h + 1, :]
             for h in range(g * hpg, (g + 1) * hpg)], axis=0)
        acc = jnp.zeros((grows, rows), F32)
        for b in range(nseq):
            h0 = h0_ref[b, g * hpg:(g + 1) * hpg].reshape(grows, SSD_STATE)
            acc = jnp.where(colseq == b, _dot_nt(h0.astype(BF16), cg), acc)
            s_local = _dot(jnp.where(colseq == b, xw_t, 0.0).astype(BF16), bg)
            for hh in range(hpg):
                h = g * hpg + hh
                dec = jnp.exp(cum_last_t[h:h + 1, b * seq:b * seq + 1])
                h_new = (dec * h0[hh * SSD_HEAD_DIM:(hh + 1) * SSD_HEAD_DIM, :]
                         + s_local[hh * SSD_HEAD_DIM:(hh + 1) * SSD_HEAD_DIM, :])
                for slab in st_slabs:
                    slab[b, h] = h_new
        for hh in range(hpg):
            h = g * hpg + hh
            yoff_t.append(acc[hh * SSD_HEAD_DIM:(hh + 1) * SSD_HEAD_DIM, :] * jnp.exp(cum_t[h:h + 1, :]))
    yoff = jnp.concatenate(yoff_t, axis=0).T
    y = _ssd_finish(jnp.concatenate(ydiag, axis=1) + yoff, xs, z_ref[...], dvec_ref[...], nw_ref[...])
    y_ref[...] = y.astype(y_ref.dtype)


def _ssd_sample(z, xbc, small, bufp, h0_all, layer, lp, b, l, new_prev=None):
    depth = h0_all.shape[0]
    rows = SAMPLE_SEQS * l
    rs = lambda w: pl.BlockSpec((rows, w), lambda i: (i, 0))
    tile = (SAMPLE_SEQS, SSD_HEADS, SSD_HEAD_DIM, SSD_STATE)
    st_in = pl.BlockSpec((None,) + tile, lambda i: (layer, i, 0, 0, 0))
    args = [z, xbc, small, bufp, h0_all, lp["conv_w"], lp["conv_b"], lp["dtb"], lp["alog"], lp["dvec"], lp["ssd_nw"]]
    in_specs = [rs(SSD_INNER), rs(SSD_CONV_DIM), rs(LANES), rs(SSD_CONV_DIM), st_in] + _ssd_vec_specs()
    aliases = {}
    if new_prev is None:
        st_out = pl.BlockSpec((depth,) + tile, lambda i: (0, i, 0, 0, 0))
    else:
        st_out = st_in
        aliases[len(args)] = 1
        args.append(new_prev)
        in_specs.append(pl.BlockSpec(memory_space=pl.ANY))
    return pl.pallas_call(
        functools.partial(_ssd_sample_kernel, seq=l),
        grid=(b // SAMPLE_SEQS,),
        in_specs=in_specs,
        out_specs=[rs(SSD_INNER), st_out],
        out_shape=[jax.ShapeDtypeStruct((b * l, SSD_INNER), BF16),
                   jax.ShapeDtypeStruct((depth, b, SSD_HEADS, SSD_HEAD_DIM, SSD_STATE), F32)],
        input_output_aliases=aliases,
        compiler_params=_params(("arbitrary",)),
        name="ssd_sample",
    )(*args)


def _gla_consts():
    rk = np.arange(GLA_DK)[:, None] // GLA_HEAD_K
    cv = np.arange(GLA_DV)[None, :] // GLA_HEAD_V
    expand = (rk == cv).astype(np.float32)
    rv = np.arange(GLA_DV)[:, None] // GLA_HEAD_V
    seg = (rv == cv).astype(np.float32) / GLA_HEAD_V
    return jnp.asarray(expand, BF16), jnp.asarray(seg, BF16)


def _gla_prepare(gq, gk, small, wg, bg, seg01):
    glin = _dot(small.astype(BF16), wg) + bg
    g = -_softplus(-glin) * (1.0 / GLA_TAU)
    gc = _split3_dot(seg01, g)
    q = gq * (GLA_HEAD_K ** -0.5)
    return q, gk, gc


def _gla_pairwise(q, k, v, gc, expand, diag):
    rows = q.shape[0]
    nb = rows // diag
    q4 = q.reshape(nb, 1, diag, GLA_DK)
    g4 = gc.reshape(nb, 1, diag, GLA_DK)
    k4 = k.reshape(nb, diag, 1, GLA_DK)
    gs4 = gc.reshape(nb, diag, 1, GLA_DK)
    shape = (nb, diag, diag, GLA_DK)
    si = lax.broadcasted_iota(jnp.int32, shape, 1)
    ti = lax.broadcasted_iota(jnp.int32, shape, 2)
    w = jnp.exp(jnp.where(si <= ti, g4 - gs4, NEG))
    m = (q4 * k4 * w).reshape(nb * diag * diag, GLA_DK)
    p = _dot(m.astype(BF16), expand).reshape(nb, diag, diag, GLA_DV)
    o = jnp.sum(p * v.reshape(nb, diag, 1, GLA_DV), axis=1)
    return o.reshape(rows, GLA_DV)


def _gla_finish(o, gr, seg, nw):
    ms = _split2_dot(o * o, seg)
    return o * lax.rsqrt(ms + EPS) * nw * _silu(gr)


def _head_stack(x, head_dim, heads):
    lane = _iota2(x.shape, 1) // head_dim
    return jnp.concatenate([jnp.where(lane == h, x, 0.0) for h in range(heads)], axis=0)


def _gla_prompt_kernel(gq_ref, gk_ref, gv_ref, gr_ref, small_ref, wg_ref, bg_ref, nw_ref, ex_ref, seg_ref,
                       o_ref, st_ref, s_scr):
    c = pl.program_id(1)
    rows = GLA_CHUNK

    @pl.when(c == 0)
    def _():
        s_scr[...] = jnp.zeros_like(s_scr)

    qi = _iota2((rows, rows), 0)
    si = _iota2((rows, rows), 1)
    seg01 = jnp.where(si <= qi, 1.0, 0.0).astype(BF16)
    bd = (_iota2((GLA_DK, GLA_DV), 0) // GLA_HEAD_K) == (_iota2((GLA_DK, GLA_DV), 1) // GLA_HEAD_V)
    s_all = s_scr[...]
    nch = gq_ref.shape[0] // rows
    prep = []
    for ci in range(nch):
        r0 = ci * rows
        q, k, gc = _gla_prepare(gq_ref[r0:r0 + rows, :], gk_ref[r0:r0 + rows, :], small_ref[r0:r0 + rows, :],
                                wg_ref[...], bg_ref[...], seg01)
        v = gv_ref[r0:r0 + rows, :]
        prep.append((q, k, gc, v, v.astype(BF16)))
    levels = []
    half = rows // 2
    while half >= GLA_DIAG:
        levels.append(half)
        half //= 2
    atts = {}
    for ci, (q, k, gc, v, vb) in enumerate(prep):
        for half in levels:
            for blk in range(rows // (2 * half)):
                s0 = blk * 2 * half
                t0 = s0 + half
                ref = gc[t0 - 1:t0, :]
                qs = q[t0:t0 + half, :] * jnp.exp(gc[t0:t0 + half, :] - ref)
                ks = k[s0:t0, :] * jnp.exp(ref - gc[s0:t0, :])
                atts[ci, half, blk] = _dot_nt(_head_stack(qs, GLA_HEAD_K, GLA_HEADS).astype(BF16),
                                              ks.astype(BF16)).astype(BF16)
    outs = [_gla_pairwise(q, k, v, gc, ex_ref[...], GLA_DIAG) for q, k, gc, v, vb in prep]
    for ci, (q, k, gc, v, vb) in enumerate(prep):
        for half in levels:
            vlane = _iota2((half, GLA_DV), 1) // GLA_HEAD_V
            pieces = []
            for blk in range(rows // (2 * half)):
                s0 = blk * 2 * half
                pv = _dot(atts[ci, half, blk], vb[s0:s0 + half, :])
                ot = jnp.zeros((half, GLA_DV), F32)
                for h in range(GLA_HEADS):
                    ot = jnp.where(vlane == h, pv[h * half:(h + 1) * half, :], ot)
                pieces += [jnp.zeros((half, GLA_DV), F32), ot]
            outs[ci] = outs[ci] + jnp.concatenate(pieces, axis=0)
    for ci, (q, k, gc, v, vb) in enumerate(prep):
        outs[ci] = outs[ci] + _dot((q * jnp.exp(gc)).astype(BF16), s_all.astype(BF16))
        gc_t = gc.T
        dcol = gc_t[:, rows - 1:rows]
        kd_t = k.T * jnp.exp(dcol - gc_t)
        upd = _dot(kd_t.astype(BF16), vb)
        s_all = jnp.exp(dcol) * s_all + jnp.where(bd, upd, 0.0)
    o_ref[...] = _gla_finish(jnp.concatenate(outs, axis=0), gr_ref[...], seg_ref[...],
                             nw_ref[...]).astype(o_ref.dtype)
    s_scr[...] = s_all

    @pl.when(c == pl.num_programs(1) - 1)
    def _():
        st_ref[0] = s_all


def _gla_vec_specs():
    return [_const_spec((LANES, GLA_DK)), _const_spec((1, GLA_DK)), _const_spec((1, GLA_DV)),
            _const_spec((GLA_DK, GLA_DV)), _const_spec((GLA_DV, GLA_DV))]


def _gla_prompt(gq, gk, gv, gr, small, lp, b, l):
    nc = l // GLA_STEP
    rs = lambda w: pl.BlockSpec((GLA_STEP, w), lambda bi, ci: (bi * nc + ci, 0))
    expand, seg = _gla_consts()
    return pl.pallas_call(
        _gla_prompt_kernel,
        grid=(b, nc),
        in_specs=[rs(GLA_DK), rs(GLA_DK), rs(GLA_DV), rs(GLA_DV), rs(LANES)] + _gla_vec_specs(),
        out_specs=[rs(GLA_DV), pl.BlockSpec((1, GLA_DK, GLA_DV), lambda bi, ci: (bi, 0, 0))],
        out_shape=[jax.ShapeDtypeStruct((b * l, GLA_DV), BF16),
                   jax.ShapeDtypeStruct((b, GLA_DK, GLA_DV), F32)],
        scratch_shapes=[pltpu.VMEM((GLA_DK, GLA_DV), F32)],
        compiler_params=_params(("arbitrary", "arbitrary")),
        name="gla_prompt",
    )(gq, gk, gv, gr, small, lp["gla_wg"], lp["gla_bg"], lp["gla_nw"], expand, seg)


def _gla_sample_kernel(gq_ref, gk_ref, gv_ref, gr_ref, small_ref, s0_ref, wg_ref, bg_ref, nw_ref, ex_ref, seg_ref,
                       o_ref, st_ref, *, seq):
    rows = gq_ref.shape[0]
    nseq = rows // seq
    qi = _iota2((rows, rows), 0)
    si = _iota2((rows, rows), 1)
    seg01 = jnp.where(((qi // seq) == (si // seq)) & (si <= qi), 1.0, 0.0).astype(BF16)
    last01 = jnp.where(si == (qi // seq) * seq + (seq - 1), 1.0, 0.0).astype(BF16)
    q, k, gc = _gla_prepare(gq_ref[...], gk_ref[...], small_ref[...], wg_ref[...], bg_ref[...], seg01)
    v = gv_ref[...]
    vb = v.astype(BF16)
    gc_last = _split3_dot(last01, gc)
    qg = (q * jnp.exp(gc)).astype(BF16)
    kd_t = (k * jnp.exp(gc_last - gc)).T
    dec_t = jnp.exp(gc_last).T
    colseq = _iota2((GLA_DK, rows), 1) // seq
    rowseq = _iota2((rows, GLA_DV), 0) // seq
    bd = (_iota2((GLA_DK, GLA_DV), 0) // GLA_HEAD_K) == (_iota2((GLA_DK, GLA_DV), 1) // GLA_HEAD_V)
    o = _gla_pairwise(q, k, v, gc, ex_ref[...], seq)
    for b in range(nseq):
        s0 = s0_ref[b]
        o = o + jnp.where(rowseq == b, _dot(qg, s0.astype(BF16)), 0.0)
        upd = _dot(jnp.where(colseq == b, kd_t, 0.0).astype(BF16), vb)
        st_ref[b] = dec_t[:, b * seq:b * seq + 1] * s0 + jnp.where(bd, upd, 0.0)
    o_ref[...] = _gla_finish(o, gr_ref[...], seg_ref[...], nw_ref[...]).astype(o_ref.dtype)


def _gla_sample(gq, gk, gv, gr, small, s0, lp, b, l):
    rows = SAMPLE_SEQS * l
    rs = lambda w: pl.BlockSpec((rows, w), lambda i: (i, 0))
    st = pl.BlockSpec((SAMPLE_SEQS, GLA_DK, GLA_DV), lambda i: (i, 0, 0))
    expand, seg = _gla_consts()
    return pl.pallas_call(
        functools.partial(_gla_sample_kernel, seq=l),
        grid=(b // SAMPLE_SEQS,),
        in_specs=[rs(GLA_DK), rs(GLA_DK), rs(GLA_DV), rs(GLA_DV), rs(LANES), st] + _gla_vec_specs(),
        out_specs=[rs(GLA_DV), st],
        out_shape=[jax.ShapeDtypeStruct((b * l, GLA_DV), BF16),
                   jax.ShapeDtypeStruct((b, GLA_DK, GLA_DV), F32)],
        compiler_params=_params(("arbitrary",)),
        name="gla_sample",
    )(gq, gk, gv, gr, small, s0, lp["gla_wg"], lp["gla_bg"], lp["gla_nw"], expand, seg)


def _gla_state_expand(s):
    b = s.shape[0]
    eye = jnp.eye(GLA_HEADS, dtype=s.dtype)
    return (s[:, :, :, None, :] * eye[None, :, None, :, None]).reshape(b, GLA_DK, GLA_DV)


def _gla_state_extract(s):
    b = s.shape[0]
    s5 = s.reshape(b, GLA_HEADS, GLA_HEAD_K, GLA_HEADS, GLA_HEAD_V)
    return jnp.stack([s5[:, h, :, h, :] for h in range(GLA_HEADS)], axis=1)


def _att_window(qs, ks, vs, valids, prev):
    lane_half = _iota2(qs[0].shape, 1) // ATT_HEAD_DIM
    heads = [(p, half) for p in range(len(qs)) for half in range(2)]
    scores = [_dot_nt(jnp.where(lane_half == half, qs[p], 0.0).astype(BF16), ks[p]) for p, half in heads]
    probs, stats = [], []
    for (p, half), s in zip(heads, scores):
        lse_prev = prev[p][0]
        s = jnp.where(valids[p], s, NEG)
        smax = jnp.max(s, axis=-1, keepdims=True)
        mn = jnp.broadcast_to(smax, qs[p].shape) if lse_prev is None else jnp.maximum(lse_prev[half], smax)
        pr = jnp.exp2(s - jnp.concatenate([mn] * (s.shape[1] // LANES), axis=1))
        psum = jnp.sum(pr, axis=-1, keepdims=True)
        if lse_prev is None:
            alpha, ln = None, jnp.broadcast_to(psum, qs[p].shape)
        else:
            alpha = jnp.exp2(lse_prev[half] - mn)
            ln = alpha + psum
        probs.append(pr.astype(BF16))
        stats.append((mn, ln, alpha))
    pvs = [_dot(pr, vs[p]) for (p, half), pr in zip(heads, probs)]
    new = []
    for p in range(len(qs)):
        a_prev = prev[p][1]
        (m0, l0, al0), (m1, l1, al1) = stats[2 * p], stats[2 * p + 1]
        a0 = pvs[2 * p] if al0 is None else a_prev * al0 + pvs[2 * p]
        a1 = pvs[2 * p + 1] if al1 is None else a_prev * al1 + pvs[2 * p + 1]
        new.append(([m0, m1], [l0, l1], jnp.where(lane_half == 0, a0, a1)))
    return new


def _att_prompt_kernel(q_ref, k_ref, v_ref, out_ref, acc_scr, lse_scr):
    tq = ATT_BLOCK
    sup = out_ref.shape[1]
    base = pl.program_id(1) * sup
    rel = tq + _iota2((tq, 2 * tq), 0) - _iota2((tq, 2 * tq), 1)
    band = (rel >= 0) & (rel <= ATT_KEYS - 1)
    in_cur = _iota2((tq, 2 * tq), 1) >= tq
    strides = sorted((d for _, d in DILATION_PATTERNS), reverse=True)
    for idx, d in enumerate(strides):
        first, last = idx == 0, idx == len(strides) - 1

        def rows(start, d=d):
            return pl.ds(start, tq, stride=d) if d > 1 else pl.ds(start, tq)

        def body(it, carry, d=d, first=first, last=last, rows=rows):
            npair = ATT_HEADS // 2
            locs, qs, ks, vs, valids, prev = [], [], [], [], [], []
            for j in range(ATT_UNROLL):
                sb = it * ATT_UNROLL + j
                if d > 1:
                    r = sb % d
                    mi = sb // d
                    loc = r + d * tq * mi
                    start_q = base + loc
                else:
                    mi = sb
                    loc = pl.multiple_of(sb * tq, tq)
                    start_q = pl.multiple_of(base + loc, tq)
                mglob = base // (d * tq) + mi
                start_p = jnp.where(mglob == 0, start_q, start_q - d * tq)
                valid = band & ((mglob > 0) | in_cur)
                locs.append(loc)
                for p in range(npair):
                    if first:
                        prev.append((None, None))
                    else:
                        prev.append(([lse_scr[2 * p + hf, rows(loc), :] for hf in range(2)],
                                     acc_scr[p, rows(loc), :]))
                    qs.append((q_ref[p, rows(start_q), :] * (ATT_HEAD_DIM ** -0.5 * math.log2(math.e))).astype(BF16))
                    ks.append(jnp.concatenate([k_ref[p, rows(start_p), :], k_ref[p, rows(start_q), :]],
                                              axis=0).astype(BF16))
                    vs.append(jnp.concatenate([v_ref[p, rows(start_p), :], v_ref[p, rows(start_q), :]],
                                              axis=0).astype(BF16))
                    valids.append(valid)
            new = _att_window(qs, ks, vs, valids, prev)
            lane_half = _iota2((tq, LANES), 1) // ATT_HEAD_DIM
            for j, loc in enumerate(locs):
                for p in range(npair):
                    m_new, l_new, a_new = new[j * npair + p]
                    norm = a_new / jnp.where(lane_half == 0, l_new[0], l_new[1])
                    if last:
                        out_ref[p, rows(loc), :] = norm.astype(out_ref.dtype)
                    else:
                        for hf in range(2):
                            lse_scr[2 * p + hf, rows(loc), :] = m_new[hf] + jnp.log2(l_new[hf])
                        acc_scr[p, rows(loc), :] = norm
            return carry

        lax.fori_loop(0, sup // (tq * ATT_UNROLL), body, 0)


def _att_prompt(aq, ak, av, b, l):
    npair = ATT_HEADS // 2
    sup = ATT_BLOCK * max(d for _, d in DILATION_PATTERNS)
    assert l % sup == 0
    seq = pl.BlockSpec((npair, l, LANES), lambda bi, j: (0, bi, 0))
    return pl.pallas_call(
        _att_prompt_kernel,
        grid=(b, l // sup),
        in_specs=[seq, seq, seq],
        out_specs=pl.BlockSpec((npair, sup, LANES), lambda bi, j: (0, bi * (l // sup) + j, 0)),
        out_shape=jax.ShapeDtypeStruct((npair, b * l, LANES), BF16),
        scratch_shapes=[pltpu.VMEM((npair, sup, LANES), F32), pltpu.VMEM((ATT_HEADS, sup, LANES), F32)],
        compiler_params=_params(("arbitrary", "arbitrary")),
        name="att_prompt",
    )(aq, ak, av)


def _att_counts(seq, nbuf):
    qpos = nbuf + np.arange(seq)[:, None]
    kpos = np.arange(nbuf + seq)[None, :]
    delta = qpos - kpos
    cnt = np.zeros(delta.shape, np.float32)
    for window, stride in DILATION_PATTERNS:
        cnt += ((delta >= 0) & (delta % stride == 0) & (delta <= window)).astype(np.float32)
    cnt = np.tile(cnt, (2, 1))
    new = np.zeros((2 * seq, LANES), np.float32)
    new[:, :seq] = cnt[:, nbuf:]
    return jnp.asarray(cnt[:, :nbuf]), jnp.asarray(new)


def _att_sample_kernel(q_ref, kn_ref, vn_ref, kc_ref, vc_ref, cc_ref, cn_ref, out_ref):
    nseq = kc_ref.shape[0]
    seq = q_ref.shape[1] // nseq
    cc = cc_ref[...]
    cn = cn_ref[...]
    pad = jnp.zeros((LANES - seq, LANES), F32)
    lane_half = _iota2((seq, LANES), 1) // ATT_HEAD_DIM
    for b in range(nseq):
        rows = slice(b * seq, (b + 1) * seq)
        for p in range(ATT_HEADS // 2):
            q = q_ref[p, rows, :] * (ATT_HEAD_DIM ** -0.5)
            q2 = jnp.concatenate([jnp.where(lane_half == 0, q, 0.0), jnp.where(lane_half == 1, q, 0.0)],
                                 axis=0).astype(BF16)
            kt = kc_ref[b, 2 * p:2 * p + 2].reshape(LANES, -1).astype(BF16)
            vt = vc_ref[b, 2 * p:2 * p + 2].reshape(LANES, -1).astype(BF16)
            kn = jnp.concatenate([kn_ref[p, rows, :], pad], axis=0).astype(BF16)
            vn = jnp.concatenate([vn_ref[p, rows, :], pad], axis=0).astype(BF16)
            sc = jnp.where(cc > 0, _dot(q2, kt), NEG)
            sn = jnp.where(cn > 0, _dot_nt(q2, kn), NEG)
            m = jnp.maximum(jnp.max(sc, axis=-1, keepdims=True), jnp.max(sn, axis=-1, keepdims=True))
            pc = cc * jnp.exp(sc - m)
            pn = cn * jnp.exp(sn - m)
            den = jnp.sum(pc, axis=-1, keepdims=True) + jnp.sum(pn, axis=-1, keepdims=True)
            o = (_dot_nt(pc.astype(BF16), vt) + _dot(pn.astype(BF16), vn)) / den
            out_ref[p, rows, :] = jnp.where(lane_half == 0, o[0:seq, :], o[seq:2 * seq, :]).astype(out_ref.dtype)


def _att_sample(aq, ak, av, kcache_t, vcache_t, layer, b, l):
    npair = ATT_HEADS // 2
    nbuf = kcache_t.shape[-1]
    cc, cn = _att_counts(l, nbuf)
    ns = ATT_SAMPLE_SEQS
    assert b % ns == 0
    new = pl.BlockSpec((npair, ns * l, LANES), lambda i: (0, i, 0))
    cache = pl.BlockSpec((None, ns, ATT_HEADS, ATT_HEAD_DIM, nbuf), lambda i: (layer, i, 0, 0, 0))
    return pl.pallas_call(
        _att_sample_kernel,
        grid=(b // ns,),
        in_specs=[new, new, new, cache, cache, _const_spec((2 * l, nbuf)), _const_spec((2 * l, LANES))],
        out_specs=new,
        out_shape=jax.ShapeDtypeStruct((npair, b * l, LANES), BF16),
        compiler_params=_params(("arbitrary",)),
        name="att_sample",
    )(aq, ak, av, kcache_t, vcache_t, cc, cn)


def _pad_lanes(v, width, offset=0):
    out = jnp.zeros((1, width), F32)
    return out.at[0, offset:offset + v.shape[0]].set(v.astype(F32))


def _wt_kernel(w_ref, dt_ref, glr_ref, o_ref):
    g = pl.program_id(0)
    last = pl.num_programs(0) - 1
    depth = o_ref.shape[0]

    @pl.when(g < last)
    def _():
        for l in range(depth):
            o_ref[l] = w_ref[:, l, :].T.astype(o_ref.dtype)

    @pl.when(g == last)
    def _():
        row = _iota2((SUBLANES, D_MODEL), 0)
        pad = jnp.zeros((LANES - SMALL_GLR_OFF - GLA_GATE_RANK, D_MODEL), F32)
        for l in range(depth):
            dt = jnp.where(row < SSD_HEADS, dt_ref[:, l, :], 0.0)
            o_ref[l] = jnp.concatenate([dt, glr_ref[:, l, :], pad], axis=0).T.astype(o_ref.dtype)


def _proj_weights(w_in):
    depth = w_in.shape[0]
    assert SMALL_DT_OFF == 0 and SMALL_GLR_OFF == SUBLANES and SSD_HEADS <= SUBLANES
    wt = w_in.transpose(2, 0, 1)
    offs = dict(zip(("z", "xbc", "dt", "gq", "gk", "gv", "gr", "glr", "aq", "ak", "av"),
                    np.concatenate([[0], np.cumsum(IN_SPLITS)])[:-1]))
    src = [int(offs[n]) + i * LANES for n, w in PROJ_GROUPS if n != "small" for i in range(w // LANES)]
    steps = [(g, src[g] - src[g - 1] - LANES) for g in range(1, len(src)) if src[g] - src[g - 1] != LANES]
    assert src[0] == 0 and PROJ_GROUPS[-1][0] == "small"

    def w_map(g):
        off = LANES * g
        for g0, delta in steps:
            off = off + jnp.where(g >= g0, delta, 0)
        return (jnp.where(g < len(src), off, 0), 0, 0)

    el = lambda n: (pl.Element(n), pl.Element(depth), pl.Element(D_MODEL))
    return pl.pallas_call(
        _wt_kernel,
        grid=(len(src) + 1,),
        in_specs=[pl.BlockSpec(el(LANES), w_map),
                  pl.BlockSpec(el(SUBLANES), lambda g: (int(offs["dt"]), 0, 0)),
                  pl.BlockSpec(el(GLA_GATE_RANK), lambda g: (int(offs["glr"]), 0, 0))],
        out_specs=pl.BlockSpec((depth, D_MODEL, LANES), lambda g: (0, 0, g)),
        out_shape=jax.ShapeDtypeStruct((depth, D_MODEL, PROJ_WIDTH), BF16),
        compiler_params=_params(("arbitrary",)),
        name="proj_weights",
    )(wt, wt, wt)


def _layer_params(l, ssd_conv_w, ssd_conv_b, ssd_dt_bias, ssd_a_log, ssd_d, ssd_norm_w,
                  gla_w_gate, gla_b_gate, gla_norm_w, norm_w):
    wg = jnp.zeros((LANES, GLA_DK), F32).at[SMALL_GLR_OFF:SMALL_GLR_OFF + GLA_GATE_RANK, :].set(gla_w_gate[l])
    return dict(
        conv_w=ssd_conv_w[l], conv_b=ssd_conv_b[l].reshape(1, SSD_CONV_DIM),
        dtb=_pad_lanes(ssd_dt_bias[l], LANES, SMALL_DT_OFF), alog=_pad_lanes(ssd_a_log[l], LANES, SMALL_DT_OFF),
        dvec=jnp.repeat(ssd_d[l].astype(F32), SSD_HEAD_DIM).reshape(1, SSD_INNER),
        ssd_nw=ssd_norm_w[l].reshape(1, SSD_INNER),
        gla_wg=wg.astype(BF16), gla_bg=gla_b_gate[l].reshape(1, GLA_DK),
        gla_nw=jnp.tile(gla_norm_w[l], GLA_HEADS).reshape(1, GLA_DV),
        norm_w=norm_w[l],
    )


def _trunk(x, mods, layers, stacked, norm_f, states, sample):
    b, l, _ = x.shape
    depth = len(layers)
    keep = min(ATT_MAX_WINDOW, l)
    names = [n for n, _ in PROJ_GROUPS]
    ssd_list, conv_list, gla_list, k_list, v_list = [], [], [], [], []
    ssd_stack, kv_stack = None, None
    for li, lp in enumerate(layers):
        mod = mods[li]
        x = _ffn(x, mod, 0, lp["norm_w"][0], stacked["ffn1_in"], stacked["ffn1_out"], li)
        if sample:
            proj = dict(zip(names, _inproj(x, mod, lp["norm_w"][1], stacked["w_in_p"], li)))
            st_ssd, st_conv, st_gla, kcache, vcache = states
            bufp = jnp.pad(st_conv[li], ((0, 0), (0, l - (SSD_CONV - 1)), (0, 0))).reshape(b * l, SSD_CONV_DIM)
            y, ssd_stack = _ssd_sample(proj["z"], proj["xbc"], proj["small"], bufp, st_ssd, li, lp, b, l,
                                       new_prev=ssd_stack)
            o, gla_new = _gla_sample(proj["gq"], proj["gk"], proj["gv"], proj["gr"], proj["small"],
                                     _gla_state_expand(st_gla[li]), lp, b, l)
            att = _att_sample(proj["aq"], proj["ak"], proj["av"], kcache, vcache, li, b, l)
            for acc, name in ((k_list, "ak"), (v_list, "av")):
                kv = proj[name].reshape(ATT_HEADS // 2, b, l, 2, ATT_HEAD_DIM)[:, :, l - keep:]
                acc.append(kv.transpose(1, 2, 0, 3, 4).reshape(b, keep, ATT_HEADS, ATT_HEAD_DIM))
        else:
            res = _inproj(x, mod, lp["norm_w"][1], stacked["w_in_p"], li, keep_t=keep, kv_prev=kv_stack)
            proj = dict(zip(names, res))
            kv_stack = tuple(res[len(names):])
            y, ssd_new = _ssd_prompt(proj["z"], proj["xbc"], proj["small"], lp, b, l)
            ssd_list.append(ssd_new)
            o, gla_new = _gla_prompt(proj["gq"], proj["gk"], proj["gv"], proj["gr"], proj["small"], lp, b, l)
            att = _att_prompt(proj["aq"], proj["ak"], proj["av"], b, l)
        x = _ffn(x, mod, 6, lp["norm_w"][2], stacked["ffn2_in"], stacked["ffn2_out"], li,
                 premix=(y, o, att, 5, stacked["w_out"]),
                 final_norm=norm_f if li == depth - 1 else None)
        conv_list.append(proj["xbc"].reshape(b, l, SSD_CONV_DIM)[:, l - (SSD_CONV - 1):])
        gla_list.append(_gla_state_extract(gla_new))
    if sample:
        ssd_out, k_out, v_out = ssd_stack, jnp.stack(k_list), jnp.stack(v_list)
    else:
        ssd_out = jnp.stack(ssd_list)
        k_out, v_out = (t.reshape(depth, b, ATT_HEADS, ATT_HEAD_DIM, keep).transpose(0, 1, 4, 2, 3) for t in kv_stack)
    return x, [ssd_out, jnp.stack(conv_list), jnp.stack(gla_list), k_out, v_out]


def kernel(x_prompt, x_sample, c_prompt, c_sample, state_ssd, state_ssd_conv, state_gla, cache_attn_k, cache_attn_v,
           w_in, w_out, ssd_conv_w, ssd_conv_b, ssd_dt_bias, ssd_a_log, ssd_d, ssd_norm_w,
           gla_w_gate, gla_b_gate, gla_norm_w, norm_w, w_mod, b_mod,
           ffn1_w_in, ffn1_w_out, ffn2_w_in, ffn2_w_out, norm_f):
    bp, bs = x_prompt.shape[0], x_sample.shape[0]
    depth = w_in.shape[0]
    layers = [_layer_params(l, ssd_conv_w, ssd_conv_b, ssd_dt_bias, ssd_a_log, ssd_d, ssd_norm_w,
                            gla_w_gate, gla_b_gate, gla_norm_w, norm_w) for l in range(depth)]
    stacked = dict(w_in_p=_proj_weights(w_in), w_out=w_out.astype(BF16), ffn1_in=ffn1_w_in.astype(BF16), ffn1_out=ffn1_w_out.astype(BF16),
                   ffn2_in=ffn2_w_in.astype(BF16), ffn2_out=ffn2_w_out.astype(BF16))
    npad = -(bp + bs) % SUBLANES
    c_all = jnp.concatenate([c_prompt, c_sample, jnp.zeros((npad, D_MODEL), F32)], axis=0)
    m_all = _modulation(c_all, w_mod.astype(BF16), b_mod)
    mods_p, mods_s = [], []
    for l in range(depth):
        m = m_all[l].reshape(-1, ADALN_MODS, 1, D_MODEL).transpose(1, 0, 2, 3)
        mods_p.append(m[:, :bp])
        mods_s.append(m[:, bp:bp + bs])
    kcache = cache_attn_k.transpose(0, 1, 3, 4, 2)
    vcache = cache_attn_v.transpose(0, 1, 3, 4, 2)
    y_p, (ssd_p, conv_p, gla_p, k_p, v_p) = _trunk(x_prompt, mods_p, layers, stacked, norm_f, None, sample=False)
    y_s, (ssd_s, conv_s, gla_s, k_s, v_s) = _trunk(
        x_sample, mods_s, layers, stacked, norm_f, (state_ssd, state_ssd_conv, state_gla, kcache, vcache),
        sample=True)
    return (y_p, y_s, ssd_p, ssd_s, conv_p, conv_s, gla_p, gla_s, k_p, k_s, v_p, v_s)
```

```python
import functools
import math

import numpy as np
import jax
import jax.numpy as jnp
from jax import lax
from jax.experimental import pallas as pl
from jax.experimental.pallas import tpu as pltpu

F32 = jnp.float32
BF16 = jnp.bfloat16

D_MODEL = 1024
DEPTH = 2
SSD_HEADS = 6
SSD_HEAD_DIM = 64
SSD_INNER = SSD_HEADS * SSD_HEAD_DIM
SSD_GROUPS = 2
SSD_STATE = 128
SSD_CONV = 4
SSD_CONV_DIM = SSD_INNER + 2 * SSD_GROUPS * SSD_STATE
GLA_HEADS = 4
GLA_HEAD_K = 32
GLA_HEAD_V = 64
GLA_DK = GLA_HEADS * GLA_HEAD_K
GLA_DV = GLA_HEADS * GLA_HEAD_V
GLA_GATE_RANK = 16
GLA_TAU = 16.0
ATT_HEADS = 6
ATT_HEAD_DIM = 64
ATT_DIM = ATT_HEADS * ATT_HEAD_DIM
DILATION_PATTERNS = ((128, 1), (512, 4), (2048, 16))
ATT_MAX_WINDOW = 2048
ATT_KEYS = 129
D_MIX = SSD_INNER + GLA_DV + ATT_DIM
IN_SPLITS = (SSD_INNER, SSD_CONV_DIM, SSD_HEADS, GLA_DK, GLA_DK, GLA_DV, GLA_DV, GLA_GATE_RANK,
             ATT_DIM, ATT_DIM, ATT_DIM)
D_FF = 2816
ADALN_MODS = 9
FFN_RES = 0.5
EPS = 1e-6

LANES = 128
SUBLANES = 8
VMEM_LIMIT = 56 * 1024 * 1024

PROJ_GROUPS = (("z", SSD_INNER), ("xbc", SSD_CONV_DIM), ("gq", GLA_DK), ("gk", GLA_DK), ("gv", GLA_DV),
               ("gr", GLA_DV), ("aq", ATT_DIM), ("ak", ATT_DIM), ("av", ATT_DIM), ("small", LANES))
PROJ_WIDTH = sum(w for _, w in PROJ_GROUPS)
PAIR_MAJOR = ("aq", "ak", "av")
KV_T = ("ak", "av")
SMALL_DT_OFF = 0
SMALL_GLR_OFF = 8

ROW_TILE = 512
FF_CHUNK = 256
SSD_CHUNK = 128
SSD_STEP = 512
GLA_CHUNK = 128
GLA_STEP = 512
GLA_DIAG = 16
SAMPLE_SEQS = 16
ATT_BLOCK = 128
ATT_SAMPLE_SEQS = 2
ATT_UNROLL = 2
NEG = -1e30


def _dot(a, b):
    return jnp.dot(a, b, preferred_element_type=F32)


def _dot_nt(a, b):
    return lax.dot_general(a, b, (((1,), (1,)), ((), ())), preferred_element_type=F32)


def _sigmoid(x):
    return 1.0 / (1.0 + jnp.exp(-x))


def _silu(x):
    return x * _sigmoid(x)


def _softplus(x):
    return jnp.maximum(x, 0.0) + jnp.log1p(jnp.exp(-jnp.abs(x)))


def _split3_dot(m01, a):
    a1 = a.astype(BF16)
    r1 = a - a1.astype(F32)
    a2 = r1.astype(BF16)
    a3 = (r1 - a2.astype(F32)).astype(BF16)
    return _dot(m01, a1) + _dot(m01, a2) + _dot(m01, a3)


def _split2_dot(a, m01):
    a1 = a.astype(BF16)
    a2 = (a - a1.astype(F32)).astype(BF16)
    return _dot(a1, m01) + _dot(a2, m01)


def _rms_mod(x, nw, shift, scale):
    ms = jnp.mean(x * x, axis=-1, keepdims=True)
    y = x * lax.rsqrt(ms + EPS) * nw
    return y * (1.0 + scale) + shift


def _iota2(shape, axis):
    return lax.broadcasted_iota(jnp.int32, shape, axis)


def _params(sem):
    return pltpu.CompilerParams(dimension_semantics=sem, vmem_limit_bytes=VMEM_LIMIT)


def _const_spec(shape, layer=None):
    nd = len(shape)
    if layer is None:
        return pl.BlockSpec(shape, lambda *_: (0,) * nd, pipeline_mode=pl.Buffered(1))
    return pl.BlockSpec((None,) + tuple(shape), lambda *_: (layer,) + (0,) * nd, pipeline_mode=pl.Buffered(1))


def _mod_kernel(c_ref, w_ref, b_ref, *o_refs):
    c = c_ref[...]
    res = _dot(_silu(c).astype(BF16), w_ref[...].astype(BF16)) + b_ref[...]
    row = 0
    for o_ref in o_refs:
        n = o_ref.shape[0]
        o_ref[...] = res[row:row + n, :].reshape(n, 1, res.shape[1])
        row += -(-n // SUBLANES) * SUBLANES


def _modulation(c_groups, w_mod, b_mod):
    d = c_groups[0].shape[1]
    depth, _, nout = w_mod.shape
    padded = [jnp.pad(c, ((0, -c.shape[0] % SUBLANES), (0, 0))) for c in c_groups]
    c_all = jnp.concatenate(padded, axis=0)
    n = c_all.shape[0]
    return pl.pallas_call(
        _mod_kernel,
        grid=(depth, nout // d),
        in_specs=[pl.BlockSpec((n, d), lambda l, j: (0, 0)),
                  pl.BlockSpec((None, d, d), lambda l, j: (l, 0, j)),
                  pl.BlockSpec((None, 1, d), lambda l, j: (l, 0, j))],
        out_specs=[pl.BlockSpec((None, None, c.shape[0], 1, d), lambda l, j: (l, j, 0, 0, 0)) for c in c_groups],
        out_shape=[jax.ShapeDtypeStruct((depth, nout // d, c.shape[0], 1, d), F32) for c in c_groups],
        compiler_params=_params(("arbitrary", "arbitrary")),
        name="adaln_mod",
    )(c_all, w_mod, b_mod.reshape(depth, 1, nout))


def _row_tiling(b, l):
    if l >= ROW_TILE:
        assert l % ROW_TILE == 0
        return 1, ROW_TILE
    assert ROW_TILE % l == 0 and b % (ROW_TILE // l) == 0
    return ROW_TILE // l, l


def _x_spec(bb, ll, nlb):
    return pl.BlockSpec((bb, ll, D_MODEL), lambda i: (i // nlb, i % nlb, 0))


def _mod_spec(k, bb, nlb, layer):
    return pl.BlockSpec((None, 1, bb, 1, D_MODEL), lambda i: (layer, k, i // nlb, 0, 0))


def _rows_spec(r, width):
    return pl.BlockSpec((r, width), lambda i: (i, 0))


def _pair_spec(r, width):
    return pl.BlockSpec((width // LANES, r, LANES), lambda i: (0, i, 0))


def _ffn_kernel(*refs, premix, final):
    refs = list(refs)
    x_ref = refs.pop(0)
    if premix:
        y_ref, o_ref, a_ref, g2_ref, wo_ref = refs[:5]
        refs = refs[5:]
    sh_ref, sc_ref, gt_ref, nw_ref, win_ref, wout_ref = refs[:6]
    refs = refs[6:]
    if final:
        nf_ref = refs.pop(0)
    out_ref, act_ref = refs
    bb, ll, d = x_ref.shape
    r = bb * ll
    x = x_ref[...]
    if premix:
        mix = (_dot(y_ref[...], wo_ref[0:SSD_INNER, :])
               + _dot(o_ref[...], wo_ref[SSD_INNER:SSD_INNER + GLA_DV, :])
               + sum(_dot(a_ref[p], wo_ref[SSD_INNER + GLA_DV + p * LANES:SSD_INNER + GLA_DV + (p + 1) * LANES, :])
                     for p in range(ATT_DIM // LANES)))
        x = x + g2_ref[0] * mix.reshape(bb, ll, d)
    h = _rms_mod(x, nw_ref[...], sh_ref[0], sc_ref[0]).reshape(r, d).astype(BF16)
    for c in range(D_FF // FF_CHUNK):
        g = _dot(h, win_ref[:, c * FF_CHUNK:(c + 1) * FF_CHUNK])
        u = _dot(h, win_ref[:, D_FF + c * FF_CHUNK:D_FF + (c + 1) * FF_CHUNK])
        act_ref[:, c * FF_CHUNK:(c + 1) * FF_CHUNK] = (_silu(g) * u).astype(BF16)
    y = _dot(act_ref[...], wout_ref[...])
    x = x + FFN_RES * gt_ref[0] * y.reshape(bb, ll, d)
    if final:
        ms = jnp.mean(x * x, axis=-1, keepdims=True)
        x = x * lax.rsqrt(ms + EPS) * nf_ref[...]
    out_ref[...] = x


def _ffn(x, mod, mod_base, norm_w, w_in, w_out, layer, premix=None, final_norm=None):
    b, l, d = x.shape
    bb, ll = _row_tiling(b, l)
    nlb = l // ll
    r = bb * ll
    nsteps = (b // bb) * nlb
    args, specs = [x], [_x_spec(bb, ll, nlb)]
    if premix is not None:
        y, o, a, gate_row, wo = premix
        args += [y, o, a, mod, wo]
        specs += [_rows_spec(r, SSD_INNER), _rows_spec(r, GLA_DV), _pair_spec(r, ATT_DIM),
                  _mod_spec(gate_row, bb, nlb, layer), _const_spec((D_MIX, d), layer)]
    args += [mod, mod, mod, norm_w.reshape(1, d), w_in, w_out]
    specs += [_mod_spec(mod_base, bb, nlb, layer), _mod_spec(mod_base + 1, bb, nlb, layer),
              _mod_spec(mod_base + 2, bb, nlb, layer),
              _const_spec((1, d)), _const_spec((d, 2 * D_FF), layer), _const_spec((D_FF, d), layer)]
    if final_norm is not None:
        args.append(final_norm.reshape(1, d))
        specs.append(_const_spec((1, d)))
    return pl.pallas_call(
        functools.partial(_ffn_kernel, premix=premix is not None, final=final_norm is not None),
        grid=(nsteps,),
        in_specs=specs,
        out_specs=_x_spec(bb, ll, nlb),
        out_shape=jax.ShapeDtypeStruct((b, l, d), F32),
        scratch_shapes=[pltpu.VMEM((r, D_FF), BF16)],
        compiler_params=_params(("arbitrary",)),
        name="ffn",
    )(*args)


def _inproj_kernel(x_ref, sh_ref, sc_ref, nw_ref, w_ref, *out_refs, first_kept, n_alias):
    out_refs = out_refs[n_alias:]
    bb, ll, d = x_ref.shape
    h = _rms_mod(x_ref[...], nw_ref[...], sh_ref[0], sc_ref[0]).reshape(bb * ll, d).astype(BF16)
    results, start, run = {}, 0, []
    for name, width in PROJ_GROUPS:
        run.append((name, width))
        total = sum(w for _, w in run)
        if total % (2 * LANES) == 0:
            big = _dot(h, w_ref[:, start:start + total])
            o = 0
            for n, w in run:
                results[n] = big[:, o:o + w]
                o += w
            start, run = start + total, []
    assert not run
    for ref, (name, width) in zip(out_refs, PROJ_GROUPS):
        res = results[name]
        if name in PAIR_MAJOR:
            for p in range(width // LANES):
                ref[p] = res[:, p * LANES:(p + 1) * LANES]
        else:
            ref[...] = res
        if first_kept is not None and name in KV_T:
            t_ref = out_refs[len(PROJ_GROUPS) + KV_T.index(name)]

            @pl.when(pl.program_id(0) % first_kept[1] >= first_kept[0])
            def _(t_ref=t_ref, res=res):
                rt = res.T
                if len(t_ref.shape) == 4:
                    for lyr in range(t_ref.shape[0]):
                        t_ref[lyr, 0] = rt
                else:
                    t_ref[0] = rt


def _inproj(x, mod, norm_w, w_in_p, layer, keep_t=None, kv_prev=None):
    b, l, d = x.shape
    bb, ll = _row_tiling(b, l)
    nlb = l // ll
    r = bb * ll
    out_specs = [_pair_spec(r, w) if n in PAIR_MAJOR else _rows_spec(r, w) for n, w in PROJ_GROUPS]
    out_shape = [jax.ShapeDtypeStruct((w // LANES, b * l, LANES) if n in PAIR_MAJOR else (b * l, w), F32)
                 for n, w in PROJ_GROUPS]
    first_kept = None
    if keep_t is not None:
        assert bb == 1 and keep_t % ll == 0 and keep_t <= l
        skip = nlb - keep_t // ll
        first_kept = (skip, nlb)
        depth = w_in_p.shape[0]
        pos = lambda i: jnp.maximum(i % nlb - skip, 0)
        if kv_prev is None:
            t_spec = pl.BlockSpec((depth, 1, ATT_DIM, ll), lambda i: (0, i // nlb, 0, pos(i)))
        else:
            t_spec = pl.BlockSpec((None, 1, ATT_DIM, ll), lambda i: (layer, i // nlb, 0, pos(i)))
        out_specs += [t_spec] * len(KV_T)
        out_shape += [jax.ShapeDtypeStruct((depth, b, ATT_DIM, keep_t), F32)] * len(KV_T)
    args = [x, mod, mod, norm_w.reshape(1, d), w_in_p]
    in_specs = [_x_spec(bb, ll, nlb), _mod_spec(3, bb, nlb, layer), _mod_spec(4, bb, nlb, layer),
                _const_spec((1, d)), _const_spec((d, PROJ_WIDTH), layer)]
    aliases = {}
    if kv_prev is not None:
        for j, arr in enumerate(kv_prev):
            aliases[len(args)] = len(PROJ_GROUPS) + j
            args.append(arr)
            in_specs.append(pl.BlockSpec(memory_space=pl.ANY))
    return pl.pallas_call(
        functools.partial(_inproj_kernel, first_kept=first_kept, n_alias=len(aliases)),
        grid=((b // bb) * nlb,),
        in_specs=in_specs,
        out_specs=out_specs,
        out_shape=out_shape,
        input_output_aliases=aliases,
        compiler_params=_params(("arbitrary",)),
        name="inproj",
    )(*args)


def _ssd_prepare(conv, small, dtb, alog, seg01, seglast01):
    xc = _silu(conv)
    xs = xc[:, 0:SSD_INNER]
    bm = xc[:, SSD_INNER:SSD_INNER + SSD_GROUPS * SSD_STATE]
    cm = xc[:, SSD_INNER + SSD_GROUPS * SSD_STATE:SSD_CONV_DIM]
    dt = _softplus(small + dtb)
    a = dt * (-jnp.exp(alog))
    cum = _split3_dot(seg01, a)
    if seglast01 is None:
        cum_last = jnp.broadcast_to(cum[cum.shape[0] - 1:, :], cum.shape)
    else:
        cum_last = _split3_dot(seglast01, cum)
    return xs, bm, cm, dt, cum, cum_last


def _ssd_diag(xs, bm, cm, dt, cum, cum_t, mask):
    ydiag, xdt = [], []
    cb = [_dot_nt(cm[:, g * SSD_STATE:(g + 1) * SSD_STATE].astype(BF16),
                  bm[:, g * SSD_STATE:(g + 1) * SSD_STATE].astype(BF16)) for g in range(SSD_GROUPS)]
    for h in range(SSD_HEADS):
        g = h // (SSD_HEADS // SSD_GROUPS)
        diff = cum[:, h:h + 1] - cum_t[h:h + 1, :]
        decay = jnp.exp(jnp.where(mask, diff, NEG))
        xh = xs[:, h * SSD_HEAD_DIM:(h + 1) * SSD_HEAD_DIM] * dt[:, h:h + 1]
        xdt.append(xh)
        ydiag.append(_dot((cb[g] * decay).astype(BF16), xh.astype(BF16)))
    return ydiag, xdt


def _ssd_finish(y, xs, z, dvec, normw):
    y = (y + dvec * xs) * _silu(z)
    sq = y * y
    half = SSD_INNER // SSD_GROUPS
    lane = _iota2(y.shape, 1)
    s0 = jnp.sum(jnp.where(lane < half, sq, 0.0), axis=-1, keepdims=True)
    s1 = jnp.sum(jnp.where(lane >= half, sq, 0.0), axis=-1, keepdims=True)
    ms = jnp.where(lane < half, s0, s1) * (1.0 / half)
    return y * lax.rsqrt(ms + EPS) * normw


def _ssd_prompt_kernel(z_ref, xbc_ref, small_ref, cw_ref, cb_ref, dtb_ref, alog_ref, dvec_ref, nw_ref,
                       y_ref, st_ref, h_scr, tail_scr):
    c = pl.program_id(1)
    step = xbc_ref.shape[0]
    rows = SSD_CHUNK
    hpg = SSD_HEADS // SSD_GROUPS

    @pl.when(c == 0)
    def _():
        h_scr[...] = jnp.zeros_like(h_scr)
        tail_scr[0:SUBLANES, :] = jnp.zeros((SUBLANES, SSD_CONV_DIM), F32)

    u = xbc_ref[...]
    tail_scr[SUBLANES:SUBLANES + step, :] = u
    conv = u * cw_ref[SSD_CONV - 1:SSD_CONV, :] + cb_ref[...]
    for j in range(1, SSD_CONV):
        conv = conv + tail_scr[SUBLANES - j:SUBLANES - j + step, :] * cw_ref[SSD_CONV - 1 - j:SSD_CONV - j, :]
    tail_scr[0:SUBLANES, :] = u[step - SUBLANES:, :]

    qi = _iota2((rows, rows), 0)
    si = _iota2((rows, rows), 1)
    mask = si <= qi
    seg01 = jnp.where(mask, 1.0, 0.0).astype(BF16)
    prep, diag, local = [], [], []
    for r0 in range(0, step, rows):
        prep.append(_ssd_prepare(conv[r0:r0 + rows, :], small_ref[r0:r0 + rows, :], dtb_ref[...], alog_ref[...],
                                 seg01, None))
    for xs, bm, cm, dt, cum, cum_last in prep:
        diag.append(_ssd_diag(xs, bm, cm, dt, cum, cum.T, mask))
    for (xs, bm, cm, dt, cum, cum_last), (ydiag, xdt) in zip(prep, diag):
        to_end_t = jnp.exp(cum_last - cum).T
        xdt_t = jnp.concatenate(xdt, axis=1).T
        s_local = []
        for h in range(SSD_HEADS):
            xw_t = xdt_t[h * SSD_HEAD_DIM:(h + 1) * SSD_HEAD_DIM, :] * to_end_t[h:h + 1, :]
            bg = bm[:, (h // hpg) * SSD_STATE:(h // hpg + 1) * SSD_STATE].astype(BF16)
            s_local.append(_dot(xw_t.astype(BF16), bg))
        local.append(s_local)
    hs = [h_scr[h] for h in range(SSD_HEADS)]
    for ci, ((xs, bm, cm, dt, cum, cum_last), (ydiag, xdt)) in enumerate(zip(prep, diag)):
        ys = []
        for h in range(SSD_HEADS):
            cg = cm[:, (h // hpg) * SSD_STATE:(h // hpg + 1) * SSD_STATE].astype(BF16)
            yoff = _dot_nt(cg, hs[h].astype(BF16)) * jnp.exp(cum[:, h:h + 1])
            ys.append(ydiag[h] + yoff)
            hs[h] = jnp.exp(cum_last[0:1, h:h + 1]) * hs[h] + local[ci][h]
        r0 = ci * rows
        y = _ssd_finish(jnp.concatenate(ys, axis=1), xs, z_ref[r0:r0 + rows, :], dvec_ref[...], nw_ref[...])
        y_ref[r0:r0 + rows, :] = y.astype(y_ref.dtype)
    for h in range(SSD_HEADS):
        h_scr[h] = hs[h]

    @pl.when(c == pl.num_programs(1) - 1)
    def _():
        for h in range(SSD_HEADS):
            st_ref[0, h] = hs[h]


def _ssd_vec_specs():
    return [_const_spec((SSD_CONV, SSD_CONV_DIM)), _const_spec((1, SSD_CONV_DIM)), _const_spec((1, LANES)),
            _const_spec((1, LANES)), _const_spec((1, SSD_INNER)), _const_spec((1, SSD_INNER))]


def _ssd_prompt(z, xbc, small, lp, b, l):
    nc = l // SSD_STEP
    rs = lambda w: pl.BlockSpec((SSD_STEP, w), lambda bi, ci: (bi * nc + ci, 0))
    return pl.pallas_call(
        _ssd_prompt_kernel,
        grid=(b, nc),
        in_specs=[rs(SSD_INNER), rs(SSD_CONV_DIM), rs(LANES)] + _ssd_vec_specs(),
        out_specs=[rs(SSD_INNER),
                   pl.BlockSpec((1, SSD_HEADS, SSD_HEAD_DIM, SSD_STATE), lambda bi, ci: (bi, 0, 0, 0))],
        out_shape=[jax.ShapeDtypeStruct((b * l, SSD_INNER), BF16),
                   jax.ShapeDtypeStruct((b, SSD_HEADS, SSD_HEAD_DIM, SSD_STATE), F32)],
        scratch_shapes=[pltpu.VMEM((SSD_HEADS, SSD_HEAD_DIM, SSD_STATE), F32),
                        pltpu.VMEM((SUBLANES + SSD_STEP, SSD_CONV_DIM), F32)],
        compiler_params=_params(("arbitrary", "arbitrary")),
        name="ssd_prompt",
    )(z, xbc, small, lp["conv_w"], lp["conv_b"], lp["dtb"], lp["alog"], lp["dvec"], lp["ssd_nw"])


def _ssd_sample_kernel(z_ref, xbc_ref, small_ref, buf_ref, h0_ref, cw_ref, cb_ref, dtb_ref, alog_ref, dvec_ref,
                       nw_ref, *rest, seq):
    y_ref, st_ref = rest[-2:]
    st_slabs = [st_ref.at[lyr] for lyr in range(st_ref.shape[0])] if len(st_ref.shape) == 5 else [st_ref]
    rows = xbc_ref.shape[0]
    nseq = rows // seq
    u = xbc_ref[...]
    bufp = buf_ref[...]
    tpos = _iota2((rows, SSD_CONV_DIM), 0) % seq
    conv = u * cw_ref[SSD_CONV - 1:SSD_CONV, :] + cb_ref[...]
    for j in range(1, SSD_CONV):
        uj = pltpu.roll(u, j, axis=0)
        back = (rows - (SSD_CONV - 1 - j)) % rows
        bj = pltpu.roll(bufp, back, axis=0) if back else bufp
        conv = conv + jnp.where(tpos < j, bj, uj) * cw_ref[SSD_CONV - 1 - j:SSD_CONV - j, :]

    qi = _iota2((rows, rows), 0)
    si = _iota2((rows, rows), 1)
    same = (qi // seq) == (si // seq)
    mask = same & (si <= qi)
    seg01 = jnp.where(mask, 1.0, 0.0).astype(BF16)
    last01 = jnp.where(si == (qi // seq) * seq + (seq - 1), 1.0, 0.0).astype(BF16)
    xs, bm, cm, dt, cum, cum_last = _ssd_prepare(conv, small_ref[...], dtb_ref[...], alog_ref[...], seg01, last01)
    cum_t = cum.T
    cum_last_t = cum_last.T
    ydiag, xdt = _ssd_diag(xs, bm, cm, dt, cum, cum_t, mask)
    to_end_t = jnp.exp(cum_last_t - cum_t)
    xdt_t = jnp.concatenate(xdt, axis=1).T
    hpg = SSD_HEADS // SSD_GROUPS
    grows = hpg * SSD_HEAD_DIM
    colseq = _iota2((grows, rows), 1) // seq
    yoff_t = []
    for g in range(SSD_GROUPS):
        cg = cm[:, g * SSD_STATE:(g + 1) * SSD_STATE].astype(BF16)
        bg = bm[:, g * SSD_STATE:(g + 1) * SSD_STATE].astype(BF16)
        xw_t = jnp.concatenate(
            [xdt_t[h * SSD_HEAD_DIM:(h + 1) * SSD_HEAD_DIM, :] * to_end_t[h:h + 1, :]
             for h in range(g * hpg, (g + 1) * hpg)], axis=0)
        acc = jnp.zeros((grows, rows), F32)
        for b in range(nseq):
            h0 = h0_ref[b, g * hpg:(g + 1) * hpg].reshape(grows, SSD_STATE)
            acc = jnp.where(colseq == b, _dot_nt(h0.astype(BF16), cg), acc)
            s_local = _dot(jnp.where(colseq == b, xw_t, 0.0).astype(BF16), bg)
            for hh in range(hpg):
                h = g * hpg + hh
                dec = jnp.exp(cum_last_t[h:h + 1, b * seq:b * seq + 1])
                h_new = (dec * h0[hh * SSD_HEAD_DIM:(hh + 1) * SSD_HEAD_DIM, :]
                         + s_local[hh * SSD_HEAD_DIM:(hh + 1) * SSD_HEAD_DIM, :])
                for slab in st_slabs:
                    slab[b, h] = h_new
        for hh in range(hpg):
            h = g * hpg + hh
            yoff_t.append(acc[hh * SSD_HEAD_DIM:(hh + 1) * SSD_HEAD_DIM, :] * jnp.exp(cum_t[h:h + 1, :]))
    yoff = jnp.concatenate(yoff_t, axis=0).T
    y = _ssd_finish(jnp.concatenate(ydiag, axis=1) + yoff, xs, z_ref[...], dvec_ref[...], nw_ref[...])
    y_ref[...] = y.astype(y_ref.dtype)


def _ssd_sample(z, xbc, small, bufp, h0_all, layer, lp, b, l, new_prev=None):
    depth = h0_all.shape[0]
    rows = SAMPLE_SEQS * l
    rs = lambda w: pl.BlockSpec((rows, w), lambda i: (i, 0))
    tile = (SAMPLE_SEQS, SSD_HEADS, SSD_HEAD_DIM, SSD_STATE)
    st_in = pl.BlockSpec((None,) + tile, lambda i: (layer, i, 0, 0, 0))
    args = [z, xbc, small, bufp, h0_all, lp["conv_w"], lp["conv_b"], lp["dtb"], lp["alog"], lp["dvec"], lp["ssd_nw"]]
    in_specs = [rs(SSD_INNER), rs(SSD_CONV_DIM), rs(LANES), rs(SSD_CONV_DIM), st_in] + _ssd_vec_specs()
    aliases = {}
    if new_prev is None:
        st_out = pl.BlockSpec((depth,) + tile, lambda i: (0, i, 0, 0, 0))
    else:
        st_out = st_in
        aliases[len(args)] = 1
        args.append(new_prev)
        in_specs.append(pl.BlockSpec(memory_space=pl.ANY))
    return pl.pallas_call(
        functools.partial(_ssd_sample_kernel, seq=l),
        grid=(b // SAMPLE_SEQS,),
        in_specs=in_specs,
        out_specs=[rs(SSD_INNER), st_out],
        out_shape=[jax.ShapeDtypeStruct((b * l, SSD_INNER), BF16),
                   jax.ShapeDtypeStruct((depth, b, SSD_HEADS, SSD_HEAD_DIM, SSD_STATE), F32)],
        input_output_aliases=aliases,
        compiler_params=_params(("arbitrary",)),
        name="ssd_sample",
    )(*args)


def _gla_consts():
    rk = np.arange(GLA_DK)[:, None] // GLA_HEAD_K
    cv = np.arange(GLA_DV)[None, :] // GLA_HEAD_V
    expand = (rk == cv).astype(np.float32)
    rv = np.arange(GLA_DV)[:, None] // GLA_HEAD_V
    seg = (rv == cv).astype(np.float32) / GLA_HEAD_V
    return jnp.asarray(expand, BF16), jnp.asarray(seg, BF16)


def _gla_prepare(gq, gk, small, wg, bg, seg01):
    glin = _dot(small.astype(BF16), wg) + bg
    g = -_softplus(-glin) * (1.0 / GLA_TAU)
    gc = _split3_dot(seg01, g)
    q = gq * (GLA_HEAD_K ** -0.5)
    return q, gk, gc


def _gla_pairwise(q, k, v, gc, expand, diag):
    rows = q.shape[0]
    nb = rows // diag
    q4 = q.reshape(nb, 1, diag, GLA_DK)
    g4 = gc.reshape(nb, 1, diag, GLA_DK)
    k4 = k.reshape(nb, diag, 1, GLA_DK)
    gs4 = gc.reshape(nb, diag, 1, GLA_DK)
    shape = (nb, diag, diag, GLA_DK)
    si = lax.broadcasted_iota(jnp.int32, shape, 1)
    ti = lax.broadcasted_iota(jnp.int32, shape, 2)
    w = jnp.exp(jnp.where(si <= ti, g4 - gs4, NEG))
    m = (q4 * k4 * w).reshape(nb * diag * diag, GLA_DK)
    p = _dot(m.astype(BF16), expand).reshape(nb, diag, diag, GLA_DV)
    o = jnp.sum(p * v.reshape(nb, diag, 1, GLA_DV), axis=1)
    return o.reshape(rows, GLA_DV)


def _gla_compact(s):
    rowhead = _iota2((GLA_DK, GLA_HEAD_V), 0) // GLA_HEAD_K
    out = s[:, 0:GLA_HEAD_V]
    for h in range(1, GLA_HEADS):
        out = jnp.where(rowhead == h, s[:, h * GLA_HEAD_V:(h + 1) * GLA_HEAD_V], out)
    return out


def _gla_finish(o, gr, seg, nw):
    ms = _split2_dot(o * o, seg)
    return o * lax.rsqrt(ms + EPS) * nw * _silu(gr)


def _head_stack(x, head_dim, heads):
    lane = _iota2(x.shape, 1) // head_dim
    return jnp.concatenate([jnp.where(lane == h, x, 0.0) for h in range(heads)], axis=0)


def _gla_prompt_kernel(gq_ref, gk_ref, gv_ref, gr_ref, small_ref, wg_ref, bg_ref, nw_ref, ex_ref, seg_ref,
                       o_ref, st_ref, s_scr):
    c = pl.program_id(1)
    rows = GLA_CHUNK

    @pl.when(c == 0)
    def _():
        s_scr[...] = jnp.zeros_like(s_scr)

    qi = _iota2((rows, rows), 0)
    si = _iota2((rows, rows), 1)
    seg01 = jnp.where(si <= qi, 1.0, 0.0).astype(BF16)
    bd = (_iota2((GLA_DK, GLA_DV), 0) // GLA_HEAD_K) == (_iota2((GLA_DK, GLA_DV), 1) // GLA_HEAD_V)
    s_all = s_scr[...]
    nch = gq_ref.shape[0] // rows
    prep = []
    for ci in range(nch):
        r0 = ci * rows
        q, k, gc = _gla_prepare(gq_ref[r0:r0 + rows, :], gk_ref[r0:r0 + rows, :], small_ref[r0:r0 + rows, :],
                                wg_ref[...], bg_ref[...], seg01)
        v = gv_ref[r0:r0 + rows, :]
        prep.append((q, k, gc, v, v.astype(BF16)))
    levels = []
    half = rows // 2
    while half >= GLA_DIAG:
        levels.append(half)
        half //= 2
    atts = {}
    for ci, (q, k, gc, v, vb) in enumerate(prep):
        for half in levels:
            for blk in range(rows // (2 * half)):
                s0 = blk * 2 * half
                t0 = s0 + half
                ref = gc[t0 - 1:t0, :]
                qs = q[t0:t0 + half, :] * jnp.exp(gc[t0:t0 + half, :] - ref)
                ks = k[s0:t0, :] * jnp.exp(ref - gc[s0:t0, :])
                atts[ci, half, blk] = _dot_nt(_head_stack(qs, GLA_HEAD_K, GLA_HEADS).astype(BF16),
                                              ks.astype(BF16)).astype(BF16)
    outs = [_gla_pairwise(q, k, v, gc, ex_ref[...], GLA_DIAG) for q, k, gc, v, vb in prep]
    for ci, (q, k, gc, v, vb) in enumerate(prep):
        for half in levels:
            vlane = _iota2((half, GLA_DV), 1) // GLA_HEAD_V
            pieces = []
            for blk in range(rows // (2 * half)):
                s0 = blk * 2 * half
                pv = _dot(atts[ci, half, blk], vb[s0:s0 + half, :])
                ot = jnp.zeros((half, GLA_DV), F32)
                for h in range(GLA_HEADS):
                    ot = jnp.where(vlane == h, pv[h * half:(h + 1) * half, :], ot)
                pieces += [jnp.zeros((half, GLA_DV), F32), ot]
            outs[ci] = outs[ci] + jnp.concatenate(pieces, axis=0)
    for ci, (q, k, gc, v, vb) in enumerate(prep):
        outs[ci] = outs[ci] + _dot((q * jnp.exp(gc)).astype(BF16), s_all.astype(BF16))
        gc_t = gc.T
        dcol = gc_t[:, rows - 1:rows]
        kd_t = k.T * jnp.exp(dcol - gc_t)
        upd = _dot(kd_t.astype(BF16), vb)
        s_all = jnp.exp(dcol) * s_all + jnp.where(bd, upd, 0.0)
    o_ref[...] = _gla_finish(jnp.concatenate(outs, axis=0), gr_ref[...], seg_ref[...],
                             nw_ref[...]).astype(o_ref.dtype)
    s_scr[...] = s_all

    @pl.when(c == pl.num_programs(1) - 1)
    def _():
        st_ref[0] = _gla_compact(s_all)


def _gla_vec_specs():
    return [_const_spec((LANES, GLA_DK)), _const_spec((1, GLA_DK)), _const_spec((1, GLA_DV)),
            _const_spec((GLA_DK, GLA_DV)), _const_spec((GLA_DV, GLA_DV))]


def _gla_prompt(gq, gk, gv, gr, small, lp, b, l):
    nc = l // GLA_STEP
    rs = lambda w: pl.BlockSpec((GLA_STEP, w), lambda bi, ci: (bi * nc + ci, 0))
    expand, seg = _gla_consts()
    return pl.pallas_call(
        _gla_prompt_kernel,
        grid=(b, nc),
        in_specs=[rs(GLA_DK), rs(GLA_DK), rs(GLA_DV), rs(GLA_DV), rs(LANES)] + _gla_vec_specs(),
        out_specs=[rs(GLA_DV), pl.BlockSpec((1, GLA_DK, GLA_HEAD_V), lambda bi, ci: (bi, 0, 0))],
        out_shape=[jax.ShapeDtypeStruct((b * l, GLA_DV), BF16),
                   jax.ShapeDtypeStruct((b, GLA_DK, GLA_HEAD_V), F32)],
        scratch_shapes=[pltpu.VMEM((GLA_DK, GLA_DV), F32)],
        compiler_params=_params(("arbitrary", "arbitrary")),
        name="gla_prompt",
    )(gq, gk, gv, gr, small, lp["gla_wg"], lp["gla_bg"], lp["gla_nw"], expand, seg)


def _gla_sample_kernel(gq_ref, gk_ref, gv_ref, gr_ref, small_ref, s0_ref, wg_ref, bg_ref, nw_ref, ex_ref, seg_ref,
                       o_ref, st_ref, *, seq):
    rows = gq_ref.shape[0]
    nseq = rows // seq
    qi = _iota2((rows, rows), 0)
    si = _iota2((rows, rows), 1)
    seg01 = jnp.where(((qi // seq) == (si // seq)) & (si <= qi), 1.0, 0.0).astype(BF16)
    last01 = jnp.where(si == (qi // seq) * seq + (seq - 1), 1.0, 0.0).astype(BF16)
    q, k, gc = _gla_prepare(gq_ref[...], gk_ref[...], small_ref[...], wg_ref[...], bg_ref[...], seg01)
    v = gv_ref[...]
    vb = v.astype(BF16)
    gc_last = _split3_dot(last01, gc)
    qg = (q * jnp.exp(gc)).astype(BF16)
    kd_t = (k * jnp.exp(gc_last - gc)).T
    dec_t = jnp.exp(gc_last).T
    colseq = _iota2((GLA_DK, rows), 1) // seq
    rowseq = _iota2((rows, GLA_DV), 0) // seq
    bd = (_iota2((GLA_DK, GLA_DV), 0) // GLA_HEAD_K) == (_iota2((GLA_DK, GLA_DV), 1) // GLA_HEAD_V)
    o = _gla_pairwise(q, k, v, gc, ex_ref[...], seq)
    for b in range(nseq):
        s0 = jnp.where(bd, jnp.concatenate([s0_ref[b]] * GLA_HEADS, axis=1), 0.0)
        o = o + jnp.where(rowseq == b, _dot(qg, s0.astype(BF16)), 0.0)
        upd = _dot(jnp.where(colseq == b, kd_t, 0.0).astype(BF16), vb)
        st_ref[b] = _gla_compact(dec_t[:, b * seq:b * seq + 1] * s0 + jnp.where(bd, upd, 0.0))
    o_ref[...] = _gla_finish(o, gr_ref[...], seg_ref[...], nw_ref[...]).astype(o_ref.dtype)


def _gla_sample(gq, gk, gv, gr, small, s0, lp, b, l):
    rows = SAMPLE_SEQS * l
    rs = lambda w: pl.BlockSpec((rows, w), lambda i: (i, 0))
    st = pl.BlockSpec((SAMPLE_SEQS, GLA_DK, GLA_HEAD_V), lambda i: (i, 0, 0))
    expand, seg = _gla_consts()
    return pl.pallas_call(
        functools.partial(_gla_sample_kernel, seq=l),
        grid=(b // SAMPLE_SEQS,),
        in_specs=[rs(GLA_DK), rs(GLA_DK), rs(GLA_DV), rs(GLA_DV), rs(LANES), st] + _gla_vec_specs(),
        out_specs=[rs(GLA_DV), st],
        out_shape=[jax.ShapeDtypeStruct((b * l, GLA_DV), BF16),
                   jax.ShapeDtypeStruct((b, GLA_DK, GLA_HEAD_V), F32)],
        compiler_params=_params(("arbitrary",)),
        name="gla_sample",
    )(gq, gk, gv, gr, small, s0, lp["gla_wg"], lp["gla_bg"], lp["gla_nw"], expand, seg)


def _att_window(qs, ks, vs, valids, prev):
    lane_half = _iota2(qs[0].shape, 1) // ATT_HEAD_DIM
    heads = [(p, half) for p in range(len(qs)) for half in range(2)]
    scores = [_dot_nt(jnp.where(lane_half == half, qs[p], 0.0).astype(BF16), ks[p]) for p, half in heads]
    probs, stats = [], []
    for (p, half), s in zip(heads, scores):
        lse_prev = prev[p][0]
        s = jnp.where(valids[p], s, NEG)
        smax = jnp.max(s, axis=-1, keepdims=True)
        mn = jnp.broadcast_to(smax, qs[p].shape) if lse_prev is None else jnp.maximum(lse_prev[half], smax)
        pr = jnp.exp2(s - jnp.concatenate([mn] * (s.shape[1] // LANES), axis=1))
        psum = jnp.sum(pr, axis=-1, keepdims=True)
        if lse_prev is None:
            alpha, ln = None, jnp.broadcast_to(psum, qs[p].shape)
        else:
            alpha = jnp.exp2(lse_prev[half] - mn)
            ln = alpha + psum
        probs.append(pr.astype(BF16))
        stats.append((mn, ln, alpha))
    pvs = [_dot(pr, vs[p]) for (p, half), pr in zip(heads, probs)]
    new = []
    for p in range(len(qs)):
        a_prev = prev[p][1]
        (m0, l0, al0), (m1, l1, al1) = stats[2 * p], stats[2 * p + 1]
        a0 = pvs[2 * p] if al0 is None else a_prev * al0 + pvs[2 * p]
        a1 = pvs[2 * p + 1] if al1 is None else a_prev * al1 + pvs[2 * p + 1]
        new.append(([m0, m1], [l0, l1], jnp.where(lane_half == 0, a0, a1)))
    return new


def _att_prompt_kernel(q_ref, k_ref, v_ref, out_ref, acc_scr, lse_scr):
    tq = ATT_BLOCK
    sup = out_ref.shape[1]
    base = pl.program_id(1) * sup
    rel = tq + _iota2((tq, 2 * tq), 0) - _iota2((tq, 2 * tq), 1)
    band = (rel >= 0) & (rel <= ATT_KEYS - 1)
    in_cur = _iota2((tq, 2 * tq), 1) >= tq
    strides = sorted((d for _, d in DILATION_PATTERNS), reverse=True)
    for idx, d in enumerate(strides):
        first, last = idx == 0, idx == len(strides) - 1

        def rows(start, d=d):
            return pl.ds(start, tq, stride=d) if d > 1 else pl.ds(start, tq)

        def body(it, carry, d=d, first=first, last=last, rows=rows):
            npair = ATT_HEADS // 2
            locs, qs, ks, vs, valids, prev = [], [], [], [], [], []
            for j in range(ATT_UNROLL):
                sb = it * ATT_UNROLL + j
                if d > 1:
                    r = sb % d
                    mi = sb // d
                    loc = r + d * tq * mi
                    start_q = base + loc
                else:
                    mi = sb
                    loc = pl.multiple_of(sb * tq, tq)
                    start_q = pl.multiple_of(base + loc, tq)
                mglob = base // (d * tq) + mi
                start_p = jnp.where(mglob == 0, start_q, start_q - d * tq)
                valid = band & ((mglob > 0) | in_cur)
                locs.append(loc)
                for p in range(npair):
                    if first:
                        prev.append((None, None))
                    else:
                        prev.append(([lse_scr[2 * p + hf, rows(loc), :] for hf in range(2)],
                                     acc_scr[p, rows(loc), :]))
                    qs.append((q_ref[p, rows(start_q), :] * (ATT_HEAD_DIM ** -0.5 * math.log2(math.e))).astype(BF16))
                    ks.append(jnp.concatenate([k_ref[p, rows(start_p), :], k_ref[p, rows(start_q), :]],
                                              axis=0).astype(BF16))
                    vs.append(jnp.concatenate([v_ref[p, rows(start_p), :], v_ref[p, rows(start_q), :]],
                                              axis=0).astype(BF16))
                    valids.append(valid)
            new = _att_window(qs, ks, vs, valids, prev)
            lane_half = _iota2((tq, LANES), 1) // ATT_HEAD_DIM
            for j, loc in enumerate(locs):
                for p in range(npair):
                    m_new, l_new, a_new = new[j * npair + p]
                    norm = a_new / jnp.where(lane_half == 0, l_new[0], l_new[1])
                    if last:
                        out_ref[p, rows(loc), :] = norm.astype(out_ref.dtype)
                    else:
                        for hf in range(2):
                            lse_scr[2 * p + hf, rows(loc), :] = m_new[hf] + jnp.log2(l_new[hf])
                        acc_scr[p, rows(loc), :] = norm
            return carry

        lax.fori_loop(0, sup // (tq * ATT_UNROLL), body, 0)


def _att_prompt(aq, ak, av, b, l):
    npair = ATT_HEADS // 2
    sup = ATT_BLOCK * max(d for _, d in DILATION_PATTERNS)
    assert l % sup == 0
    seq = pl.BlockSpec((npair, l, LANES), lambda bi, j: (0, bi, 0))
    return pl.pallas_call(
        _att_prompt_kernel,
        grid=(b, l // sup),
        in_specs=[seq, seq, seq],
        out_specs=pl.BlockSpec((npair, sup, LANES), lambda bi, j: (0, bi * (l // sup) + j, 0)),
        out_shape=jax.ShapeDtypeStruct((npair, b * l, LANES), BF16),
        scratch_shapes=[pltpu.VMEM((npair, sup, LANES), F32), pltpu.VMEM((ATT_HEADS, sup, LANES), F32)],
        compiler_params=_params(("arbitrary", "arbitrary")),
        name="att_prompt",
    )(aq, ak, av)


def _att_counts(seq, nbuf):
    qpos = nbuf + np.arange(seq)[:, None]
    kpos = np.arange(nbuf + seq)[None, :]
    delta = qpos - kpos
    cnt = np.zeros(delta.shape, np.float32)
    for window, stride in DILATION_PATTERNS:
        cnt += ((delta >= 0) & (delta % stride == 0) & (delta <= window)).astype(np.float32)
    cnt = np.tile(cnt, (2, 1))
    new = np.zeros((2 * seq, LANES), np.float32)
    new[:, :seq] = cnt[:, nbuf:]
    return jnp.asarray(cnt[:, :nbuf]), jnp.asarray(new)


def _att_sample_kernel(q_ref, kn_ref, vn_ref, kc_ref, vc_ref, cc_ref, cn_ref, out_ref):
    nseq = kc_ref.shape[0]
    seq = q_ref.shape[1] // nseq
    cc = cc_ref[...]
    cn = cn_ref[...]
    pad = jnp.zeros((LANES - seq, LANES), F32)
    lane_half = _iota2((seq, LANES), 1) // ATT_HEAD_DIM
    for b in range(nseq):
        rows = slice(b * seq, (b + 1) * seq)
        for p in range(ATT_HEADS // 2):
            q = q_ref[p, rows, :] * (ATT_HEAD_DIM ** -0.5)
            q2 = jnp.concatenate([jnp.where(lane_half == 0, q, 0.0), jnp.where(lane_half == 1, q, 0.0)],
                                 axis=0).astype(BF16)
            kt = kc_ref[b, 2 * p:2 * p + 2].reshape(LANES, -1).astype(BF16)
            vt = vc_ref[b, 2 * p:2 * p + 2].reshape(LANES, -1).astype(BF16)
            kn = jnp.concatenate([kn_ref[p, rows, :], pad], axis=0).astype(BF16)
            vn = jnp.concatenate([vn_ref[p, rows, :], pad], axis=0).astype(BF16)
            sc = jnp.where(cc > 0, _dot(q2, kt), NEG)
            sn = jnp.where(cn > 0, _dot_nt(q2, kn), NEG)
            m = jnp.maximum(jnp.max(sc, axis=-1, keepdims=True), jnp.max(sn, axis=-1, keepdims=True))
            pc = cc * jnp.exp(sc - m)
            pn = cn * jnp.exp(sn - m)
            den = jnp.sum(pc, axis=-1, keepdims=True) + jnp.sum(pn, axis=-1, keepdims=True)
            o = (_dot_nt(pc.astype(BF16), vt) + _dot(pn.astype(BF16), vn)) / den
            out_ref[p, rows, :] = jnp.where(lane_half == 0, o[0:seq, :], o[seq:2 * seq, :]).astype(out_ref.dtype)


def _att_sample(aq, ak, av, kcache_t, vcache_t, layer, b, l):
    npair = ATT_HEADS // 2
    nbuf = kcache_t.shape[-1]
    cc, cn = _att_counts(l, nbuf)
    ns = ATT_SAMPLE_SEQS
    assert b % ns == 0
    new = pl.BlockSpec((npair, ns * l, LANES), lambda i: (0, i, 0))
    cache = pl.BlockSpec((None, ns, ATT_HEADS, ATT_HEAD_DIM, nbuf), lambda i: (layer, i, 0, 0, 0))
    return pl.pallas_call(
        _att_sample_kernel,
        grid=(b // ns,),
        in_specs=[new, new, new, cache, cache, _const_spec((2 * l, nbuf)), _const_spec((2 * l, LANES))],
        out_specs=new,
        out_shape=jax.ShapeDtypeStruct((npair, b * l, LANES), BF16),
        compiler_params=_params(("arbitrary",)),
        name="att_sample",
    )(aq, ak, av, kcache_t, vcache_t, cc, cn)


def _pad_lanes(v, width, offset=0):
    out = jnp.zeros((1, width), F32)
    return out.at[0, offset:offset + v.shape[0]].set(v.astype(F32))


def _wt_kernel(w_ref, dt_ref, glr_ref, o_ref):
    g = pl.program_id(0)
    last = pl.num_programs(0) - 1
    depth = o_ref.shape[0]

    @pl.when(g < last)
    def _():
        for l in range(depth):
            o_ref[l] = w_ref[:, l, :].T.astype(o_ref.dtype)

    @pl.when(g == last)
    def _():
        row = _iota2((SUBLANES, D_MODEL), 0)
        pad = jnp.zeros((LANES - SMALL_GLR_OFF - GLA_GATE_RANK, D_MODEL), F32)
        for l in range(depth):
            dt = jnp.where(row < SSD_HEADS, dt_ref[:, l, :], 0.0)
            o_ref[l] = jnp.concatenate([dt, glr_ref[:, l, :], pad], axis=0).T.astype(o_ref.dtype)


def _proj_weights(w_in):
    depth = w_in.shape[0]
    assert SMALL_DT_OFF == 0 and SMALL_GLR_OFF == SUBLANES and SSD_HEADS <= SUBLANES
    wt = w_in.transpose(2, 0, 1)
    offs = dict(zip(("z", "xbc", "dt", "gq", "gk", "gv", "gr", "glr", "aq", "ak", "av"),
                    np.concatenate([[0], np.cumsum(IN_SPLITS)])[:-1]))
    src = [int(offs[n]) + i * LANES for n, w in PROJ_GROUPS if n != "small" for i in range(w // LANES)]
    steps = [(g, src[g] - src[g - 1] - LANES) for g in range(1, len(src)) if src[g] - src[g - 1] != LANES]
    assert src[0] == 0 and PROJ_GROUPS[-1][0] == "small"

    def w_map(g):
        off = LANES * g
        for g0, delta in steps:
            off = off + jnp.where(g >= g0, delta, 0)
        return (jnp.where(g < len(src), off, 0), 0, 0)

    el = lambda n: (pl.Element(n), pl.Element(depth), pl.Element(D_MODEL))
    return pl.pallas_call(
        _wt_kernel,
        grid=(len(src) + 1,),
        in_specs=[pl.BlockSpec(el(LANES), w_map),
                  pl.BlockSpec(el(SUBLANES), lambda g: (int(offs["dt"]), 0, 0)),
                  pl.BlockSpec(el(GLA_GATE_RANK), lambda g: (int(offs["glr"]), 0, 0))],
        out_specs=pl.BlockSpec((depth, D_MODEL, LANES), lambda g: (0, 0, g)),
        out_shape=jax.ShapeDtypeStruct((depth, D_MODEL, PROJ_WIDTH), BF16),
        compiler_params=_params(("arbitrary",)),
        name="proj_weights",
    )(wt, wt, wt)


def _layer_params(l, ssd_conv_w, ssd_conv_b, ssd_dt_bias, ssd_a_log, ssd_d, ssd_norm_w,
                  gla_w_gate, gla_b_gate, gla_norm_w, norm_w):
    wg = jnp.zeros((LANES, GLA_DK), F32).at[SMALL_GLR_OFF:SMALL_GLR_OFF + GLA_GATE_RANK, :].set(gla_w_gate[l])
    return dict(
        conv_w=ssd_conv_w[l], conv_b=ssd_conv_b[l].reshape(1, SSD_CONV_DIM),
        dtb=_pad_lanes(ssd_dt_bias[l], LANES, SMALL_DT_OFF), alog=_pad_lanes(ssd_a_log[l], LANES, SMALL_DT_OFF),
        dvec=jnp.repeat(ssd_d[l].astype(F32), SSD_HEAD_DIM).reshape(1, SSD_INNER),
        ssd_nw=ssd_norm_w[l].reshape(1, SSD_INNER),
        gla_wg=wg.astype(BF16), gla_bg=gla_b_gate[l].reshape(1, GLA_DK),
        gla_nw=jnp.tile(gla_norm_w[l], GLA_HEADS).reshape(1, GLA_DV),
        norm_w=norm_w[l],
    )


def _trunk(x, mods, layers, stacked, norm_f, states, sample):
    b, l, _ = x.shape
    depth = len(layers)
    keep = min(ATT_MAX_WINDOW, l)
    names = [n for n, _ in PROJ_GROUPS]
    ssd_list, conv_list, gla_list, k_list, v_list = [], [], [], [], []
    ssd_stack, kv_stack = None, None
    for li, lp in enumerate(layers):
        mod = mods
        x = _ffn(x, mod, 0, lp["norm_w"][0], stacked["ffn1_in"], stacked["ffn1_out"], li)
        if sample:
            proj = dict(zip(names, _inproj(x, mod, lp["norm_w"][1], stacked["w_in_p"], li)))
            st_ssd, st_conv, st_gla, kcache, vcache = states
            bufp = jnp.pad(st_conv[li], ((0, 0), (0, l - (SSD_CONV - 1)), (0, 0))).reshape(b * l, SSD_CONV_DIM)
            y, ssd_stack = _ssd_sample(proj["z"], proj["xbc"], proj["small"], bufp, st_ssd, li, lp, b, l,
                                       new_prev=ssd_stack)
            o, gla_new = _gla_sample(proj["gq"], proj["gk"], proj["gv"], proj["gr"], proj["small"],
                                     st_gla[li].reshape(b, GLA_DK, GLA_HEAD_V), lp, b, l)
            att = _att_sample(proj["aq"], proj["ak"], proj["av"], kcache, vcache, li, b, l)
            for acc, name in ((k_list, "ak"), (v_list, "av")):
                kv = proj[name].reshape(ATT_HEADS // 2, b, l, 2, ATT_HEAD_DIM)[:, :, l - keep:]
                acc.append(kv.transpose(1, 2, 0, 3, 4).reshape(b, keep, ATT_HEADS, ATT_HEAD_DIM))
        else:
            res = _inproj(x, mod, lp["norm_w"][1], stacked["w_in_p"], li, keep_t=keep, kv_prev=kv_stack)
            proj = dict(zip(names, res))
            kv_stack = tuple(res[len(names):])
            y, ssd_new = _ssd_prompt(proj["z"], proj["xbc"], proj["small"], lp, b, l)
            ssd_list.append(ssd_new)
            o, gla_new = _gla_prompt(proj["gq"], proj["gk"], proj["gv"], proj["gr"], proj["small"], lp, b, l)
            att = _att_prompt(proj["aq"], proj["ak"], proj["av"], b, l)
        x = _ffn(x, mod, 6, lp["norm_w"][2], stacked["ffn2_in"], stacked["ffn2_out"], li,
                 premix=(y, o, att, 5, stacked["w_out"]),
                 final_norm=norm_f if li == depth - 1 else None)
        conv_list.append(proj["xbc"].reshape(b, l, SSD_CONV_DIM)[:, l - (SSD_CONV - 1):])
        gla_list.append(gla_new.reshape(b, GLA_HEADS, GLA_HEAD_K, GLA_HEAD_V))
    if sample:
        ssd_out, k_out, v_out = ssd_stack, jnp.stack(k_list), jnp.stack(v_list)
    else:
        ssd_out = jnp.stack(ssd_list)
        k_out, v_out = (t.reshape(depth, b, ATT_HEADS, ATT_HEAD_DIM, keep).transpose(0, 1, 4, 2, 3) for t in kv_stack)
    return x, [ssd_out, jnp.stack(conv_list), jnp.stack(gla_list), k_out, v_out]


def kernel(x_prompt, x_sample, c_prompt, c_sample, state_ssd, state_ssd_conv, state_gla, cache_attn_k, cache_attn_v,
           w_in, w_out, ssd_conv_w, ssd_conv_b, ssd_dt_bias, ssd_a_log, ssd_d, ssd_norm_w,
           gla_w_gate, gla_b_gate, gla_norm_w, norm_w, w_mod, b_mod,
           ffn1_w_in, ffn1_w_out, ffn2_w_in, ffn2_w_out, norm_f):
    bp, bs = x_prompt.shape[0], x_sample.shape[0]
    depth = w_in.shape[0]
    layers = [_layer_params(l, ssd_conv_w, ssd_conv_b, ssd_dt_bias, ssd_a_log, ssd_d, ssd_norm_w,
                            gla_w_gate, gla_b_gate, gla_norm_w, norm_w) for l in range(depth)]
    stacked = dict(w_in_p=_proj_weights(w_in), w_out=w_out.astype(BF16), ffn1_in=ffn1_w_in.astype(BF16), ffn1_out=ffn1_w_out.astype(BF16),
                   ffn2_in=ffn2_w_in.astype(BF16), ffn2_out=ffn2_w_out.astype(BF16))
    mods_s, mods_p = _modulation([c_sample, c_prompt], w_mod, b_mod)
    kcache = cache_attn_k.transpose(0, 1, 3, 4, 2)
    vcache = cache_attn_v.transpose(0, 1, 3, 4, 2)
    y_p, (ssd_p, conv_p, gla_p, k_p, v_p) = _trunk(x_prompt, mods_p, layers, stacked, norm_f, None, sample=False)
    y_s, (ssd_s, conv_s, gla_s, k_s, v_s) = _trunk(
        x_sample, mods_s, layers, stacked, norm_f, (state_ssd, state_ssd_conv, state_gla, kcache, vcache),
        sample=True)
    return (y_p, y_s, ssd_p, ssd_s, conv_p, conv_s, gla_p, gla_s, k_p, k_s, v_p, v_s)
```

```python
import functools
import math

import numpy as np
import jax
import jax.numpy as jnp
from jax import lax
from jax.experimental import pallas as pl
from jax.experimental.pallas import tpu as pltpu

F32 = jnp.float32
BF16 = jnp.bfloat16

D_MODEL = 1024
DEPTH = 2
SSD_HEADS = 6
SSD_HEAD_DIM = 64
SSD_INNER = SSD_HEADS * SSD_HEAD_DIM
SSD_GROUPS = 2
SSD_STATE = 128
SSD_CONV = 4
SSD_CONV_DIM = SSD_INNER + 2 * SSD_GROUPS * SSD_STATE
GLA_HEADS = 4
GLA_HEAD_K = 32
GLA_HEAD_V = 64
GLA_DK = GLA_HEADS * GLA_HEAD_K
GLA_DV = GLA_HEADS * GLA_HEAD_V
GLA_GATE_RANK = 16
GLA_TAU = 16.0
ATT_HEADS = 6
ATT_HEAD_DIM = 64
ATT_DIM = ATT_HEADS * ATT_HEAD_DIM
DILATION_PATTERNS = ((128, 1), (512, 4), (2048, 16))
ATT_MAX_WINDOW = 2048
ATT_KEYS = 129
D_MIX = SSD_INNER + GLA_DV + ATT_DIM
IN_SPLITS = (SSD_INNER, SSD_CONV_DIM, SSD_HEADS, GLA_DK, GLA_DK, GLA_DV, GLA_DV, GLA_GATE_RANK,
             ATT_DIM, ATT_DIM, ATT_DIM)
D_FF = 2816
ADALN_MODS = 9
FFN_RES = 0.5
EPS = 1e-6

LANES = 128
SUBLANES = 8
VMEM_LIMIT = 56 * 1024 * 1024

PROJ_GROUPS = (("z", SSD_INNER), ("xbc", SSD_CONV_DIM), ("gq", GLA_DK), ("gk", GLA_DK), ("gv", GLA_DV),
               ("gr", GLA_DV), ("aq", ATT_DIM), ("ak", ATT_DIM), ("av", ATT_DIM), ("small", LANES))
PROJ_WIDTH = sum(w for _, w in PROJ_GROUPS)
PAIR_MAJOR = ("aq", "ak", "av")
KV_T = ("ak", "av")
SMALL_DT_OFF = 0
SMALL_GLR_OFF = 8

ROW_TILE = 512
FF_CHUNK = 256
SSD_CHUNK = 128
SSD_STEP = 512
GLA_CHUNK = 128
GLA_STEP = 512
GLA_DIAG = 16
SAMPLE_SEQS = 16
ATT_BLOCK = 128
ATT_SAMPLE_SEQS = 2
ATT_UNROLL = 2
NEG = -1e30


def _dot(a, b):
    return jnp.dot(a, b, preferred_element_type=F32)


def _dot_nt(a, b):
    return lax.dot_general(a, b, (((1,), (1,)), ((), ())), preferred_element_type=F32)


def _sigmoid(x):
    return 1.0 / (1.0 + jnp.exp(-x))


def _silu(x):
    return x * _sigmoid(x)


def _softplus(x):
    return jnp.maximum(x, 0.0) + jnp.log1p(jnp.exp(-jnp.abs(x)))


def _split3_dot(m01, a):
    a1 = a.astype(BF16)
    r1 = a - a1.astype(F32)
    a2 = r1.astype(BF16)
    a3 = (r1 - a2.astype(F32)).astype(BF16)
    return _dot(m01, a1) + _dot(m01, a2) + _dot(m01, a3)


def _split2_dot(a, m01):
    a1 = a.astype(BF16)
    a2 = (a - a1.astype(F32)).astype(BF16)
    return _dot(a1, m01) + _dot(a2, m01)


def _rms_mod(x, nw, shift, scale):
    ms = jnp.mean(x * x, axis=-1, keepdims=True)
    y = x * lax.rsqrt(ms + EPS) * nw
    return y * (1.0 + scale) + shift


def _iota2(shape, axis):
    return lax.broadcasted_iota(jnp.int32, shape, axis)


def _params(sem):
    return pltpu.CompilerParams(dimension_semantics=sem, vmem_limit_bytes=VMEM_LIMIT)


def _const_spec(shape, layer=None):
    nd = len(shape)
    if layer is None:
        return pl.BlockSpec(shape, lambda *_: (0,) * nd, pipeline_mode=pl.Buffered(1))
    return pl.BlockSpec((None,) + tuple(shape), lambda *_: (layer,) + (0,) * nd, pipeline_mode=pl.Buffered(1))


def _mod_kernel(c_ref, w_ref, b_ref, *o_refs):
    c = c_ref[...]
    res = _dot(_silu(c).astype(BF16), w_ref[...].astype(BF16)) + b_ref[...]
    row = 0
    for o_ref in o_refs:
        n = o_ref.shape[0]
        o_ref[...] = res[row:row + n, :].reshape(n, 1, res.shape[1])
        row += -(-n // SUBLANES) * SUBLANES


def _modulation(c_groups, w_mod, b_mod):
    d = c_groups[0].shape[1]
    depth, _, nout = w_mod.shape
    padded = [jnp.pad(c, ((0, -c.shape[0] % SUBLANES), (0, 0))) for c in c_groups]
    c_all = jnp.concatenate(padded, axis=0)
    n = c_all.shape[0]
    return pl.pallas_call(
        _mod_kernel,
        grid=(depth, nout // d),
        in_specs=[pl.BlockSpec((n, d), lambda l, j: (0, 0)),
                  pl.BlockSpec((None, d, d), lambda l, j: (l, 0, j)),
                  pl.BlockSpec((None, 1, d), lambda l, j: (l, 0, j))],
        out_specs=[pl.BlockSpec((None, None, c.shape[0], 1, d), lambda l, j: (l, j, 0, 0, 0)) for c in c_groups],
        out_shape=[jax.ShapeDtypeStruct((depth, nout // d, c.shape[0], 1, d), F32) for c in c_groups],
        compiler_params=_params(("arbitrary", "arbitrary")),
        name="adaln_mod",
    )(c_all, w_mod, b_mod.reshape(depth, 1, nout))


def _row_tiling(b, l):
    if l >= ROW_TILE:
        assert l % ROW_TILE == 0
        return 1, ROW_TILE
    assert ROW_TILE % l == 0 and b % (ROW_TILE // l) == 0
    return ROW_TILE // l, l


def _x_spec(bb, ll, nlb):
    return pl.BlockSpec((bb, ll, D_MODEL), lambda i: (i // nlb, i % nlb, 0))


def _mod_spec(k, bb, nlb, layer):
    return pl.BlockSpec((None, 1, bb, 1, D_MODEL), lambda i: (layer, k, i // nlb, 0, 0))


def _rows_spec(r, width):
    return pl.BlockSpec((r, width), lambda i: (i, 0))


def _pair_spec(r, width):
    return pl.BlockSpec((width // LANES, r, LANES), lambda i: (0, i, 0))


def _ffn_kernel(*refs, premix, final):
    refs = list(refs)
    x_ref = refs.pop(0)
    if premix:
        y_ref, o_ref, a_ref, g2_ref, wo_ref = refs[:5]
        refs = refs[5:]
    sh_ref, sc_ref, gt_ref, nw_ref, win_ref, wout_ref = refs[:6]
    refs = refs[6:]
    if final:
        nf_ref = refs.pop(0)
    out_ref, act_ref = refs
    bb, ll, d = x_ref.shape
    r = bb * ll
    x = x_ref[...]
    if premix:
        mixed = jnp.concatenate([y_ref[...], o_ref[...]] + [a_ref[p] for p in range(ATT_DIM // LANES)], axis=1)
        mix = _dot(mixed, wo_ref[...])
        x = x + g2_ref[0] * mix.reshape(bb, ll, d)
    h = _rms_mod(x, nw_ref[...], sh_ref[0], sc_ref[0]).reshape(r, d).astype(BF16)
    for c in range(D_FF // FF_CHUNK):
        g = _dot(h, win_ref[:, c * FF_CHUNK:(c + 1) * FF_CHUNK])
        u = _dot(h, win_ref[:, D_FF + c * FF_CHUNK:D_FF + (c + 1) * FF_CHUNK])
        act_ref[:, c * FF_CHUNK:(c + 1) * FF_CHUNK] = (_silu(g) * u).astype(BF16)
    y = _dot(act_ref[...], wout_ref[...])
    x = x + FFN_RES * gt_ref[0] * y.reshape(bb, ll, d)
    if final:
        ms = jnp.mean(x * x, axis=-1, keepdims=True)
        x = x * lax.rsqrt(ms + EPS) * nf_ref[...]
    out_ref[...] = x


def _ffn(x, mod, mod_base, norm_w, w_in, w_out, layer, premix=None, final_norm=None):
    b, l, d = x.shape
    bb, ll = _row_tiling(b, l)
    nlb = l // ll
    r = bb * ll
    nsteps = (b // bb) * nlb
    args, specs = [x], [_x_spec(bb, ll, nlb)]
    if premix is not None:
        y, o, a, gate_row, wo = premix
        args += [y, o, a, mod, wo]
        specs += [_rows_spec(r, SSD_INNER), _rows_spec(r, GLA_DV), _pair_spec(r, ATT_DIM),
                  _mod_spec(gate_row, bb, nlb, layer), _const_spec((D_MIX, d), layer)]
    args += [mod, mod, mod, norm_w.reshape(1, d), w_in, w_out]
    specs += [_mod_spec(mod_base, bb, nlb, layer), _mod_spec(mod_base + 1, bb, nlb, layer),
              _mod_spec(mod_base + 2, bb, nlb, layer),
              _const_spec((1, d)), _const_spec((d, 2 * D_FF), layer), _const_spec((D_FF, d), layer)]
    if final_norm is not None:
        args.append(final_norm.reshape(1, d))
        specs.append(_const_spec((1, d)))
    return pl.pallas_call(
        functools.partial(_ffn_kernel, premix=premix is not None, final=final_norm is not None),
        grid=(nsteps,),
        in_specs=specs,
        out_specs=_x_spec(bb, ll, nlb),
        out_shape=jax.ShapeDtypeStruct((b, l, d), F32),
        scratch_shapes=[pltpu.VMEM((r, D_FF), BF16)],
        compiler_params=_params(("arbitrary",)),
        name="ffn",
    )(*args)


def _inproj_kernel(x_ref, sh_ref, sc_ref, nw_ref, w_ref, *out_refs, first_kept, n_alias):
    out_refs = out_refs[n_alias:]
    bb, ll, d = x_ref.shape
    h = _rms_mod(x_ref[...], nw_ref[...], sh_ref[0], sc_ref[0]).reshape(bb * ll, d).astype(BF16)
    results, start, run = {}, 0, []
    for name, width in PROJ_GROUPS:
        run.append((name, width))
        total = sum(w for _, w in run)
        if total % (2 * LANES) == 0:
            big = _dot(h, w_ref[:, start:start + total])
            o = 0
            for n, w in run:
                results[n] = big[:, o:o + w]
                o += w
            start, run = start + total, []
    assert not run
    for ref, (name, width) in zip(out_refs, PROJ_GROUPS):
        res = results[name]
        if name in PAIR_MAJOR:
            for p in range(width // LANES):
                ref[p] = res[:, p * LANES:(p + 1) * LANES]
        else:
            ref[...] = res
        if first_kept is not None and name in KV_T:
            t_ref = out_refs[len(PROJ_GROUPS) + KV_T.index(name)]

            @pl.when(pl.program_id(0) % first_kept[1] >= first_kept[0])
            def _(t_ref=t_ref, res=res):
                rt = res.T
                if len(t_ref.shape) == 4:
                    for lyr in range(t_ref.shape[0]):
                        t_ref[lyr, 0] = rt
                else:
                    t_ref[0] = rt


def _inproj(x, mod, norm_w, w_in_p, layer, keep_t=None, kv_prev=None):
    b, l, d = x.shape
    bb, ll = _row_tiling(b, l)
    nlb = l // ll
    r = bb * ll
    out_specs = [_pair_spec(r, w) if n in PAIR_MAJOR else _rows_spec(r, w) for n, w in PROJ_GROUPS]
    out_shape = [jax.ShapeDtypeStruct((w // LANES, b * l, LANES) if n in PAIR_MAJOR else (b * l, w), F32)
                 for n, w in PROJ_GROUPS]
    first_kept = None
    if keep_t is not None:
        assert bb == 1 and keep_t % ll == 0 and keep_t <= l
        skip = nlb - keep_t // ll
        first_kept = (skip, nlb)
        depth = w_in_p.shape[0]
        pos = lambda i: jnp.maximum(i % nlb - skip, 0)
        if kv_prev is None:
            t_spec = pl.BlockSpec((depth, 1, ATT_DIM, ll), lambda i: (0, i // nlb, 0, pos(i)))
        else:
            t_spec = pl.BlockSpec((None, 1, ATT_DIM, ll), lambda i: (layer, i // nlb, 0, pos(i)))
        out_specs += [t_spec] * len(KV_T)
        out_shape += [jax.ShapeDtypeStruct((depth, b, ATT_DIM, keep_t), F32)] * len(KV_T)
    args = [x, mod, mod, norm_w.reshape(1, d), w_in_p]
    in_specs = [_x_spec(bb, ll, nlb), _mod_spec(3, bb, nlb, layer), _mod_spec(4, bb, nlb, layer),
                _const_spec((1, d)), _const_spec((d, PROJ_WIDTH), layer)]
    aliases = {}
    if kv_prev is not None:
        for j, arr in enumerate(kv_prev):
            aliases[len(args)] = len(PROJ_GROUPS) + j
            args.append(arr)
            in_specs.append(pl.BlockSpec(memory_space=pl.ANY))
    return pl.pallas_call(
        functools.partial(_inproj_kernel, first_kept=first_kept, n_alias=len(aliases)),
        grid=((b // bb) * nlb,),
        in_specs=in_specs,
        out_specs=out_specs,
        out_shape=out_shape,
        input_output_aliases=aliases,
        compiler_params=_params(("arbitrary",)),
        name="inproj",
    )(*args)


def _ssd_prepare(conv, small, dtb, alog, seg01, seglast01):
    xc = _silu(conv)
    xs = xc[:, 0:SSD_INNER]
    bm = xc[:, SSD_INNER:SSD_INNER + SSD_GROUPS * SSD_STATE]
    cm = xc[:, SSD_INNER + SSD_GROUPS * SSD_STATE:SSD_CONV_DIM]
    dt = _softplus(small + dtb)
    a = dt * (-jnp.exp(alog))
    cum = _split3_dot(seg01, a)
    if seglast01 is None:
        cum_last = jnp.broadcast_to(cum[cum.shape[0] - 1:, :], cum.shape)
    else:
        cum_last = _split3_dot(seglast01, cum)
    return xs, bm, cm, dt, cum, cum_last


def _ssd_diag(xs, bm, cm, dt, cum, cum_t, mask):
    ydiag, xdt = [], []
    cb = [_dot_nt(cm[:, g * SSD_STATE:(g + 1) * SSD_STATE].astype(BF16),
                  bm[:, g * SSD_STATE:(g + 1) * SSD_STATE].astype(BF16)) for g in range(SSD_GROUPS)]
    for h in range(SSD_HEADS):
        g = h // (SSD_HEADS // SSD_GROUPS)
        diff = cum[:, h:h + 1] - cum_t[h:h + 1, :]
        decay = jnp.exp(jnp.where(mask, diff, NEG))
        xh = xs[:, h * SSD_HEAD_DIM:(h + 1) * SSD_HEAD_DIM] * dt[:, h:h + 1]
        xdt.append(xh)
        ydiag.append(_dot((cb[g] * decay).astype(BF16), xh.astype(BF16)))
    return ydiag, xdt


def _ssd_finish(y, xs, z, dvec, normw):
    y = (y + dvec * xs) * _silu(z)
    sq = y * y
    half = SSD_INNER // SSD_GROUPS
    lane = _iota2(y.shape, 1)
    s0 = jnp.sum(jnp.where(lane < half, sq, 0.0), axis=-1, keepdims=True)
    s1 = jnp.sum(jnp.where(lane >= half, sq, 0.0), axis=-1, keepdims=True)
    ms = jnp.where(lane < half, s0, s1) * (1.0 / half)
    return y * lax.rsqrt(ms + EPS) * normw


def _ssd_prompt_kernel(z_ref, xbc_ref, small_ref, cw_ref, cb_ref, dtb_ref, alog_ref, dvec_ref, nw_ref,
                       y_ref, st_ref, h_scr, tail_scr):
    c = pl.program_id(1)
    step = xbc_ref.shape[0]
    rows = SSD_CHUNK
    hpg = SSD_HEADS // SSD_GROUPS

    @pl.when(c == 0)
    def _():
        h_scr[...] = jnp.zeros_like(h_scr)
        tail_scr[0:SUBLANES, :] = jnp.zeros((SUBLANES, SSD_CONV_DIM), F32)

    u = xbc_ref[...]
    tail_scr[SUBLANES:SUBLANES + step, :] = u
    conv = u * cw_ref[SSD_CONV - 1:SSD_CONV, :] + cb_ref[...]
    for j in range(1, SSD_CONV):
        conv = conv + tail_scr[SUBLANES - j:SUBLANES - j + step, :] * cw_ref[SSD_CONV - 1 - j:SSD_CONV - j, :]
    tail_scr[0:SUBLANES, :] = u[step - SUBLANES:, :]

    qi = _iota2((rows, rows), 0)
    si = _iota2((rows, rows), 1)
    mask = si <= qi
    seg01 = jnp.where(mask, 1.0, 0.0).astype(BF16)
    prep, diag, local = [], [], []
    for r0 in range(0, step, rows):
        prep.append(_ssd_prepare(conv[r0:r0 + rows, :], small_ref[r0:r0 + rows, :], dtb_ref[...], alog_ref[...],
                                 seg01, None))
    for xs, bm, cm, dt, cum, cum_last in prep:
        diag.append(_ssd_diag(xs, bm, cm, dt, cum, cum.T, mask))
    for (xs, bm, cm, dt, cum, cum_last), (ydiag, xdt) in zip(prep, diag):
        to_end_t = jnp.exp(cum_last - cum).T
        xdt_t = jnp.concatenate(xdt, axis=1).T
        s_local = []
        for h in range(SSD_HEADS):
            xw_t = xdt_t[h * SSD_HEAD_DIM:(h + 1) * SSD_HEAD_DIM, :] * to_end_t[h:h + 1, :]
            bg = bm[:, (h // hpg) * SSD_STATE:(h // hpg + 1) * SSD_STATE].astype(BF16)
            s_local.append(_dot(xw_t.astype(BF16), bg))
        local.append(s_local)
    hs = [h_scr[h] for h in range(SSD_HEADS)]
    for ci, ((xs, bm, cm, dt, cum, cum_last), (ydiag, xdt)) in enumerate(zip(prep, diag)):
        ys = []
        for h in range(SSD_HEADS):
            cg = cm[:, (h // hpg) * SSD_STATE:(h // hpg + 1) * SSD_STATE].astype(BF16)
            yoff = _dot_nt(cg, hs[h].astype(BF16)) * jnp.exp(cum[:, h:h + 1])
            ys.append(ydiag[h] + yoff)
            hs[h] = jnp.exp(cum_last[0:1, h:h + 1]) * hs[h] + local[ci][h]
        r0 = ci * rows
        y = _ssd_finish(jnp.concatenate(ys, axis=1), xs, z_ref[r0:r0 + rows, :], dvec_ref[...], nw_ref[...])
        y_ref[r0:r0 + rows, :] = y.astype(y_ref.dtype)
    for h in range(SSD_HEADS):
        h_scr[h] = hs[h]

    @pl.when(c == pl.num_programs(1) - 1)
    def _():
        for h in range(SSD_HEADS):
            st_ref[0, h] = hs[h]


def _ssd_vec_specs():
    return [_const_spec((SSD_CONV, SSD_CONV_DIM)), _const_spec((1, SSD_CONV_DIM)), _const_spec((1, LANES)),
            _const_spec((1, LANES)), _const_spec((1, SSD_INNER)), _const_spec((1, SSD_INNER))]


def _ssd_prompt(z, xbc, small, lp, b, l):
    nc = l // SSD_STEP
    rs = lambda w: pl.BlockSpec((SSD_STEP, w), lambda bi, ci: (bi * nc + ci, 0))
    return pl.pallas_call(
        _ssd_prompt_kernel,
        grid=(b, nc),
        in_specs=[rs(SSD_INNER), rs(SSD_CONV_DIM), rs(LANES)] + _ssd_vec_specs(),
        out_specs=[rs(SSD_INNER),
                   pl.BlockSpec((1, SSD_HEADS, SSD_HEAD_DIM, SSD_STATE), lambda bi, ci: (bi, 0, 0, 0))],
        out_shape=[jax.ShapeDtypeStruct((b * l, SSD_INNER), BF16),
                   jax.ShapeDtypeStruct((b, SSD_HEADS, SSD_HEAD_DIM, SSD_STATE), F32)],
        scratch_shapes=[pltpu.VMEM((SSD_HEADS, SSD_HEAD_DIM, SSD_STATE), F32),
                        pltpu.VMEM((SUBLANES + SSD_STEP, SSD_CONV_DIM), F32)],
        compiler_params=_params(("arbitrary", "arbitrary")),
        name="ssd_prompt",
    )(z, xbc, small, lp["conv_w"], lp["conv_b"], lp["dtb"], lp["alog"], lp["dvec"], lp["ssd_nw"])


def _ssd_sample_kernel(z_ref, xbc_ref, small_ref, buf_ref, h0_ref, cw_ref, cb_ref, dtb_ref, alog_ref, dvec_ref,
                       nw_ref, *rest, seq):
    y_ref, st_ref = rest[-2:]
    st_slabs = [st_ref.at[lyr] for lyr in range(st_ref.shape[0])] if len(st_ref.shape) == 5 else [st_ref]
    rows = xbc_ref.shape[0]
    nseq = rows // seq
    u = xbc_ref[...]
    bufp = buf_ref[...]
    tpos = _iota2((rows, SSD_CONV_DIM), 0) % seq
    conv = u * cw_ref[SSD_CONV - 1:SSD_CONV, :] + cb_ref[...]
    for j in range(1, SSD_CONV):
        uj = pltpu.roll(u, j, axis=0)
        back = (rows - (SSD_CONV - 1 - j)) % rows
        bj = pltpu.roll(bufp, back, axis=0) if back else bufp
        conv = conv + jnp.where(tpos < j, bj, uj) * cw_ref[SSD_CONV - 1 - j:SSD_CONV - j, :]

    qi = _iota2((rows, rows), 0)
    si = _iota2((rows, rows), 1)
    same = (qi // seq) == (si // seq)
    mask = same & (si <= qi)
    seg01 = jnp.where(mask, 1.0, 0.0).astype(BF16)
    last01 = jnp.where(si == (qi // seq) * seq + (seq - 1), 1.0, 0.0).astype(BF16)
    xs, bm, cm, dt, cum, cum_last = _ssd_prepare(conv, small_ref[...], dtb_ref[...], alog_ref[...], seg01, last01)
    cum_t = cum.T
    cum_last_t = cum_last.T
    ydiag, xdt = _ssd_diag(xs, bm, cm, dt, cum, cum_t, mask)
    to_end_t = jnp.exp(cum_last_t - cum_t)
    xdt_t = jnp.concatenate(xdt, axis=1).T
    hpg = SSD_HEADS // SSD_GROUPS
    grows = hpg * SSD_HEAD_DIM
    colseq = _iota2((grows, rows), 1) // seq
    yoff_t = []
    for g in range(SSD_GROUPS):
        cg = cm[:, g * SSD_STATE:(g + 1) * SSD_STATE].astype(BF16)
        bg = bm[:, g * SSD_STATE:(g + 1) * SSD_STATE].astype(BF16)
        xw_t = jnp.concatenate(
            [xdt_t[h * SSD_HEAD_DIM:(h + 1) * SSD_HEAD_DIM, :] * to_end_t[h:h + 1, :]
             for h in range(g * hpg, (g + 1) * hpg)], axis=0)
        acc = jnp.zeros((grows, rows), F32)
        for b in range(nseq):
            h0 = h0_ref[b, g * hpg:(g + 1) * hpg].reshape(grows, SSD_STATE)
            acc = jnp.where(colseq == b, _dot_nt(h0.astype(BF16), cg), acc)
            s_local = _dot(jnp.where(colseq == b, xw_t, 0.0).astype(BF16), bg)
            for hh in range(hpg):
                h = g * hpg + hh
                dec = jnp.exp(cum_last_t[h:h + 1, b * seq:b * seq + 1])
                h_new = (dec * h0[hh * SSD_HEAD_DIM:(hh + 1) * SSD_HEAD_DIM, :]
                         + s_local[hh * SSD_HEAD_DIM:(hh + 1) * SSD_HEAD_DIM, :])
                for slab in st_slabs:
                    slab[b, h] = h_new
        for hh in range(hpg):
            h = g * hpg + hh
            yoff_t.append(acc[hh * SSD_HEAD_DIM:(hh + 1) * SSD_HEAD_DIM, :] * jnp.exp(cum_t[h:h + 1, :]))
    yoff = jnp.concatenate(yoff_t, axis=0).T
    y = _ssd_finish(jnp.concatenate(ydiag, axis=1) + yoff, xs, z_ref[...], dvec_ref[...], nw_ref[...])
    y_ref[...] = y.astype(y_ref.dtype)


def _ssd_sample(z, xbc, small, bufp, h0_all, layer, lp, b, l, new_prev=None):
    depth = h0_all.shape[0]
    rows = SAMPLE_SEQS * l
    rs = lambda w: pl.BlockSpec((rows, w), lambda i: (i, 0))
    tile = (SAMPLE_SEQS, SSD_HEADS, SSD_HEAD_DIM, SSD_STATE)
    st_in = pl.BlockSpec((None,) + tile, lambda i: (layer, i, 0, 0, 0))
    args = [z, xbc, small, bufp, h0_all, lp["conv_w"], lp["conv_b"], lp["dtb"], lp["alog"], lp["dvec"], lp["ssd_nw"]]
    in_specs = [rs(SSD_INNER), rs(SSD_CONV_DIM), rs(LANES), rs(SSD_CONV_DIM), st_in] + _ssd_vec_specs()
    aliases = {}
    if new_prev is None:
        st_out = pl.BlockSpec((depth,) + tile, lambda i: (0, i, 0, 0, 0))
    else:
        st_out = st_in
        aliases[len(args)] = 1
        args.append(new_prev)
        in_specs.append(pl.BlockSpec(memory_space=pl.ANY))
    return pl.pallas_call(
        functools.partial(_ssd_sample_kernel, seq=l),
        grid=(b // SAMPLE_SEQS,),
        in_specs=in_specs,
        out_specs=[rs(SSD_INNER), st_out],
        out_shape=[jax.ShapeDtypeStruct((b * l, SSD_INNER), BF16),
                   jax.ShapeDtypeStruct((depth, b, SSD_HEADS, SSD_HEAD_DIM, SSD_STATE), F32)],
        input_output_aliases=aliases,
        compiler_params=_params(("arbitrary",)),
        name="ssd_sample",
    )(*args)


def _gla_consts():
    rk = np.arange(GLA_DK)[:, None] // GLA_HEAD_K
    cv = np.arange(GLA_DV)[None, :] // GLA_HEAD_V
    expand = (rk == cv).astype(np.float32)
    rv = np.arange(GLA_DV)[:, None] // GLA_HEAD_V
    seg = (rv == cv).astype(np.float32) / GLA_HEAD_V
    return jnp.asarray(expand, BF16), jnp.asarray(seg, BF16)


def _gla_prepare(gq, gk, small, wg, bg, seg01):
    glin = _dot(small.astype(BF16), wg) + bg
    g = -_softplus(-glin) * (1.0 / GLA_TAU)
    gc = _split3_dot(seg01, g)
    q = gq * (GLA_HEAD_K ** -0.5)
    return q, gk, gc


def _gla_pairwise(q, k, v, gc, expand, diag):
    rows = q.shape[0]
    nb = rows // diag
    q4 = q.reshape(nb, 1, diag, GLA_DK)
    g4 = gc.reshape(nb, 1, diag, GLA_DK)
    k4 = k.reshape(nb, diag, 1, GLA_DK)
    gs4 = gc.reshape(nb, diag, 1, GLA_DK)
    shape = (nb, diag, diag, GLA_DK)
    si = lax.broadcasted_iota(jnp.int32, shape, 1)
    ti = lax.broadcasted_iota(jnp.int32, shape, 2)
    w = jnp.exp(jnp.where(si <= ti, g4 - gs4, NEG))
    m = (q4 * k4 * w).reshape(nb * diag * diag, GLA_DK)
    p = _dot(m.astype(BF16), expand).reshape(nb, diag, diag, GLA_DV)
    o = jnp.sum(p * v.reshape(nb, diag, 1, GLA_DV), axis=1)
    return o.reshape(rows, GLA_DV)


def _gla_compact(s):
    rowhead = _iota2((GLA_DK, GLA_HEAD_V), 0) // GLA_HEAD_K
    out = s[:, 0:GLA_HEAD_V]
    for h in range(1, GLA_HEADS):
        out = jnp.where(rowhead == h, s[:, h * GLA_HEAD_V:(h + 1) * GLA_HEAD_V], out)
    return out


def _gla_finish(o, gr, seg, nw):
    ms = _split2_dot(o * o, seg)
    return o * lax.rsqrt(ms + EPS) * nw * _silu(gr)


def _head_stack(x, head_dim, heads):
    lane = _iota2(x.shape, 1) // head_dim
    return jnp.concatenate([jnp.where(lane == h, x, 0.0) for h in range(heads)], axis=0)


def _gla_prompt_kernel(gq_ref, gk_ref, gv_ref, gr_ref, small_ref, wg_ref, bg_ref, nw_ref, ex_ref, seg_ref,
                       o_ref, st_ref, s_scr):
    c = pl.program_id(1)
    rows = GLA_CHUNK

    @pl.when(c == 0)
    def _():
        s_scr[...] = jnp.zeros_like(s_scr)

    qi = _iota2((rows, rows), 0)
    si = _iota2((rows, rows), 1)
    seg01 = jnp.where(si <= qi, 1.0, 0.0).astype(BF16)
    bd = (_iota2((GLA_DK, GLA_DV), 0) // GLA_HEAD_K) == (_iota2((GLA_DK, GLA_DV), 1) // GLA_HEAD_V)
    s_all = s_scr[...]
    nch = gq_ref.shape[0] // rows
    prep = []
    for ci in range(nch):
        r0 = ci * rows
        q, k, gc = _gla_prepare(gq_ref[r0:r0 + rows, :], gk_ref[r0:r0 + rows, :], small_ref[r0:r0 + rows, :],
                                wg_ref[...], bg_ref[...], seg01)
        v = gv_ref[r0:r0 + rows, :]
        prep.append((q, k, gc, v, v.astype(BF16)))
    levels = []
    half = rows // 2
    while half >= GLA_DIAG:
        levels.append(half)
        half //= 2
    atts = {}
    for ci, (q, k, gc, v, vb) in enumerate(prep):
        for half in levels:
            for blk in range(rows // (2 * half)):
                s0 = blk * 2 * half
                t0 = s0 + half
                ref = gc[t0 - 1:t0, :]
                qs = q[t0:t0 + half, :] * jnp.exp(gc[t0:t0 + half, :] - ref)
                ks = k[s0:t0, :] * jnp.exp(ref - gc[s0:t0, :])
                atts[ci, half, blk] = _dot_nt(_head_stack(qs, GLA_HEAD_K, GLA_HEADS).astype(BF16),
                                              ks.astype(BF16)).astype(BF16)
    outs = [_gla_pairwise(q, k, v, gc, ex_ref[...], GLA_DIAG) for q, k, gc, v, vb in prep]
    for ci, (q, k, gc, v, vb) in enumerate(prep):
        for half in levels:
            vlane = _iota2((half, GLA_DV), 1) // GLA_HEAD_V
            pieces = []
            for blk in range(rows // (2 * half)):
                s0 = blk * 2 * half
                pv = _dot(atts[ci, half, blk], vb[s0:s0 + half, :])
                ot = jnp.zeros((half, GLA_DV), F32)
                for h in range(GLA_HEADS):
                    ot = jnp.where(vlane == h, pv[h * half:(h + 1) * half, :], ot)
                pieces += [jnp.zeros((half, GLA_DV), F32), ot]
            outs[ci] = outs[ci] + jnp.concatenate(pieces, axis=0)
    for ci, (q, k, gc, v, vb) in enumerate(prep):
        outs[ci] = outs[ci] + _dot((q * jnp.exp(gc)).astype(BF16), s_all.astype(BF16))
        gc_t = gc.T
        dcol = gc_t[:, rows - 1:rows]
        kd_t = k.T * jnp.exp(dcol - gc_t)
        upd = _dot(kd_t.astype(BF16), vb)
        s_all = jnp.exp(dcol) * s_all + jnp.where(bd, upd, 0.0)
    o_ref[...] = _gla_finish(jnp.concatenate(outs, axis=0), gr_ref[...], seg_ref[...],
                             nw_ref[...]).astype(o_ref.dtype)
    s_scr[...] = s_all

    @pl.when(c == pl.num_programs(1) - 1)
    def _():
        st_ref[0] = _gla_compact(s_all)


def _gla_vec_specs():
    return [_const_spec((LANES, GLA_DK)), _const_spec((1, GLA_DK)), _const_spec((1, GLA_DV)),
            _const_spec((GLA_DK, GLA_DV)), _const_spec((GLA_DV, GLA_DV))]


def _gla_prompt(gq, gk, gv, gr, small, lp, b, l):
    nc = l // GLA_STEP
    rs = lambda w: pl.BlockSpec((GLA_STEP, w), lambda bi, ci: (bi * nc + ci, 0))
    expand, seg = _gla_consts()
    return pl.pallas_call(
        _gla_prompt_kernel,
        grid=(b, nc),
        in_specs=[rs(GLA_DK), rs(GLA_DK), rs(GLA_DV), rs(GLA_DV), rs(LANES)] + _gla_vec_specs(),
        out_specs=[rs(GLA_DV), pl.BlockSpec((1, GLA_DK, GLA_HEAD_V), lambda bi, ci: (bi, 0, 0))],
        out_shape=[jax.ShapeDtypeStruct((b * l, GLA_DV), BF16),
                   jax.ShapeDtypeStruct((b, GLA_DK, GLA_HEAD_V), F32)],
        scratch_shapes=[pltpu.VMEM((GLA_DK, GLA_DV), F32)],
        compiler_params=_params(("arbitrary", "arbitrary")),
        name="gla_prompt",
    )(gq, gk, gv, gr, small, lp["gla_wg"], lp["gla_bg"], lp["gla_nw"], expand, seg)


def _gla_sample_kernel(gq_ref, gk_ref, gv_ref, gr_ref, small_ref, s0_ref, wg_ref, bg_ref, nw_ref, ex_ref, seg_ref,
                       o_ref, st_ref, *, seq):
    rows = gq_ref.shape[0]
    nseq = rows // seq
    qi = _iota2((rows, rows), 0)
    si = _iota2((rows, rows), 1)
    seg01 = jnp.where(((qi // seq) == (si // seq)) & (si <= qi), 1.0, 0.0).astype(BF16)
    last01 = jnp.where(si == (qi // seq) * seq + (seq - 1), 1.0, 0.0).astype(BF16)
    q, k, gc = _gla_prepare(gq_ref[...], gk_ref[...], small_ref[...], wg_ref[...], bg_ref[...], seg01)
    v = gv_ref[...]
    vb = v.astype(BF16)
    gc_last = _split3_dot(last01, gc)
    qg = (q * jnp.exp(gc)).astype(BF16)
    kd_t = (k * jnp.exp(gc_last - gc)).T
    dec_t = jnp.exp(gc_last).T
    colseq = _iota2((GLA_DK, rows), 1) // seq
    rowseq = _iota2((rows, GLA_DV), 0) // seq
    bd = (_iota2((GLA_DK, GLA_DV), 0) // GLA_HEAD_K) == (_iota2((GLA_DK, GLA_DV), 1) // GLA_HEAD_V)
    o = _gla_pairwise(q, k, v, gc, ex_ref[...], seq)
    for b in range(nseq):
        s0 = jnp.where(bd, jnp.concatenate([s0_ref[b]] * GLA_HEADS, axis=1), 0.0)
        o = o + jnp.where(rowseq == b, _dot(qg, s0.astype(BF16)), 0.0)
        upd = _dot(jnp.where(colseq == b, kd_t, 0.0).astype(BF16), vb)
        st_ref[b] = _gla_compact(dec_t[:, b * seq:b * seq + 1] * s0 + jnp.where(bd, upd, 0.0))
    o_ref[...] = _gla_finish(o, gr_ref[...], seg_ref[...], nw_ref[...]).astype(o_ref.dtype)


def _gla_sample(gq, gk, gv, gr, small, s0, lp, b, l):
    rows = SAMPLE_SEQS * l
    rs = lambda w: pl.BlockSpec((rows, w), lambda i: (i, 0))
    st = pl.BlockSpec((SAMPLE_SEQS, GLA_DK, GLA_HEAD_V), lambda i: (i, 0, 0))
    expand, seg = _gla_consts()
    return pl.pallas_call(
        functools.partial(_gla_sample_kernel, seq=l),
        grid=(b // SAMPLE_SEQS,),
        in_specs=[rs(GLA_DK), rs(GLA_DK), rs(GLA_DV), rs(GLA_DV), rs(LANES), st] + _gla_vec_specs(),
        out_specs=[rs(GLA_DV), st],
        out_shape=[jax.ShapeDtypeStruct((b * l, GLA_DV), BF16),
                   jax.ShapeDtypeStruct((b, GLA_DK, GLA_HEAD_V), F32)],
        compiler_params=_params(("arbitrary",)),
        name="gla_sample",
    )(gq, gk, gv, gr, small, s0, lp["gla_wg"], lp["gla_bg"], lp["gla_nw"], expand, seg)


def _att_window(qs, ks, vs, valids, prev):
    lane_half = _iota2(qs[0].shape, 1) // ATT_HEAD_DIM
    heads = [(p, half) for p in range(len(qs)) for half in range(2)]
    scores = [_dot_nt(jnp.where(lane_half == half, qs[p], 0.0).astype(BF16), ks[p]) for p, half in heads]
    probs, stats = [], []
    for (p, half), s in zip(heads, scores):
        lse_prev = prev[p][0]
        s = jnp.where(valids[p], s, NEG)
        smax = jnp.max(s, axis=-1, keepdims=True)
        mn = jnp.broadcast_to(smax, qs[p].shape) if lse_prev is None else jnp.maximum(lse_prev[half], smax)
        pr = jnp.exp2(s - jnp.concatenate([mn] * (s.shape[1] // LANES), axis=1))
        psum = jnp.sum(pr, axis=-1, keepdims=True)
        if lse_prev is None:
            alpha, ln = None, jnp.broadcast_to(psum, qs[p].shape)
        else:
            alpha = jnp.exp2(lse_prev[half] - mn)
            ln = alpha + psum
        probs.append(pr.astype(BF16))
        stats.append((mn, ln, alpha))
    pvs = [_dot(pr, vs[p]) for (p, half), pr in zip(heads, probs)]
    new = []
    for p in range(len(qs)):
        a_prev = prev[p][1]
        (m0, l0, al0), (m1, l1, al1) = stats[2 * p], stats[2 * p + 1]
        a0 = pvs[2 * p] if al0 is None else a_prev * al0 + pvs[2 * p]
        a1 = pvs[2 * p + 1] if al1 is None else a_prev * al1 + pvs[2 * p + 1]
        new.append(([m0, m1], [l0, l1], jnp.where(lane_half == 0, a0, a1)))
    return new


def _att_prompt_kernel(q_ref, k_ref, v_ref, out_ref, acc_scr, lse_scr):
    tq = ATT_BLOCK
    sup = out_ref.shape[1]
    base = pl.program_id(1) * sup
    rel = tq + _iota2((tq, 2 * tq), 0) - _iota2((tq, 2 * tq), 1)
    band = (rel >= 0) & (rel <= ATT_KEYS - 1)
    in_cur = _iota2((tq, 2 * tq), 1) >= tq
    strides = sorted((d for _, d in DILATION_PATTERNS), reverse=True)
    for idx, d in enumerate(strides):
        first, last = idx == 0, idx == len(strides) - 1

        def rows(start, d=d):
            return pl.ds(start, tq, stride=d) if d > 1 else pl.ds(start, tq)

        def body(it, carry, d=d, first=first, last=last, rows=rows):
            npair = ATT_HEADS // 2
            locs, qs, ks, vs, valids, prev = [], [], [], [], [], []
            for j in range(ATT_UNROLL):
                sb = it * ATT_UNROLL + j
                if d > 1:
                    r = sb % d
                    mi = sb // d
                    loc = r + d * tq * mi
                    start_q = base + loc
                else:
                    mi = sb
                    loc = pl.multiple_of(sb * tq, tq)
                    start_q = pl.multiple_of(base + loc, tq)
                mglob = base // (d * tq) + mi
                start_p = jnp.where(mglob == 0, start_q, start_q - d * tq)
                valid = band & ((mglob > 0) | in_cur)
                locs.append(loc)
                for p in range(npair):
                    if first:
                        prev.append((None, None))
                    else:
                        prev.append(([lse_scr[2 * p + hf, rows(loc), :] for hf in range(2)],
                                     acc_scr[p, rows(loc), :]))
                    qs.append((q_ref[p, rows(start_q), :] * (ATT_HEAD_DIM ** -0.5 * math.log2(math.e))).astype(BF16))
                    ks.append(jnp.concatenate([k_ref[p, rows(start_p), :], k_ref[p, rows(start_q), :]],
                                              axis=0).astype(BF16))
                    vs.append(jnp.concatenate([v_ref[p, rows(start_p), :], v_ref[p, rows(start_q), :]],
                                              axis=0).astype(BF16))
                    valids.append(valid)
            new = _att_window(qs, ks, vs, valids, prev)
            lane_half = _iota2((tq, LANES), 1) // ATT_HEAD_DIM
            for j, loc in enumerate(locs):
                for p in range(npair):
                    m_new, l_new, a_new = new[j * npair + p]
                    norm = a_new / jnp.where(lane_half == 0, l_new[0], l_new[1])
                    if last:
                        out_ref[p, rows(loc), :] = norm.astype(out_ref.dtype)
                    else:
                        for hf in range(2):
                            lse_scr[2 * p + hf, rows(loc), :] = m_new[hf] + jnp.log2(l_new[hf])
                        acc_scr[p, rows(loc), :] = norm
            return carry

        lax.fori_loop(0, sup // (tq * ATT_UNROLL), body, 0)


def _att_prompt(aq, ak, av, b, l):
    npair = ATT_HEADS // 2
    sup = ATT_BLOCK * max(d for _, d in DILATION_PATTERNS)
    assert l % sup == 0
    seq = pl.BlockSpec((npair, l, LANES), lambda bi, j: (0, bi, 0))
    return pl.pallas_call(
        _att_prompt_kernel,
        grid=(b, l // sup),
        in_specs=[seq, seq, seq],
        out_specs=pl.BlockSpec((npair, sup, LANES), lambda bi, j: (0, bi * (l // sup) + j, 0)),
        out_shape=jax.ShapeDtypeStruct((npair, b * l, LANES), BF16),
        scratch_shapes=[pltpu.VMEM((npair, sup, LANES), F32), pltpu.VMEM((ATT_HEADS, sup, LANES), F32)],
        compiler_params=_params(("arbitrary", "arbitrary")),
        name="att_prompt",
    )(aq, ak, av)


def _att_counts(seq, nbuf):
    qpos = nbuf + np.arange(seq)[:, None]
    kpos = np.arange(nbuf + seq)[None, :]
    delta = qpos - kpos
    cnt = np.zeros(delta.shape, np.float32)
    for window, stride in DILATION_PATTERNS:
        cnt += ((delta >= 0) & (delta % stride == 0) & (delta <= window)).astype(np.float32)
    cnt = np.tile(cnt, (2, 1))
    new = np.zeros((2 * seq, LANES), np.float32)
    new[:, :seq] = cnt[:, nbuf:]
    return jnp.asarray(cnt[:, :nbuf]), jnp.asarray(new)


def _att_sample_kernel(q_ref, kn_ref, vn_ref, kc_ref, vc_ref, cc_ref, cn_ref, out_ref):
    nseq = kc_ref.shape[0]
    seq = q_ref.shape[1] // nseq
    cc = cc_ref[...]
    cn = cn_ref[...]
    pad = jnp.zeros((LANES - seq, LANES), F32)
    lane_half = _iota2((seq, LANES), 1) // ATT_HEAD_DIM
    for b in range(nseq):
        rows = slice(b * seq, (b + 1) * seq)
        for p in range(ATT_HEADS // 2):
            q = q_ref[p, rows, :] * (ATT_HEAD_DIM ** -0.5)
            q2 = jnp.concatenate([jnp.where(lane_half == 0, q, 0.0), jnp.where(lane_half == 1, q, 0.0)],
                                 axis=0).astype(BF16)
            kt = kc_ref[b, 2 * p:2 * p + 2].reshape(LANES, -1).astype(BF16)
            vt = vc_ref[b, 2 * p:2 * p + 2].reshape(LANES, -1).astype(BF16)
            kn = jnp.concatenate([kn_ref[p, rows, :], pad], axis=0).astype(BF16)
            vn = jnp.concatenate([vn_ref[p, rows, :], pad], axis=0).astype(BF16)
            sc = jnp.where(cc > 0, _dot(q2, kt), NEG)
            sn = jnp.where(cn > 0, _dot_nt(q2, kn), NEG)
            m = jnp.maximum(jnp.max(sc, axis=-1, keepdims=True), jnp.max(sn, axis=-1, keepdims=True))
            pc = cc * jnp.exp(sc - m)
            pn = cn * jnp.exp(sn - m)
            den = jnp.sum(pc, axis=-1, keepdims=True) + jnp.sum(pn, axis=-1, keepdims=True)
            o = (_dot_nt(pc.astype(BF16), vt) + _dot(pn.astype(BF16), vn)) / den
            out_ref[p, rows, :] = jnp.where(lane_half == 0, o[0:seq, :], o[seq:2 * seq, :]).astype(out_ref.dtype)


def _att_sample(aq, ak, av, kcache_t, vcache_t, layer, b, l):
    npair = ATT_HEADS // 2
    nbuf = kcache_t.shape[-1]
    cc, cn = _att_counts(l, nbuf)
    ns = ATT_SAMPLE_SEQS
    assert b % ns == 0
    new = pl.BlockSpec((npair, ns * l, LANES), lambda i: (0, i, 0))
    cache = pl.BlockSpec((None, ns, ATT_HEADS, ATT_HEAD_DIM, nbuf), lambda i: (layer, i, 0, 0, 0))
    return pl.pallas_call(
        _att_sample_kernel,
        grid=(b // ns,),
        in_specs=[new, new, new, cache, cache, _const_spec((2 * l, nbuf)), _const_spec((2 * l, LANES))],
        out_specs=new,
        out_shape=jax.ShapeDtypeStruct((npair, b * l, LANES), BF16),
        compiler_params=_params(("arbitrary",)),
        name="att_sample",
    )(aq, ak, av, kcache_t, vcache_t, cc, cn)


def _pad_lanes(v, width, offset=0):
    out = jnp.zeros((1, width), F32)
    return out.at[0, offset:offset + v.shape[0]].set(v.astype(F32))


def _wt_kernel(w_ref, dt_ref, glr_ref, o_ref):
    g = pl.program_id(0)
    last = pl.num_programs(0) - 1
    depth = o_ref.shape[0]

    @pl.when(g < last)
    def _():
        for l in range(depth):
            o_ref[l] = w_ref[:, l, :].T.astype(o_ref.dtype)

    @pl.when(g == last)
    def _():
        row = _iota2((SUBLANES, D_MODEL), 0)
        pad = jnp.zeros((LANES - SMALL_GLR_OFF - GLA_GATE_RANK, D_MODEL), F32)
        for l in range(depth):
            dt = jnp.where(row < SSD_HEADS, dt_ref[:, l, :], 0.0)
            o_ref[l] = jnp.concatenate([dt, glr_ref[:, l, :], pad], axis=0).T.astype(o_ref.dtype)


def _proj_weights(w_in):
    depth = w_in.shape[0]
    assert SMALL_DT_OFF == 0 and SMALL_GLR_OFF == SUBLANES and SSD_HEADS <= SUBLANES
    wt = w_in.transpose(2, 0, 1)
    offs = dict(zip(("z", "xbc", "dt", "gq", "gk", "gv", "gr", "glr", "aq", "ak", "av"),
                    np.concatenate([[0], np.cumsum(IN_SPLITS)])[:-1]))
    src = [int(offs[n]) + i * LANES for n, w in PROJ_GROUPS if n != "small" for i in range(w // LANES)]
    steps = [(g, src[g] - src[g - 1] - LANES) for g in range(1, len(src)) if src[g] - src[g - 1] != LANES]
    assert src[0] == 0 and PROJ_GROUPS[-1][0] == "small"

    def w_map(g):
        off = LANES * g
        for g0, delta in steps:
            off = off + jnp.where(g >= g0, delta, 0)
        return (jnp.where(g < len(src), off, 0), 0, 0)

    el = lambda n: (pl.Element(n), pl.Element(depth), pl.Element(D_MODEL))
    return pl.pallas_call(
        _wt_kernel,
        grid=(len(src) + 1,),
        in_specs=[pl.BlockSpec(el(LANES), w_map),
                  pl.BlockSpec(el(SUBLANES), lambda g: (int(offs["dt"]), 0, 0)),
                  pl.BlockSpec(el(GLA_GATE_RANK), lambda g: (int(offs["glr"]), 0, 0))],
        out_specs=pl.BlockSpec((depth, D_MODEL, LANES), lambda g: (0, 0, g)),
        out_shape=jax.ShapeDtypeStruct((depth, D_MODEL, PROJ_WIDTH), BF16),
        compiler_params=_params(("arbitrary",)),
        name="proj_weights",
    )(wt, wt, wt)


def _layer_params(l, ssd_conv_w, ssd_conv_b, ssd_dt_bias, ssd_a_log, ssd_d, ssd_norm_w,
                  gla_w_gate, gla_b_gate, gla_norm_w, norm_w):
    wg = jnp.zeros((LANES, GLA_DK), F32).at[SMALL_GLR_OFF:SMALL_GLR_OFF + GLA_GATE_RANK, :].set(gla_w_gate[l])
    return dict(
        conv_w=ssd_conv_w[l], conv_b=ssd_conv_b[l].reshape(1, SSD_CONV_DIM),
        dtb=_pad_lanes(ssd_dt_bias[l], LANES, SMALL_DT_OFF), alog=_pad_lanes(ssd_a_log[l], LANES, SMALL_DT_OFF),
        dvec=jnp.repeat(ssd_d[l].astype(F32), SSD_HEAD_DIM).reshape(1, SSD_INNER),
        ssd_nw=ssd_norm_w[l].reshape(1, SSD_INNER),
        gla_wg=wg.astype(BF16), gla_bg=gla_b_gate[l].reshape(1, GLA_DK),
        gla_nw=jnp.tile(gla_norm_w[l], GLA_HEADS).reshape(1, GLA_DV),
        norm_w=norm_w[l],
    )


def _trunk(x, mods, layers, stacked, norm_f, states, sample):
    b, l, _ = x.shape
    depth = len(layers)
    keep = min(ATT_MAX_WINDOW, l)
    names = [n for n, _ in PROJ_GROUPS]
    ssd_list, conv_list, gla_list, k_list, v_list = [], [], [], [], []
    ssd_stack, kv_stack = None, None
    for li, lp in enumerate(layers):
        mod = mods
        x = _ffn(x, mod, 0, lp["norm_w"][0], stacked["ffn1_in"], stacked["ffn1_out"], li)
        if sample:
            proj = dict(zip(names, _inproj(x, mod, lp["norm_w"][1], stacked["w_in_p"], li)))
            st_ssd, st_conv, st_gla, kcache, vcache = states
            bufp = jnp.pad(st_conv[li], ((0, 0), (0, l - (SSD_CONV - 1)), (0, 0))).reshape(b * l, SSD_CONV_DIM)
            y, ssd_stack = _ssd_sample(proj["z"], proj["xbc"], proj["small"], bufp, st_ssd, li, lp, b, l,
                                       new_prev=ssd_stack)
            o, gla_new = _gla_sample(proj["gq"], proj["gk"], proj["gv"], proj["gr"], proj["small"],
                                     st_gla[li].reshape(b, GLA_DK, GLA_HEAD_V), lp, b, l)
            att = _att_sample(proj["aq"], proj["ak"], proj["av"], kcache, vcache, li, b, l)
            for acc, name in ((k_list, "ak"), (v_list, "av")):
                kv = proj[name].reshape(ATT_HEADS // 2, b, l, 2, ATT_HEAD_DIM)[:, :, l - keep:]
                acc.append(kv.transpose(1, 2, 0, 3, 4).reshape(b, keep, ATT_HEADS, ATT_HEAD_DIM))
        else:
            res = _inproj(x, mod, lp["norm_w"][1], stacked["w_in_p"], li, keep_t=keep, kv_prev=kv_stack)
            proj = dict(zip(names, res))
            kv_stack = tuple(res[len(names):])
            y, ssd_new = _ssd_prompt(proj["z"], proj["xbc"], proj["small"], lp, b, l)
            ssd_list.append(ssd_new)
            o, gla_new = _gla_prompt(proj["gq"], proj["gk"], proj["gv"], proj["gr"], proj["small"], lp, b, l)
            att = _att_prompt(proj["aq"], proj["ak"], proj["av"], b, l)
        x = _ffn(x, mod, 6, lp["norm_w"][2], stacked["ffn2_in"], stacked["ffn2_out"], li,
                 premix=(y, o, att, 5, stacked["w_out"]),
                 final_norm=norm_f if li == depth - 1 else None)
        conv_list.append(proj["xbc"].reshape(b, l, SSD_CONV_DIM)[:, l - (SSD_CONV - 1):])
        gla_list.append(gla_new.reshape(b, GLA_HEADS, GLA_HEAD_K, GLA_HEAD_V))
    if sample:
        ssd_out, k_out, v_out = ssd_stack, jnp.stack(k_list), jnp.stack(v_list)
    else:
        ssd_out = jnp.stack(ssd_list)
        k_out, v_out = (t.reshape(depth, b, ATT_HEADS, ATT_HEAD_DIM, keep).transpose(0, 1, 4, 2, 3) for t in kv_stack)
    return x, [ssd_out, jnp.stack(conv_list), jnp.stack(gla_list), k_out, v_out]


def kernel(x_prompt, x_sample, c_prompt, c_sample, state_ssd, state_ssd_conv, state_gla, cache_attn_k, cache_attn_v,
           w_in, w_out, ssd_conv_w, ssd_conv_b, ssd_dt_bias, ssd_a_log, ssd_d, ssd_norm_w,
           gla_w_gate, gla_b_gate, gla_norm_w, norm_w, w_mod, b_mod,
           ffn1_w_in, ffn1_w_out, ffn2_w_in, ffn2_w_out, norm_f):
    bp, bs = x_prompt.shape[0], x_sample.shape[0]
    depth = w_in.shape[0]
    layers = [_layer_params(l, ssd_conv_w, ssd_conv_b, ssd_dt_bias, ssd_a_log, ssd_d, ssd_norm_w,
                            gla_w_gate, gla_b_gate, gla_norm_w, norm_w) for l in range(depth)]
    stacked = dict(w_in_p=_proj_weights(w_in), w_out=w_out.astype(BF16), ffn1_in=ffn1_w_in.astype(BF16), ffn1_out=ffn1_w_out.astype(BF16),
                   ffn2_in=ffn2_w_in.astype(BF16), ffn2_out=ffn2_w_out.astype(BF16))
    mods_s, mods_p = _modulation([c_sample, c_prompt], w_mod, b_mod)
    kcache = cache_attn_k.transpose(0, 1, 3, 4, 2)
    vcache = cache_attn_v.transpose(0, 1, 3, 4, 2)
    y_p, (ssd_p, conv_p, gla_p, k_p, v_p) = _trunk(x_prompt, mods_p, layers, stacked, norm_f, None, sample=False)
    y_s, (ssd_s, conv_s, gla_s, k_s, v_s) = _trunk(
        x_sample, mods_s, layers, stacked, norm_f, (state_ssd, state_ssd_conv, state_gla, kcache, vcache),
        sample=True)
    return (y_p, y_s, ssd_p, ssd_s, conv_p, conv_s, gla_p, gla_s, k_p, k_s, v_p, v_s)
```
